```python
import math
import jax, jax.numpy as jnp
from jax import lax
import numpy as np

D_MODEL = 1024
BATCH = 4
SEQ = 4096
DEPTH = 2
DEC_BATCH = 32
DEC_SEQ = 8
PAST_LEN = 8192
PAGE_SIZE = 128

HEAD_DIM = 64
NSA_HEADS = 8
MOBA_HEADS = 8
KV_HEADS = 2
NSA_WIDTH = NSA_HEADS * HEAD_DIM
MOBA_WIDTH = MOBA_HEADS * HEAD_DIM
KV_WIDTH = KV_HEADS * HEAD_DIM
MIX_WIDTH = NSA_WIDTH + MOBA_WIDTH
ROT_DIM = HEAD_DIM // 4
ROPE_THETA = 500000.0
CMP_LEN = 32
CMP_STRIDE = 16
CMP_HIDDEN = 256
SEL_BLOCK = 64
SEL_TOPK = 8
WINDOW = 512
MOBA_BLOCK = 256
MOBA_TOPK = 3
PLE_DIM = 256
N_PAGED_SLOTS = 6
NSA_QBLK = 128
MOBA_QBLK = 16
RMS_EPS = 1e-6
NEG_INF = -1e30
FORCE_SCORE = 1e4
IN_SIZES = (NSA_WIDTH, 6 * KV_WIDTH, 3 * NSA_HEADS, NSA_WIDTH, MOBA_WIDTH, 2 * KV_WIDTH, MOBA_WIDTH)
IN_COLS = 2 * NSA_WIDTH + 8 * KV_WIDTH + 3 * NSA_HEADS + 2 * MOBA_WIDTH

kernel_name = 'hybrid_nsa_moba_decoder_step'


def rms_norm(x, g):
    xf = x.astype(jnp.float32)
    y = xf * lax.rsqrt(jnp.mean(xf * xf, axis=-1, keepdims=True) + RMS_EPS)
    return (y * g.astype(jnp.float32)).astype(x.dtype)


def rope_partial(x, pos):
    half = ROT_DIM // 2
    inv_freq = ROPE_THETA ** (-jnp.arange(half, dtype=jnp.float32) / half)
    ang = pos.astype(jnp.float32)[:, None] * inv_freq[None, :]
    cos = jnp.cos(ang)[None, :, None, :]
    sin = jnp.sin(ang)[None, :, None, :]
    xr = x[..., :ROT_DIM].astype(jnp.float32)
    x1, x2 = xr[..., :half], xr[..., half:]
    rot = jnp.concatenate([x1 * cos - x2 * sin, x2 * cos + x1 * sin], axis=-1)
    return jnp.concatenate([rot.astype(x.dtype), x[..., ROT_DIM:]], axis=-1)


def masked_softmax(s, mask):
    s = jnp.where(mask, s.astype(jnp.float32), NEG_INF)
    return jnp.where(mask, jax.nn.softmax(s, axis=-1), 0.0)


def softmax_last2(s, mask):
    shp = s.shape
    flat = shp[:-2] + (shp[-2] * shp[-1],)
    m = jnp.broadcast_to(mask, shp).reshape(flat)
    return masked_softmax(s.reshape(flat), m).reshape(shp)


def to_groups(a):
    B, T, H, E = a.shape
    return a.reshape(B, T, KV_HEADS, H // KV_HEADS, E).transpose(0, 2, 1, 3, 4)


def from_groups(a):
    B, K, T, G, E = a.shape
    return a.transpose(0, 2, 1, 3, 4).reshape(B, T, K * G, E)


def to_blocks(k, size):
    B, L, K, E = k.shape
    return k.reshape(B, L // size, size, K, E).transpose(0, 3, 1, 2, 4)


def gather_blocks(blocks, idx):
    return jax.vmap(jax.vmap(lambda a, i: a[i]))(blocks, idx)


def compress(k, pos_emb, w1, w2):
    B, L, K, E = k.shape
    chunks = k.reshape(B, L // CMP_STRIDE, CMP_STRIDE, K, E)
    blocks = jnp.concatenate([chunks[:, :-1], chunks[:, 1:]], axis=2) + pos_emb[None, None, :, None, :]
    hid = jax.nn.silu(jnp.einsum('bclkd,ldf->bckf', blocks, w1))
    return jnp.einsum('bckf,fd->bckd', hid, w2)


def selection_map(n_cmp, n_sel):
    c0 = np.arange(n_cmp)[:, None] * CMP_STRIDE
    j0 = np.arange(n_sel)[None, :] * SEL_BLOCK
    ov = np.clip(np.minimum(c0 + CMP_LEN, j0 + SEL_BLOCK) - np.maximum(c0, j0), 0, None)
    return jnp.asarray(ov / CMP_STRIDE, dtype=jnp.float32)


def over_query_blocks(fn, qblk, qpos, *q_arrays):
    T = qpos.shape[0]
    qblk = math.gcd(qblk, T)

    def body(i):
        s = i * qblk
        return fn(lax.dynamic_slice_in_dim(qpos, s, qblk),
                  *[lax.dynamic_slice_in_dim(a, s, qblk, axis=1) for a in q_arrays])

    out = jnp.moveaxis(lax.map(body, jnp.arange(T // qblk)), 0, 1)
    return out.reshape((out.shape[0], T) + out.shape[3:])


def nsa_block(qpos, q, q_rot, gate, kc, vc, ks_blk, vs_blk, win_k, win_v, win_start, sel_map):
    Tq = qpos.shape[0]
    scale = HEAD_DIM ** -0.5
    qg, qr = to_groups(q), to_groups(q_rot)
    c_end = jnp.arange(kc.shape[1]) * CMP_STRIDE + CMP_LEN
    c_mask = (c_end[None, :] <= qpos[:, None] + 1)[None, None, :, None, :]
    p_cmp = masked_softmax(jnp.einsum('bktgd,bckd->bktgc', qg, kc) * scale, c_mask)
    o_cmp = jnp.einsum('bktgc,bckd->bktgd', p_cmp.astype(vc.dtype), vc)
    n_sel = ks_blk.shape[2]
    imp = jnp.einsum('bktc,cj->bktj', p_cmp.sum(axis=3), sel_map)
    blk = jnp.arange(n_sel)[None, :]
    jt = (qpos // SEL_BLOCK)[:, None]
    score = jnp.where((blk == jt) | (blk == 0), FORCE_SCORE, jnp.where(blk < jt, imp, NEG_INF))
    top_s, idx = lax.top_k(score, min(SEL_TOPK, n_sel))
    kpos = idx[..., None] * SEL_BLOCK + jnp.arange(SEL_BLOCK)
    s_mask = (top_s > 0.5 * NEG_INF)[..., None] & (kpos <= qpos[None, None, :, None, None])
    kg, vg = gather_blocks(ks_blk, idx), gather_blocks(vs_blk, idx)
    p_sel = softmax_last2(jnp.einsum('bktgd,bktnsd->bktgns', qr, kg) * scale, s_mask[:, :, :, None])
    o_sel = jnp.einsum('bktgns,bktnsd->bktgd', p_sel.astype(vg.dtype), vg)
    q0 = qpos[0]
    span = WINDOW + Tq
    start = q0 - WINDOW - win_start
    wk = lax.dynamic_slice_in_dim(win_k, start, span, axis=1)
    wv = lax.dynamic_slice_in_dim(win_v, start, span, axis=1)
    wpos = (q0 - WINDOW + jnp.arange(span))[None, :]
    w_mask = (wpos >= 0) & (wpos <= qpos[:, None]) & (wpos >= qpos[:, None] - WINDOW)
    p_win = masked_softmax(jnp.einsum('bktgd,bskd->bktgs', qr, wk) * scale, w_mask[None, None, :, None, :])
    o_win = jnp.einsum('bktgs,bskd->bktgd', p_win.astype(wv.dtype), wv)
    g = to_groups(gate)
    return from_groups(g[..., 0:1] * o_cmp + g[..., 1:2] * o_sel + g[..., 2:3] * o_win)


def moba_block(qpos, q, kb_blk, vb_blk, k_mean):
    scale = HEAD_DIM ** -0.5
    qg = to_groups(q)
    nb = kb_blk.shape[2]
    bt = qpos // MOBA_BLOCK
    past = (jnp.arange(nb)[None, :] < bt[:, None])[None, None, :, None, :]
    gate_s = jnp.where(past, jnp.einsum('bktgd,bknd->bktgn', qg.astype(jnp.float32), k_mean), NEG_INF)
    top_s, top_i = lax.top_k(gate_s, min(MOBA_TOPK, nb))
    own = jnp.broadcast_to(bt[None, None, :, None, None], top_i.shape[:-1] + (1,)).astype(top_i.dtype)
    idx = jnp.concatenate([top_i, own], axis=-1)
    valid = jnp.concatenate([top_s > 0.5 * NEG_INF, jnp.ones(own.shape, dtype=bool)], axis=-1)
    kpos = idx[..., None] * MOBA_BLOCK + jnp.arange(MOBA_BLOCK)
    mask = valid[..., None] & (kpos <= qpos[None, None, :, None, None, None])
    kg, vg = gather_blocks(kb_blk, idx), gather_blocks(vb_blk, idx)
    p = softmax_last2(jnp.einsum('bktgd,bktgnsd->bktgns', qg, kg) * scale, mask)
    return from_groups(jnp.einsum('bktgns,bktgnsd->bktgd', p.astype(vg.dtype), vg))


def mixer_layer(x, ple, pos, past_kv, win_prev, win_start, g_mix, w_in, w_out,
                cmp_pos_k, cmp_w1_k, cmp_w2_k, cmp_pos_v, cmp_w1_v, cmp_w2_v,
                g_ple, w_ple_gate, w_ple_proj):
    B, T, _ = x.shape
    h = rms_norm(x, g_mix)
    q_a, kv_a, gate_a, z_a, q_b, kv_b, z_b = jnp.split(h @ w_in, np.cumsum(IN_SIZES)[:-1].tolist(), axis=-1)
    q_a = q_a.reshape(B, T, NSA_HEADS, HEAD_DIM)
    q_b = rope_partial(q_b.reshape(B, T, MOBA_HEADS, HEAD_DIM), pos)
    kv_a = kv_a.reshape(B, T, 6, KV_HEADS, HEAD_DIM)
    kv_b = kv_b.reshape(B, T, 2, KV_HEADS, HEAD_DIM)
    gate_a = jax.nn.sigmoid(gate_a.astype(jnp.float32)).astype(x.dtype).reshape(B, T, NSA_HEADS, 3)
    new_paged = jnp.stack([kv_a[:, :, 0], kv_a[:, :, 1], rope_partial(kv_a[:, :, 2], pos), kv_a[:, :, 3],
                           rope_partial(kv_b[:, :, 0], pos), kv_b[:, :, 1]], axis=2)
    new_win = jnp.stack([rope_partial(kv_a[:, :, 4], pos), kv_a[:, :, 5]], axis=2)
    full = new_paged if past_kv is None else jnp.concatenate([past_kv, new_paged], axis=1)
    L = full.shape[1]
    Lp = -(-L // MOBA_BLOCK) * MOBA_BLOCK
    full = jnp.pad(full, ((0, 0), (0, Lp - L), (0, 0), (0, 0), (0, 0)))
    win = jnp.concatenate([win_prev, new_win], axis=1)
    win_k, win_v = win[:, :, 0], win[:, :, 1]
    kc = compress(full[:, :, 0], cmp_pos_k, cmp_w1_k, cmp_w2_k)
    vc = compress(full[:, :, 1], cmp_pos_v, cmp_w1_v, cmp_w2_v)
    ks_blk, vs_blk = to_blocks(full[:, :, 2], SEL_BLOCK), to_blocks(full[:, :, 3], SEL_BLOCK)
    kb_blk, vb_blk = to_blocks(full[:, :, 4], MOBA_BLOCK), to_blocks(full[:, :, 5], MOBA_BLOCK)
    k_mean = jnp.mean(kb_blk.astype(jnp.float32), axis=3)
    sel_map = selection_map(kc.shape[1], ks_blk.shape[2])
    o_a = over_query_blocks(
        lambda qp, q, qr, g: nsa_block(qp, q, qr, g, kc, vc, ks_blk, vs_blk, win_k, win_v, win_start, sel_map),
        NSA_QBLK, pos, q_a, rope_partial(q_a, pos), gate_a)
    o_b = over_query_blocks(lambda qp, q: moba_block(qp, q, kb_blk, vb_blk, k_mean), MOBA_QBLK, pos, q_b)
    mixed = jnp.concatenate([o_a.reshape(B, T, NSA_WIDTH) * jax.nn.silu(z_a),
                             o_b.reshape(B, T, MOBA_WIDTH) * jax.nn.silu(z_b)], axis=-1)
    x = x + mixed @ w_out
    x = x + jax.nn.sigmoid(rms_norm(x, g_ple) @ w_ple_gate) * (ple @ w_ple_proj)
    return x, new_paged, new_win


def setup_inputs(seed: int = 0) -> dict:
    key = jax.random.key(seed)
    ks = jax.random.split(key, 20)
    n_pages = PAST_LEN // PAGE_SIZE
    n_used = DEC_BATCH * n_pages
    n_pool = n_used + n_used // 4
    wbuf = min(WINDOW, PAST_LEN)

    def nrm(k, shape, scale=1.0):
        return scale * jax.random.normal(k, shape, jnp.float32)

    page_table = jax.random.permutation(ks[4], n_pool)[:n_used].reshape(DEC_BATCH, n_pages).astype(jnp.int32)
    return {
        'x_prompt': nrm(ks[0], (BATCH, SEQ, D_MODEL)),
        'x_sample': nrm(ks[1], (DEC_BATCH, DEC_SEQ, D_MODEL)),
        'cache_paged_kv': nrm(ks[2], (DEPTH, n_pool, PAGE_SIZE, N_PAGED_SLOTS, KV_HEADS, HEAD_DIM)),
        'cache_win_kv': nrm(ks[3], (DEPTH, DEC_BATCH, wbuf, 2, KV_HEADS, HEAD_DIM)),
        'page_table': page_table,
        'p_prompt': nrm(ks[5], (DEPTH, BATCH, SEQ, PLE_DIM)),
        'p_sample': nrm(ks[6], (DEPTH, DEC_BATCH, DEC_SEQ, PLE_DIM)),
        'g_mix': 1.0 + nrm(ks[7], (DEPTH, D_MODEL), 0.02),
        'w_in': nrm(ks[8], (DEPTH, D_MODEL, IN_COLS), D_MODEL ** -0.5),
        'w_out': nrm(ks[9], (DEPTH, MIX_WIDTH, D_MODEL), MIX_WIDTH ** -0.5),
        'cmp_pos_k': nrm(ks[10], (DEPTH, CMP_LEN, HEAD_DIM), 0.02),
        'cmp_w1_k': nrm(ks[11], (DEPTH, CMP_LEN, HEAD_DIM, CMP_HIDDEN), (CMP_LEN * HEAD_DIM) ** -0.5),
        'cmp_w2_k': nrm(ks[12], (DEPTH, CMP_HIDDEN, HEAD_DIM), CMP_HIDDEN ** -0.5),
        'cmp_pos_v': nrm(ks[13], (DEPTH, CMP_LEN, HEAD_DIM), 0.02),
        'cmp_w1_v': nrm(ks[14], (DEPTH, CMP_LEN, HEAD_DIM, CMP_HIDDEN), (CMP_LEN * HEAD_DIM) ** -0.5),
        'cmp_w2_v': nrm(ks[15], (DEPTH, CMP_HIDDEN, HEAD_DIM), CMP_HIDDEN ** -0.5),
        'g_ple': 1.0 + nrm(ks[16], (DEPTH, D_MODEL), 0.02),
        'w_ple_gate': nrm(ks[17], (DEPTH, D_MODEL, D_MODEL), D_MODEL ** -0.5),
        'w_ple_proj': nrm(ks[18], (DEPTH, PLE_DIM, D_MODEL), PLE_DIM ** -0.5),
        'g_final': 1.0 + nrm(ks[19], (D_MODEL,), 0.02),
    }


def reference(x_prompt, x_sample, cache_paged_kv, cache_win_kv, page_table, p_prompt, p_sample,
              g_mix, w_in, w_out, cmp_pos_k, cmp_w1_k, cmp_w2_k, cmp_pos_v, cmp_w1_v, cmp_w2_v,
              g_ple, w_ple_gate, w_ple_proj, g_final):
    bp, tp = x_prompt.shape[0], x_prompt.shape[1]
    bs, ts = x_sample.shape[0], x_sample.shape[1]
    past_len = page_table.shape[1] * PAGE_SIZE
    wbuf = cache_win_kv.shape[2]
    pos_p = jnp.arange(tp, dtype=jnp.int32)
    pos_s = past_len + jnp.arange(ts, dtype=jnp.int32)
    xp, xs = x_prompt, x_sample
    new_pp, new_pw, new_sp, new_sw = [], [], [], []
    for i in range(DEPTH):
        w = (g_mix[i], w_in[i], w_out[i], cmp_pos_k[i], cmp_w1_k[i], cmp_w2_k[i],
             cmp_pos_v[i], cmp_w1_v[i], cmp_w2_v[i], g_ple[i], w_ple_gate[i], w_ple_proj[i])
        win0 = jnp.zeros((bp, WINDOW, 2, KV_HEADS, HEAD_DIM), xp.dtype)
        xp, pk, pw = mixer_layer(xp, p_prompt[i], pos_p, None, win0, -WINDOW, *w)
        past = cache_paged_kv[i, page_table].reshape(bs, past_len, N_PAGED_SLOTS, KV_HEADS, HEAD_DIM)
        win_prev = jnp.concatenate([jnp.zeros((bs, WINDOW - wbuf, 2, KV_HEADS, HEAD_DIM), xs.dtype),
                                    cache_win_kv[i]], axis=1)
        xs, sk, sw = mixer_layer(xs, p_sample[i], pos_s, past, win_prev, past_len - WINDOW, *w)
        new_pp.append(pk)
        new_pw.append(pw[:, tp - min(WINDOW, tp):])
        new_sp.append(sk)
        new_sw.append(sw)
    y_prompt = rms_norm(xp, g_final)
    y_sample = rms_norm(xs, g_final)
    return (y_prompt, y_sample, jnp.stack(new_pp), jnp.stack(new_pw), jnp.stack(new_sp), jnp.stack(new_sw))
```

```python
import functools

import numpy as np
import jax
import jax.numpy as jnp
from jax import lax
from jax.experimental import pallas as pl
from jax.experimental.pallas import tpu as pltpu

HEAD_DIM = 64
NSA_HEADS = 8
MOBA_HEADS = 8
KV_HEADS = 2
GROUP = NSA_HEADS // KV_HEADS
ROT_DIM = HEAD_DIM // 4
ROPE_THETA = 500000.0
CMP_LEN = 32
CMP_STRIDE = 16
SEL_BLOCK = 64
SEL_TOPK = 8
WINDOW = 512
MOBA_BLOCK = 256
MOBA_TOPK = 3
N_PAGED_SLOTS = 6
RMS_EPS = 1e-6
NEG_INF = -1e30
FORCE_SCORE = 1e4
SCALE = HEAD_DIM ** -0.5
SEL_SHIFT = SEL_BLOCK.bit_length() - 1
MOBA_SHIFT = MOBA_BLOCK.bit_length() - 1

LANES = 128
KV_WIDTH = KV_HEADS * HEAD_DIM
NSA_WIDTH = NSA_HEADS * HEAD_DIM
MOBA_WIDTH = MOBA_HEADS * HEAD_DIM
PAGED_WIDTH = N_PAGED_SLOTS * KV_WIDTH
KEY_TILE = 256
VMEM_LIMIT = 56 * 1024 * 1024

F32 = jnp.float32
BF16 = jnp.bfloat16
NT_DIMS = (((1,), (1,)), ((), ()))


def _round_up(n, m):
    return -(-n // m) * m


def _params(sem):
    return pltpu.CompilerParams(dimension_semantics=sem, vmem_limit_bytes=VMEM_LIMIT)


def _rms(x, g):
    return x * lax.rsqrt(jnp.mean(x * x, axis=-1, keepdims=True) + RMS_EPS) * g


def _sigmoid(x):
    return 1.0 / (1.0 + jnp.exp(-x))


def _dot(a, b):
    return jnp.dot(a, b, preferred_element_type=F32)


def _dot_nt(a, b):
    return lax.dot_general(a, b, NT_DIMS, preferred_element_type=F32)


def _rope128(v, c, sp, sm):
    half = ROT_DIM // 2
    return v * c + pltpu.roll(v, half, axis=1) * sp + pltpu.roll(v, LANES - half, axis=1) * sm


def _rope(v, c, sp, sm):
    n = v.shape[-1] // LANES
    return jnp.concatenate([_rope128(v[:, i * LANES:(i + 1) * LANES], c, sp, sm) for i in range(n)], axis=-1)


def _proj_kernel(x_ref, g_ref, c_ref, sp_ref, sm_ref, wqa, wkva, wgate, wza, wqb, wkvb, wzb,
                 qa_o, qar_o, paged_o, win_o, gate_o, za_o, qb_o, zb_o):
    h = _rms(x_ref[...], g_ref[...]).astype(BF16)
    c, sp, sm = c_ref[...], sp_ref[...], sm_ref[...]
    qa = _dot(h, wqa[...])
    qa_o[...] = qa
    qar_o[...] = _rope(qa, c, sp, sm)
    kva = _dot(h, wkva[...])
    paged_o[:, 0:2 * KV_WIDTH] = kva[:, 0:2 * KV_WIDTH]
    paged_o[:, 2 * KV_WIDTH:3 * KV_WIDTH] = _rope(kva[:, 2 * KV_WIDTH:3 * KV_WIDTH], c, sp, sm)
    paged_o[:, 3 * KV_WIDTH:4 * KV_WIDTH] = kva[:, 3 * KV_WIDTH:4 * KV_WIDTH]
    win_o[:, 0:KV_WIDTH] = _rope(kva[:, 4 * KV_WIDTH:5 * KV_WIDTH], c, sp, sm)
    win_o[:, KV_WIDTH:2 * KV_WIDTH] = kva[:, 5 * KV_WIDTH:6 * KV_WIDTH]
    kvb = _dot(h, wkvb[...])
    paged_o[:, 4 * KV_WIDTH:5 * KV_WIDTH] = _rope(kvb[:, 0:KV_WIDTH], c, sp, sm)
    paged_o[:, 5 * KV_WIDTH:6 * KV_WIDTH] = kvb[:, KV_WIDTH:2 * KV_WIDTH]
    gate_o[...] = _sigmoid(_dot(h, wgate[...]))
    za = _dot(h, wza[...])
    za_o[...] = za * _sigmoid(za)
    qb_o[...] = _rope(_dot(h, wqb[...]), c, sp, sm)
    zb = _dot(h, wzb[...])
    zb_o[...] = zb * _sigmoid(zb)


def _rope_tables(pos, n_rows):
    half = ROT_DIM // 2
    inv_freq = ROPE_THETA ** (-jnp.arange(half, dtype=F32) / half)
    ang = pos.astype(F32)[:, None] * inv_freq[None, :]
    cos, sin = jnp.cos(ang), jnp.sin(ang)
    d = np.arange(LANES) % HEAD_DIM
    idx = d % half
    c = jnp.where(d < ROT_DIM, cos[:, idx], 1.0)
    sp = jnp.where((d >= half) & (d < ROT_DIM), sin[:, idx], 0.0)
    sm = jnp.where(d < half, -sin[:, idx], 0.0)
    rep = n_rows // pos.shape[0]
    return tuple(jnp.tile(t, (rep, 1)) for t in (c, sp, sm))


def _proj(x2d, pos, g, w):
    n, d_model = x2d.shape
    t = pos.shape[0]
    tm = min(256, n)
    tab_rows = max(t, tm)
    tabs = _rope_tables(pos, tab_rows)
    n_tab = tab_rows // tm
    row = lambda width: pl.BlockSpec((tm, width), lambda i: (i, 0))
    tab = pl.BlockSpec((tm, LANES), lambda i: (i % n_tab, 0))
    full = lambda a: pl.BlockSpec(a.shape, lambda i: (0, 0))
    widths = (NSA_WIDTH, NSA_WIDTH, PAGED_WIDTH, 2 * KV_WIDTH, LANES, NSA_WIDTH, MOBA_WIDTH, MOBA_WIDTH)
    return pl.pallas_call(
        _proj_kernel,
        grid=(n // tm,),
        in_specs=[row(d_model), full(g), tab, tab, tab] + [full(a) for a in w],
        out_specs=[row(wd) for wd in widths],
        out_shape=[jax.ShapeDtypeStruct((n, wd), F32) for wd in widths],
        compiler_params=_params(("parallel",)),
        name="proj",
    )(x2d, g, *tabs, *w)


def _gather_kernel(pt_ref, pool_ref, new_ref, o_ref, *, n_pages, t_new):
    p = pl.program_id(1)

    @pl.when(p < n_pages)
    def _():
        o_ref[...] = pool_ref[...]

    @pl.when(p == n_pages)
    def _():
        o_ref[...] = jnp.zeros(o_ref.shape, F32)
        o_ref[0, 0:t_new, :] = new_ref[0]

    @pl.when(p > n_pages)
    def _():
        o_ref[...] = jnp.zeros(o_ref.shape, F32)


def _gather(pool, page_table, page_base, new_rows, l_pad):
    bs, n_pages = page_table.shape
    page = pool.shape[1]
    t_new = new_rows.shape[1]
    grid_spec = pltpu.PrefetchScalarGridSpec(
        num_scalar_prefetch=1,
        grid=(bs, l_pad // page),
        in_specs=[
            pl.BlockSpec((1, page, PAGED_WIDTH),
                         lambda b, p, pt: (page_base + pt[b, jnp.minimum(p, n_pages - 1)], 0, 0)),
            pl.BlockSpec((1, t_new, PAGED_WIDTH), lambda b, p, pt: (b, 0, 0)),
        ],
        out_specs=pl.BlockSpec((1, page, PAGED_WIDTH), lambda b, p, pt: (b, p, 0)),
    )
    return pl.pallas_call(
        functools.partial(_gather_kernel, n_pages=n_pages, t_new=t_new),
        grid_spec=grid_spec,
        out_shape=jax.ShapeDtypeStruct((bs, l_pad, PAGED_WIDTH), F32),
        compiler_params=_params(("parallel", "arbitrary")),
        name="gather",
    )(page_table, pool, new_rows)


def _compress_kernel(src_ref, pos_ref, w1_ref, w2_ref, o_ref, *, n_chunks):
    m = n_chunks
    hid2 = w2_ref.shape[0]
    xs = [src_ref[0, pl.ds(l, m, stride=CMP_STRIDE), :].astype(BF16) for l in range(CMP_STRIDE)]
    xcat = jnp.concatenate(xs, axis=-1)
    ab = _dot(xcat, w1_ref[...])
    pb = _dot(pos_ref[...], w1_ref[...])
    bias = pb[0:1, 0:hid2] + pb[1:2, hid2:2 * hid2]
    nxt = pltpu.roll(ab[:, hid2:2 * hid2], m - 1, axis=0)
    pre = ab[:, 0:hid2] + nxt + bias
    hid = pre * _sigmoid(pre)
    out = _dot(hid.astype(BF16), w2_ref[...])
    o_ref[0, 0:m, :] = out
    if o_ref.shape[1] > m:
        o_ref[0, m:, :] = jnp.zeros((o_ref.shape[1] - m, LANES), F32)


def _compress(full, slot, pos_rows, w1, w2, nc_pad):
    b, l, _ = full.shape
    n_chunks = l // CMP_STRIDE
    return pl.pallas_call(
        functools.partial(_compress_kernel, n_chunks=n_chunks),
        grid=(b,),
        in_specs=[
            pl.BlockSpec((1, l, KV_WIDTH), lambda i: (i, 0, slot)),
            pl.BlockSpec(pos_rows.shape, lambda i: (0, 0)),
            pl.BlockSpec(w1.shape, lambda i: (0, 0)),
            pl.BlockSpec(w2.shape, lambda i: (0, 0)),
        ],
        out_specs=pl.BlockSpec((1, nc_pad, KV_WIDTH), lambda i: (i, 0, 0)),
        out_shape=jax.ShapeDtypeStruct((b, nc_pad, KV_WIDTH), F32),
        compiler_params=_params(("parallel",)),
        name="compress",
    )(full, pos_rows, w1, w2)


def _compress_weights(pos_emb, w1, w2):
    hidden = w1.shape[-1]
    eye = jnp.eye(KV_HEADS, dtype=F32)

    def half(w):
        return jnp.einsum('ldf,hg->lhdgf', w, eye).reshape(CMP_STRIDE * KV_WIDTH, KV_HEADS * hidden)

    w1bd = jnp.concatenate([half(w1[:CMP_STRIDE]), half(w1[CMP_STRIDE:])], axis=1).astype(BF16)
    w2bd = jnp.einsum('fd,hg->hfgd', w2, eye).reshape(KV_HEADS * hidden, KV_WIDTH).astype(BF16)

    def pos_row(p):
        return jnp.broadcast_to(p[:, None, :], (CMP_STRIDE, KV_HEADS, HEAD_DIM)).reshape(-1)

    pos_rows = jnp.zeros((8, CMP_STRIDE * KV_WIDTH), F32)
    pos_rows = pos_rows.at[0].set(pos_row(pos_emb[:CMP_STRIDE])).at[1].set(pos_row(pos_emb[CMP_STRIDE:]))
    return pos_rows.astype(BF16), w1bd, w2bd


def _kmean_kernel(src_ref, o_ref, *, n_blocks):
    x = src_ref[0].reshape(n_blocks, MOBA_BLOCK, KV_WIDTH)
    o_ref[0, 0:n_blocks, :] = jnp.sum(x, axis=1) * (1.0 / MOBA_BLOCK)
    o_ref[0, n_blocks:, :] = jnp.zeros((o_ref.shape[1] - n_blocks, KV_WIDTH), F32)


def _kmean(full, nb_pad):
    b, l, _ = full.shape
    return pl.pallas_call(
        functools.partial(_kmean_kernel, n_blocks=l // MOBA_BLOCK),
        grid=(b,),
        in_specs=[pl.BlockSpec((1, l, KV_WIDTH), lambda i: (i, 0, 4))],
        out_specs=pl.BlockSpec((1, nb_pad, KV_WIDTH), lambda i: (i, 0, 0)),
        out_shape=jax.ShapeDtypeStruct((b, nb_pad, KV_WIDTH), F32),
        compiler_params=_params(("parallel",)),
        name="kmean",
    )(full)


def _stack_heads(ref, kvh):
    return jnp.concatenate(
        [ref[0, :, (GROUP * kvh + g) * HEAD_DIM:(GROUP * kvh + g + 1) * HEAD_DIM] for g in range(GROUP)], axis=0)


def _masked_softmax(s, valid):
    s = jnp.where(valid, s, NEG_INF)
    e = jnp.where(valid, jnp.exp(s - jnp.max(s, axis=-1, keepdims=True)), 0.0)
    l = jnp.sum(e, axis=-1, keepdims=True)
    return e / jnp.where(l > 0.0, l, 1.0)


def _topk_mask(score, k):
    idx = lax.broadcasted_iota(jnp.int32, score.shape, 1).astype(F32)
    sel = jnp.zeros(score.shape, F32)
    s = score
    for _ in range(k):
        m = jnp.max(s, axis=-1, keepdims=True)
        first = jnp.min(jnp.where(s == m, idx, float(score.shape[-1])), axis=-1, keepdims=True)
        pick = idx == first
        sel = jnp.where(pick & (m > 0.5 * NEG_INF), 1.0, sel)
        s = jnp.where(pick, -3e38, s)
    return sel


def _masked_flash(q, k_ref, v_ref, lo, n_tiles, tile_mask):
    r = q.shape[0]

    def body(j, carry):
        m_i, l_i, acc = carry
        k0 = pl.multiple_of(j * KEY_TILE, KEY_TILE)
        kt = k_ref[0, pl.ds(k0, KEY_TILE), lo:lo + HEAD_DIM].astype(BF16)
        vt = v_ref[0, pl.ds(k0, KEY_TILE), lo:lo + HEAD_DIM].astype(BF16)
        valid = tile_mask(j, k0)
        s = jnp.where(valid, _dot_nt(q, kt), NEG_INF)
        m_new = jnp.maximum(m_i, jnp.max(s, axis=-1, keepdims=True))
        alpha = jnp.exp(m_i - m_new)
        p = jnp.where(valid, jnp.exp(s - m_new), 0.0)
        l_new = alpha * l_i + jnp.sum(p, axis=-1, keepdims=True)
        acc = alpha * acc + _dot(p.astype(BF16), vt)
        return m_new, l_new, acc

    init = (jnp.full((r, 1), NEG_INF, F32), jnp.zeros((r, 1), F32), jnp.zeros((r, HEAD_DIM), F32))
    _, l_i, acc = lax.fori_loop(0, n_tiles, body, init)
    return acc / jnp.where(l_i > 0.0, l_i, 1.0)


def _nsa_kernel(qa_ref, qar_ref, gate_ref, kc_ref, vc_ref, sk_ref, sv_ref, win_ref, map_ref, o_ref,
                *, tq, q_pos0, span):
    i = pl.program_id(1)
    t0 = q_pos0 + i * tq
    r = GROUP * tq
    nc_pad = kc_ref.shape[1]
    ns_pad = map_ref.shape[1]
    row = lax.broadcasted_iota(jnp.int32, (r, 1), 0)
    t_row = t0 + (row & (tq - 1))
    t_tok = t0 + lax.broadcasted_iota(jnp.int32, (tq, 1), 0)
    c_end = lax.broadcasted_iota(jnp.int32, (1, nc_pad), 1) * CMP_STRIDE + CMP_LEN
    blk = lax.broadcasted_iota(jnp.int32, (1, ns_pad), 1)
    lane_tile = lax.broadcasted_iota(jnp.int32, (1, KEY_TILE), 1)
    blk_col = lax.broadcasted_iota(jnp.int32, (ns_pad, 1), 0)
    n_tiles = (t0 + tq + KEY_TILE - 1) // KEY_TILE
    w0 = pl.multiple_of(i * tq, 8)
    wpos = t0 - WINDOW + lax.broadcasted_iota(jnp.int32, (1, span), 1)
    w_valid = (wpos >= 0) & (wpos <= t_row) & (wpos >= t_row - WINDOW)
    gates = gate_ref[0]

    for kvh in range(KV_HEADS):
        lo = kvh * HEAD_DIM
        q = (_stack_heads(qa_ref, kvh) * SCALE).astype(BF16)
        qr = (_stack_heads(qar_ref, kvh) * SCALE).astype(BF16)

        kc = kc_ref[0, :, lo:lo + HEAD_DIM].astype(BF16)
        vc = vc_ref[0, :, lo:lo + HEAD_DIM].astype(BF16)
        p_cmp = _masked_softmax(_dot_nt(q, kc), c_end <= t_row + 1)
        o_cmp = _dot(p_cmp.astype(BF16), vc)

        p_sum = p_cmp[0:tq]
        for g in range(1, GROUP):
            p_sum = p_sum + p_cmp[g * tq:(g + 1) * tq]
        imp = jnp.dot(p_sum, map_ref[...], preferred_element_type=F32, precision=lax.Precision.HIGHEST)
        jt = t_tok >> SEL_SHIFT
        score = jnp.where((blk == jt) | (blk == 0), FORCE_SCORE, jnp.where(blk < jt, imp, NEG_INF))
        sel = _topk_mask(score, SEL_TOPK).astype(BF16)

        def sel_mask(j, k0, sel=sel):
            expand = (blk_col == ((k0 + lane_tile) >> SEL_SHIFT)).astype(BF16)
            chosen = _dot(sel, expand)
            chosen = jnp.concatenate([chosen] * GROUP, axis=0)
            return (chosen > 0.5) & (k0 + lane_tile <= t_row)

        o_sel = _masked_flash(qr, sk_ref, sv_ref, lo, n_tiles, sel_mask)

        wk = win_ref[0, pl.ds(w0, span), lo:lo + HEAD_DIM].astype(BF16)
        wv = win_ref[0, pl.ds(w0, span), KV_WIDTH + lo:KV_WIDTH + lo + HEAD_DIM].astype(BF16)
        p_win = _masked_softmax(_dot_nt(qr, wk), w_valid)
        o_win = _dot(p_win.astype(BF16), wv)

        for g in range(GROUP):
            h = GROUP * kvh + g
            rows = slice(g * tq, (g + 1) * tq)
            o = (gates[:, 3 * h:3 * h + 1] * o_cmp[rows] + gates[:, 3 * h + 1:3 * h + 2] * o_sel[rows]
                 + gates[:, 3 * h + 2:3 * h + 3] * o_win[rows])
            o_ref[0, :, h * HEAD_DIM:(h + 1) * HEAD_DIM] = o


def _selection_map(n_cmp, nc_pad, ns_pad):
    c0 = np.arange(nc_pad)[:, None] * CMP_STRIDE
    j0 = np.arange(ns_pad)[None, :] * SEL_BLOCK
    ov = np.clip(np.minimum(c0 + CMP_LEN, j0 + SEL_BLOCK) - np.maximum(c0, j0), 0, None) / CMP_STRIDE
    ov = np.where(np.arange(nc_pad)[:, None] < n_cmp, ov, 0.0)
    return jnp.asarray(ov, dtype=F32)


def _nsa(qa, qar, gate, kc, vc, full, win, q_pos0, tq):
    b, t, _ = qa.shape
    l = full.shape[1]
    nc_pad = kc.shape[1]
    ns_pad = _round_up(l // SEL_BLOCK, LANES)
    span = _round_up(WINDOW + tq, LANES)
    sel_map = _selection_map(l // CMP_STRIDE - 1, nc_pad, ns_pad)
    qspec = pl.BlockSpec((1, tq, NSA_WIDTH), lambda bi, i: (bi, i, 0))
    seq = lambda a: pl.BlockSpec((1,) + a.shape[1:], lambda bi, i: (bi, 0, 0))
    return pl.pallas_call(
        functools.partial(_nsa_kernel, tq=tq, q_pos0=q_pos0, span=span),
        grid=(b, t // tq),
        in_specs=[
            qspec, qspec,
            pl.BlockSpec((1, tq, LANES), lambda bi, i: (bi, i, 0)),
            seq(kc), seq(vc),
            pl.BlockSpec((1, l, KV_WIDTH), lambda bi, i: (bi, 0, 2)),
            pl.BlockSpec((1, l, KV_WIDTH), lambda bi, i: (bi, 0, 3)),
            seq(win),
            pl.BlockSpec(sel_map.shape, lambda bi, i: (0, 0)),
        ],
        out_specs=qspec,
        out_shape=jax.ShapeDtypeStruct((b, t, NSA_WIDTH), F32),
        compiler_params=_params(("parallel", "arbitrary")),
        name="nsa",
    )(qa, qar, gate, kc, vc, full, full, win, sel_map)


def _moba_kernel(qb_ref, km_ref, mk_ref, mv_ref, o_ref, *, tq, q_pos0):
    i = pl.program_id(1)
    t0 = q_pos0 + i * tq
    r = GROUP * tq
    nb_pad = km_ref.shape[1]
    row = lax.broadcasted_iota(jnp.int32, (r, 1), 0)
    t_row = t0 + (row & (tq - 1))
    bt = t_row >> MOBA_SHIFT
    nblk = lax.broadcasted_iota(jnp.int32, (1, nb_pad), 1)
    nblk_col = lax.broadcasted_iota(jnp.int32, (nb_pad, 1), 0)
    lane_tile = lax.broadcasted_iota(jnp.int32, (1, KEY_TILE), 1)
    n_tiles = (t0 + tq + KEY_TILE - 1) // KEY_TILE

    for kvh in range(KV_HEADS):
        lo = kvh * HEAD_DIM
        qf = _stack_heads(qb_ref, kvh)
        km = km_ref[0, :, lo:lo + HEAD_DIM]
        gate_s = lax.dot_general(qf, km, NT_DIMS, preferred_element_type=F32, precision=lax.Precision.HIGHEST)
        gate_s = jnp.where(nblk < bt, gate_s, NEG_INF)
        chosen = jnp.where(nblk == bt, 1.0, _topk_mask(gate_s, MOBA_TOPK)).astype(BF16)

        def blk_mask(j, k0, chosen=chosen):
            onehot = jnp.broadcast_to(nblk_col == j, (nb_pad, KEY_TILE)).astype(BF16)
            return (_dot(chosen, onehot) > 0.5) & (k0 + lane_tile <= t_row)

        o = _masked_flash((qf * SCALE).astype(BF16), mk_ref, mv_ref, lo, n_tiles, blk_mask)
        for g in range(GROUP):
            h = GROUP * kvh + g
            o_ref[0, :, h * HEAD_DIM:(h + 1) * HEAD_DIM] = o[g * tq:(g + 1) * tq]


def _moba(qb, kmean, full, q_pos0, tq):
    b, t, _ = qb.shape
    l = full.shape[1]
    qspec = pl.BlockSpec((1, tq, MOBA_WIDTH), lambda bi, i: (bi, i, 0))
    return pl.pallas_call(
        functools.partial(_moba_kernel, tq=tq, q_pos0=q_pos0),
        grid=(b, t // tq),
        in_specs=[
            qspec,
            pl.BlockSpec((1,) + kmean.shape[1:], lambda bi, i: (bi, 0, 0)),
            pl.BlockSpec((1, l, KV_WIDTH), lambda bi, i: (bi, 0, 4)),
            pl.BlockSpec((1, l, KV_WIDTH), lambda bi, i: (bi, 0, 5)),
        ],
        out_specs=qspec,
        out_shape=jax.ShapeDtypeStruct((b, t, MOBA_WIDTH), F32),
        compiler_params=_params(("parallel", "arbitrary")),
        name="moba",
    )(qb, kmean, full, full)


def _out_kernel(x_ref, oa_ref, ob_ref, za_ref, zb_ref, wout_ref, gple_ref, wg_ref, ple_ref, wp_ref, gfin_ref,
                x_o, y_o):
    mixed = jnp.concatenate([oa_ref[...] * za_ref[...], ob_ref[...] * zb_ref[...]], axis=-1).astype(BF16)
    x1 = x_ref[...] + _dot(mixed, wout_ref[...])
    gate = _sigmoid(_dot(_rms(x1, gple_ref[...]).astype(BF16), wg_ref[...]))
    x2 = x1 + gate * _dot(ple_ref[...].astype(BF16), wp_ref[...])
    x_o[...] = x2
    y_o[...] = _rms(x2, gfin_ref[...])


def _out(x2d, oa, ob, za, zb, w_out, g_ple, w_gate, ple, w_proj, g_final):
    n, d_model = x2d.shape
    tm = min(256, n)
    row = lambda a: pl.BlockSpec((tm, a.shape[1]), lambda i: (i, 0))
    full = lambda a: pl.BlockSpec(a.shape, lambda i: (0, 0))
    args = (x2d, oa, ob, za, zb, w_out, g_ple, w_gate, ple, w_proj, g_final)
    specs = [row(x2d), row(oa), row(ob), row(za), row(zb), full(w_out), full(g_ple), full(w_gate), row(ple),
             full(w_proj), full(g_final)]
    return pl.pallas_call(
        _out_kernel,
        grid=(n // tm,),
        in_specs=specs,
        out_specs=[row(x2d), row(x2d)],
        out_shape=[jax.ShapeDtypeStruct((n, d_model), F32)] * 2,
        compiler_params=_params(("parallel",)),
        name="out",
    )(*args)


def _mixer_layer(x, ple, pos, past, win_prev, weights, g_final):
    (g_mix, w_split, w_out, cmp_k, cmp_v, g_ple, w_gate, w_proj) = weights
    b, t, d_model = x.shape
    n = b * t
    q_pos0 = 0 if past is None else past[1].shape[1] * past[0].shape[1]
    qa, qar, paged, win_new, gate, za, qb, zb = _proj(x.reshape(n, d_model), pos, g_mix, w_split)
    paged3 = paged.reshape(b, t, PAGED_WIDTH)
    if past is None:
        full = paged3
    else:
        pool, page_table, page_base = past
        full = _gather(pool, page_table, page_base, paged3, _round_up(q_pos0 + t, MOBA_BLOCK))
    l = full.shape[1]
    tq = min(128, t)
    span = _round_up(WINDOW + tq, LANES)
    win_new3 = win_new.reshape(b, t, 2 * KV_WIDTH)
    win_rows = max(WINDOW + t, t - tq + span)
    win = jnp.concatenate([win_prev, win_new3,
                           jnp.zeros((b, win_rows - WINDOW - t, 2 * KV_WIDTH), F32)], axis=1)
    nc_pad = _round_up(l // CMP_STRIDE, LANES)
    kc = _compress(full, 0, *cmp_k, nc_pad)
    vc = _compress(full, 1, *cmp_v, nc_pad)
    kmean = _kmean(full, LANES)
    o_a = _nsa(qa.reshape(b, t, -1), qar.reshape(b, t, -1), gate.reshape(b, t, -1), kc, vc, full, win, q_pos0, tq)
    o_b = _moba(qb.reshape(b, t, -1), kmean, full, q_pos0, tq)
    x_next, y = _out(x.reshape(n, d_model), o_a.reshape(n, -1), o_b.reshape(n, -1), za, zb, w_out, g_ple, w_gate,
                     ple.reshape(n, -1), w_proj, g_final)
    return x_next.reshape(b, t, d_model), y.reshape(b, t, d_model), paged3, win_new3


def _split_w_in(w_in):
    sizes = (NSA_WIDTH, 6 * KV_WIDTH, 3 * NSA_HEADS, NSA_WIDTH, MOBA_WIDTH, 2 * KV_WIDTH, MOBA_WIDTH)
    offs = np.concatenate([[0], np.cumsum(sizes)])
    wqa, wkva, wgate, wza, wqb, wkvb, wzb = [w_in[:, offs[k]:offs[k + 1]].astype(BF16) for k in range(len(sizes))]
    wgate = jnp.pad(wgate, ((0, 0), (0, LANES - wgate.shape[1])))
    return wqa, wkva, wgate, wza, wqb, wkvb, wzb


def kernel(x_prompt, x_sample, cache_paged_kv, cache_win_kv, page_table, p_prompt, p_sample, g_mix, w_in, w_out,
           cmp_pos_k, cmp_w1_k, cmp_w2_k, cmp_pos_v, cmp_w1_v, cmp_w2_v, g_ple, w_ple_gate, w_ple_proj, g_final):
    depth = w_in.shape[0]
    bp, tp, _ = x_prompt.shape
    bs, ts, _ = x_sample.shape
    n_pool, page = cache_paged_kv.shape[1], cache_paged_kv.shape[2]
    past_len = page_table.shape[1] * page
    wbuf = cache_win_kv.shape[2]
    pos_p = jnp.arange(tp, dtype=jnp.int32)
    pos_s = past_len + jnp.arange(ts, dtype=jnp.int32)
    pool = cache_paged_kv.reshape(depth * n_pool, page, PAGED_WIDTH)
    g_fin = g_final.reshape(1, -1)
    xp, xs = x_prompt, x_sample
    yp = ys = None
    new_pp, new_pw, new_sp, new_sw = [], [], [], []
    for i in range(depth):
        weights = (g_mix[i].reshape(1, -1), _split_w_in(w_in[i]), w_out[i].astype(BF16),
                   _compress_weights(cmp_pos_k[i], cmp_w1_k[i], cmp_w2_k[i]),
                   _compress_weights(cmp_pos_v[i], cmp_w1_v[i], cmp_w2_v[i]),
                   g_ple[i].reshape(1, -1), w_ple_gate[i].astype(BF16), w_ple_proj[i].astype(BF16))
        win0 = jnp.zeros((bp, WINDOW, 2 * KV_WIDTH), F32)
        xp, yp, pk, pw = _mixer_layer(xp, p_prompt[i], pos_p, None, win0, weights, g_fin)
        win_prev = jnp.concatenate([jnp.zeros((bs, WINDOW - wbuf, 2 * KV_WIDTH), F32),
                                    cache_win_kv[i].reshape(bs, wbuf, 2 * KV_WIDTH)], axis=1)
        xs, ys, sk, sw = _mixer_layer(xs, p_sample[i], pos_s, (pool, page_table, i * n_pool), win_prev, weights,
                                      g_fin)
        new_pp.append(pk.reshape(bp, tp, N_PAGED_SLOTS, KV_HEADS, HEAD_DIM))
        new_pw.append(pw[:, tp - min(WINDOW, tp):].reshape(bp, min(WINDOW, tp), 2, KV_HEADS, HEAD_DIM))
        new_sp.append(sk.reshape(bs, ts, N_PAGED_SLOTS, KV_HEADS, HEAD_DIM))
        new_sw.append(sw.reshape(bs, ts, 2, KV_HEADS, HEAD_DIM))
    return (yp, ys, jnp.stack(new_pp), jnp.stack(new_pw), jnp.stack(new_sp), jnp.stack(new_sw))
```

```python
import functools

import numpy as np
import jax
import jax.numpy as jnp
from jax import lax
from jax.experimental import pallas as pl
from jax.experimental.pallas import tpu as pltpu

HEAD_DIM = 64
NSA_HEADS = 8
MOBA_HEADS = 8
KV_HEADS = 2
GROUP = NSA_HEADS // KV_HEADS
ROT_DIM = HEAD_DIM // 4
ROPE_THETA = 500000.0
CMP_LEN = 32
CMP_STRIDE = 16
SEL_BLOCK = 64
SEL_TOPK = 8
WINDOW = 512
MOBA_BLOCK = 256
MOBA_TOPK = 3
N_PAGED_SLOTS = 6
RMS_EPS = 1e-6
NEG_INF = -1e30
FORCE_SCORE = 1e4
SCALE = HEAD_DIM ** -0.5
SEL_SHIFT = SEL_BLOCK.bit_length() - 1
MOBA_SHIFT = MOBA_BLOCK.bit_length() - 1

LANES = 128
KV_WIDTH = KV_HEADS * HEAD_DIM
NSA_WIDTH = NSA_HEADS * HEAD_DIM
MOBA_WIDTH = MOBA_HEADS * HEAD_DIM
PAGED_WIDTH = N_PAGED_SLOTS * KV_WIDTH
KEY_TILE = 256
PAGES_PER_STEP = 8
VMEM_LIMIT = 56 * 1024 * 1024

F32 = jnp.float32
BF16 = jnp.bfloat16
NT_DIMS = (((1,), (1,)), ((), ()))


def _round_up(n, m):
    return -(-n // m) * m


def _params(sem):
    return pltpu.CompilerParams(dimension_semantics=sem, vmem_limit_bytes=VMEM_LIMIT)


def _rms(x, g):
    return x * lax.rsqrt(jnp.mean(x * x, axis=-1, keepdims=True) + RMS_EPS) * g


def _sigmoid(x):
    return 1.0 / (1.0 + jnp.exp(-x))


def _dot(a, b):
    return jnp.dot(a, b, preferred_element_type=F32)


def _dot_nt(a, b):
    return lax.dot_general(a, b, NT_DIMS, preferred_element_type=F32)


def _dot_nt_exact(a, b):
    return lax.dot_general(a, b, NT_DIMS, preferred_element_type=F32, precision=lax.Precision.HIGHEST)


def _resident(a):
    return pl.BlockSpec(a.shape, lambda *_: (0,) * a.ndim, pipeline_mode=pl.Buffered(1))


def _rope128(v, c, sp, sm):
    half = ROT_DIM // 2
    return v * c + pltpu.roll(v, half, axis=1) * sp + pltpu.roll(v, LANES - half, axis=1) * sm


def _rope(v, c, sp, sm):
    n = v.shape[-1] // LANES
    return jnp.concatenate([_rope128(v[:, i * LANES:(i + 1) * LANES], c, sp, sm) for i in range(n)], axis=-1)


def _proj_kernel(x_ref, g_ref, c_ref, sp_ref, sm_ref, wqa, wkva, wgate, wza, wqb, wkvb, wzb,
                 qa_o, qar_o, paged_o, win_o, gate_o, za_o, qb_o, zb_o):
    h = _rms(x_ref[...], g_ref[...]).astype(BF16)
    c, sp, sm = c_ref[...], sp_ref[...], sm_ref[...]
    qa = _dot(h, wqa[...])
    qa_o[...] = qa
    qar_o[...] = _rope(qa, c, sp, sm)
    kva = _dot(h, wkva[...])
    paged_o[:, 0:2 * KV_WIDTH] = kva[:, 0:2 * KV_WIDTH]
    paged_o[:, 2 * KV_WIDTH:3 * KV_WIDTH] = _rope(kva[:, 2 * KV_WIDTH:3 * KV_WIDTH], c, sp, sm)
    paged_o[:, 3 * KV_WIDTH:4 * KV_WIDTH] = kva[:, 3 * KV_WIDTH:4 * KV_WIDTH]
    win_o[:, 0:KV_WIDTH] = _rope(kva[:, 4 * KV_WIDTH:5 * KV_WIDTH], c, sp, sm)
    win_o[:, KV_WIDTH:2 * KV_WIDTH] = kva[:, 5 * KV_WIDTH:6 * KV_WIDTH]
    kvb = _dot(h, wkvb[...])
    paged_o[:, 4 * KV_WIDTH:5 * KV_WIDTH] = _rope(kvb[:, 0:KV_WIDTH], c, sp, sm)
    paged_o[:, 5 * KV_WIDTH:6 * KV_WIDTH] = kvb[:, KV_WIDTH:2 * KV_WIDTH]
    gate_o[...] = _sigmoid(_dot(h, wgate[...]))
    za = _dot(h, wza[...])
    za_o[...] = za * _sigmoid(za)
    qb_o[...] = _rope(_dot(h, wqb[...]), c, sp, sm)
    zb = _dot(h, wzb[...])
    zb_o[...] = zb * _sigmoid(zb)


def _rope_tables(pos, n_rows):
    half = ROT_DIM // 2
    inv_freq = ROPE_THETA ** (-jnp.arange(half, dtype=F32) / half)
    ang = pos.astype(F32)[:, None] * inv_freq[None, :]
    cos, sin = jnp.cos(ang), jnp.sin(ang)
    d = np.arange(LANES) % HEAD_DIM
    idx = d % half
    c = jnp.where(d < ROT_DIM, cos[:, idx], 1.0)
    sp = jnp.where((d >= half) & (d < ROT_DIM), sin[:, idx], 0.0)
    sm = jnp.where(d < half, -sin[:, idx], 0.0)
    rep = n_rows // pos.shape[0]
    return tuple(jnp.tile(t, (rep, 1)) for t in (c, sp, sm))


def _proj(x2d, pos, g, w):
    n, d_model = x2d.shape
    t = pos.shape[0]
    tm = min(256, n)
    tab_rows = max(t, tm)
    tabs = _rope_tables(pos, tab_rows)
    n_tab = tab_rows // tm
    row = lambda width: pl.BlockSpec((tm, width), lambda i: (i, 0))
    tab = pl.BlockSpec((tm, LANES), lambda i: (i % n_tab, 0))
    widths = (NSA_WIDTH, NSA_WIDTH, PAGED_WIDTH, 2 * KV_WIDTH, LANES, NSA_WIDTH, MOBA_WIDTH, MOBA_WIDTH)
    return pl.pallas_call(
        _proj_kernel,
        grid=(n // tm,),
        in_specs=[row(d_model), _resident(g), tab, tab, tab] + [_resident(a) for a in w],
        out_specs=[row(wd) for wd in widths],
        out_shape=[jax.ShapeDtypeStruct((n, wd), F32) for wd in widths],
        compiler_params=_params(("parallel",)),
        name="proj",
    )(x2d, g, *tabs, *w)


def _compress_rows(chunk_row, pos_ref, w1_ref, w2_ref, m):
    hid2 = w2_ref.shape[0]
    xcat = jnp.concatenate([chunk_row(l).astype(BF16) for l in range(CMP_STRIDE)], axis=-1)
    ab = _dot(xcat, w1_ref[...])
    pb = _dot(pos_ref[...], w1_ref[...])
    bias = pb[0:1, 0:hid2] + pb[1:2, hid2:2 * hid2]
    nxt = pltpu.roll(ab[:, hid2:2 * hid2], m - 1, axis=0)
    pre = ab[:, 0:hid2] + nxt + bias
    hid = pre * _sigmoid(pre)
    return _dot(hid.astype(BF16), w2_ref[...])


def _compress_kernel(src_ref, pos_ref, w1_ref, w2_ref, o_ref, *, n_chunks):
    m = n_chunks
    out = _compress_rows(lambda l: src_ref[0, pl.ds(l, m, stride=CMP_STRIDE), :], pos_ref, w1_ref, w2_ref, m)
    o_ref[0, 0:m, :] = out
    if o_ref.shape[1] > m:
        o_ref[0, m:, :] = jnp.zeros((o_ref.shape[1] - m, LANES), F32)


def _compress(full, slot, pos_rows, w1, w2, nc_pad):
    b, l, _ = full.shape
    n_chunks = l // CMP_STRIDE
    return pl.pallas_call(
        functools.partial(_compress_kernel, n_chunks=n_chunks),
        grid=(b,),
        in_specs=[pl.BlockSpec((1, l, KV_WIDTH), lambda i: (i, 0, slot)),
                  _resident(pos_rows), _resident(w1), _resident(w2)],
        out_specs=pl.BlockSpec((1, nc_pad, KV_WIDTH), lambda i: (i, 0, 0)),
        out_shape=jax.ShapeDtypeStruct((b, nc_pad, KV_WIDTH), F32),
        compiler_params=_params(("parallel",)),
        name="compress",
    )(full, pos_rows, w1, w2)


def _compress_weights(pos_emb, w1, w2):
    hidden = w1.shape[-1]
    eye = jnp.eye(KV_HEADS, dtype=F32)

    def half(w):
        return jnp.einsum('ldf,hg->lhdgf', w, eye).reshape(CMP_STRIDE * KV_WIDTH, KV_HEADS * hidden)

    w1bd = jnp.concatenate([half(w1[:CMP_STRIDE]), half(w1[CMP_STRIDE:])], axis=1).astype(BF16)
    w2bd = jnp.einsum('fd,hg->hfgd', w2, eye).reshape(KV_HEADS * hidden, KV_WIDTH).astype(BF16)

    def pos_row(p):
        return jnp.broadcast_to(p[:, None, :], (CMP_STRIDE, KV_HEADS, HEAD_DIM)).reshape(-1)

    pos_rows = jnp.zeros((8, CMP_STRIDE * KV_WIDTH), F32)
    pos_rows = pos_rows.at[0].set(pos_row(pos_emb[:CMP_STRIDE])).at[1].set(pos_row(pos_emb[CMP_STRIDE:]))
    return pos_rows.astype(BF16), w1bd, w2bd


def _kmean_kernel(src_ref, o_ref, *, n_blocks):
    x = src_ref[0].reshape(n_blocks, MOBA_BLOCK, KV_WIDTH)
    o_ref[0, 0:n_blocks, :] = jnp.sum(x, axis=1) * (1.0 / MOBA_BLOCK)
    o_ref[0, n_blocks:, :] = jnp.zeros((o_ref.shape[1] - n_blocks, KV_WIDTH), F32)


def _kmean(full, nb_pad):
    b, l, _ = full.shape
    return pl.pallas_call(
        functools.partial(_kmean_kernel, n_blocks=l // MOBA_BLOCK),
        grid=(b,),
        in_specs=[pl.BlockSpec((1, l, KV_WIDTH), lambda i: (i, 0, 4))],
        out_specs=pl.BlockSpec((1, nb_pad, KV_WIDTH), lambda i: (i, 0, 0)),
        out_shape=jax.ShapeDtypeStruct((b, nb_pad, KV_WIDTH), F32),
        compiler_params=_params(("parallel",)),
        name="kmean",
    )(full)


def _stack_heads(ref, kvh):
    return jnp.concatenate(
        [ref[0, :, (GROUP * kvh + g) * HEAD_DIM:(GROUP * kvh + g + 1) * HEAD_DIM] for g in range(GROUP)], axis=0)


def _masked_softmax(s, valid):
    s = jnp.where(valid, s, NEG_INF)
    e = jnp.where(valid, jnp.exp(s - jnp.max(s, axis=-1, keepdims=True)), 0.0)
    l = jnp.sum(e, axis=-1, keepdims=True)
    return e / jnp.where(l > 0.0, l, 1.0)


def _topk_mask(score, k, axis):
    n = score.shape[axis]
    idx = lax.broadcasted_iota(jnp.int32, score.shape, axis).astype(F32)
    sel = jnp.zeros(score.shape, F32)
    s = score
    for _ in range(k):
        m = jnp.max(s, axis=axis, keepdims=True)
        first = jnp.min(jnp.where(s == m, idx, float(n)), axis=axis, keepdims=True)
        pick = idx == first
        sel = jnp.where(pick & (m > 0.5 * NEG_INF), 1.0, sel)
        s = jnp.where(pick, -3e38, s)
    return sel


def _block_bias(chosen):
    return ((chosen - 1.0) * -NEG_INF).astype(BF16)


def _flash(q, k_ref, v_ref, lo, diag, t_row, bias_rows, expand):
    lane = lax.broadcasted_iota(jnp.int32, (1, KEY_TILE), 1)

    def tile(j):
        k0 = pl.multiple_of(j * KEY_TILE, KEY_TILE)
        kt = k_ref[0, pl.ds(k0, KEY_TILE), lo:lo + HEAD_DIM].astype(BF16)
        vt = v_ref[0, pl.ds(k0, KEY_TILE), lo:lo + HEAD_DIM].astype(BF16)
        return _dot_nt(q, kt) + _dot(bias_rows, expand(j)), vt, k0

    s, vt, k0 = tile(diag)
    s = jnp.where(k0 + lane <= t_row, s, NEG_INF)
    m0 = jnp.max(s, axis=-1, keepdims=True)
    p = jnp.exp(s - m0)
    init = (m0, jnp.sum(p, axis=-1, keepdims=True), _dot(p.astype(BF16), vt))

    def body(j, carry):
        m_i, l_i, acc = carry
        s, vt, _ = tile(j)
        m_new = jnp.maximum(m_i, jnp.max(s, axis=-1, keepdims=True))
        alpha = jnp.exp(m_i - m_new)
        p = jnp.exp(s - m_new)
        return m_new, alpha * l_i + jnp.sum(p, axis=-1, keepdims=True), alpha * acc + _dot(p.astype(BF16), vt)

    _, l_i, acc = lax.fori_loop(0, diag, body, init)
    return acc / l_i


def _selection_map(n_cmp, nc_pad, ns_pad):
    c0 = np.arange(nc_pad)[:, None] * CMP_STRIDE
    j0 = np.arange(ns_pad)[None, :] * SEL_BLOCK
    ov = np.clip(np.minimum(c0 + CMP_LEN, j0 + SEL_BLOCK) - np.maximum(c0, j0), 0, None) / CMP_STRIDE
    return np.where(np.arange(nc_pad)[:, None] < n_cmp, ov, 0.0).astype(np.float32)


def _window_branch(qr, win_ref, w0, span, lo, w_valid):
    wk = win_ref[0, pl.ds(w0, span), lo:lo + HEAD_DIM].astype(BF16)
    wv = win_ref[0, pl.ds(w0, span), KV_WIDTH + lo:KV_WIDTH + lo + HEAD_DIM].astype(BF16)
    s = jnp.where(w_valid, _dot_nt(qr, wk), NEG_INF)
    e = jnp.exp(s - jnp.max(s, axis=-1, keepdims=True))
    return _dot(e.astype(BF16), wv) / jnp.sum(e, axis=-1, keepdims=True)


def _write_heads(o_ref, kvh, tq, head_out):
    for g in range(GROUP):
        h = GROUP * kvh + g
        o_ref[0, :, h * HEAD_DIM:(h + 1) * HEAD_DIM] = head_out(g, h, slice(g * tq, (g + 1) * tq))


def _nsa_kernel(qa_ref, qar_ref, gate_ref, kc_ref, vc_ref, sk_ref, sv_ref, win_ref, mapt_ref, o_ref, *, tq, span):
    i = pl.program_id(1)
    t0 = i * tq
    r = GROUP * tq
    nc_pad = kc_ref.shape[1]
    ns_pad = mapt_ref.shape[0]
    row = lax.broadcasted_iota(jnp.int32, (r, 1), 0)
    t_row = t0 + (row & (tq - 1))
    c_end = lax.broadcasted_iota(jnp.int32, (1, nc_pad), 1) * CMP_STRIDE + CMP_LEN
    blk_t = lax.broadcasted_iota(jnp.int32, (ns_pad, 1), 0)
    jt_t = (t0 + lax.broadcasted_iota(jnp.int32, (1, tq), 1)) >> SEL_SHIFT
    lane_blk = lax.broadcasted_iota(jnp.int32, (1, KEY_TILE), 1) >> SEL_SHIFT
    diag = t0 // KEY_TILE
    w0 = pl.multiple_of(i * tq, 8)
    wpos = t0 - WINDOW + lax.broadcasted_iota(jnp.int32, (1, span), 1)
    w_valid = (wpos >= 0) & (wpos <= t_row) & (wpos >= t_row - WINDOW)
    gates = gate_ref[0]

    def expand(j):
        return (blk_t - j * (KEY_TILE // SEL_BLOCK) == lane_blk).astype(BF16)

    for kvh in range(KV_HEADS):
        lo = kvh * HEAD_DIM
        q = (_stack_heads(qa_ref, kvh) * SCALE).astype(BF16)
        qr = (_stack_heads(qar_ref, kvh) * SCALE).astype(BF16)

        kc = kc_ref[0, :, lo:lo + HEAD_DIM].astype(BF16)
        vc = vc_ref[0, :, lo:lo + HEAD_DIM].astype(BF16)
        p_cmp = _masked_softmax(_dot_nt(q, kc), c_end <= t_row + 1)
        o_cmp = _dot(p_cmp.astype(BF16), vc)

        p_sum = p_cmp[0:tq]
        for g in range(1, GROUP):
            p_sum = p_sum + p_cmp[g * tq:(g + 1) * tq]
        imp_t = _dot_nt_exact(mapt_ref[...], p_sum)
        score_t = jnp.where((blk_t == jt_t) | (blk_t == 0), FORCE_SCORE, jnp.where(blk_t < jt_t, imp_t, NEG_INF))
        sel = _topk_mask(score_t, SEL_TOPK, 0).T
        bias_rows = _block_bias(jnp.concatenate([sel] * GROUP, axis=0))

        o_sel = _flash(qr, sk_ref, sv_ref, lo, diag, t_row, bias_rows, expand)
        o_win = _window_branch(qr, win_ref, w0, span, lo, w_valid)

        _write_heads(o_ref, kvh, tq, lambda g, h, rows: (
            gates[:, 3 * h:3 * h + 1] * o_cmp[rows] + gates[:, 3 * h + 1:3 * h + 2] * o_sel[rows]
            + gates[:, 3 * h + 2:3 * h + 3] * o_win[rows]))


def _nsa(qa, qar, gate, kc, vc, full, win, tq):
    b, t, _ = qa.shape
    l = full.shape[1]
    nc_pad = kc.shape[1]
    ns_pad = _round_up(l // SEL_BLOCK, LANES)
    span = _round_up(WINDOW + tq, LANES)
    map_t = jnp.asarray(_selection_map(l // CMP_STRIDE - 1, nc_pad, ns_pad).T)
    qspec = pl.BlockSpec((1, tq, NSA_WIDTH), lambda bi, i: (bi, i, 0))
    seq = lambda a: pl.BlockSpec((1,) + a.shape[1:], lambda bi, i: (bi, 0, 0))
    return pl.pallas_call(
        functools.partial(_nsa_kernel, tq=tq, span=span),
        grid=(b, t // tq),
        in_specs=[
            qspec, qspec,
            pl.BlockSpec((1, tq, LANES), lambda bi, i: (bi, i, 0)),
            seq(kc), seq(vc),
            pl.BlockSpec((1, l, KV_WIDTH), lambda bi, i: (bi, 0, 2)),
            pl.BlockSpec((1, l, KV_WIDTH), lambda bi, i: (bi, 0, 3)),
            seq(win),
            _resident(map_t),
        ],
        out_specs=qspec,
        out_shape=jax.ShapeDtypeStruct((b, t, NSA_WIDTH), F32),
        compiler_params=_params(("parallel", "arbitrary")),
        name="nsa",
    )(qa, qar, gate, kc, vc, full, full, win, map_t)


def _moba_kernel(qb_ref, km_ref, mk_ref, mv_ref, o_ref, *, tq):
    i = pl.program_id(1)
    t0 = i * tq
    r = GROUP * tq
    nb_pad = km_ref.shape[1]
    row = lax.broadcasted_iota(jnp.int32, (r, 1), 0)
    t_row = t0 + (row & (tq - 1))
    bt_t = (t0 + (lax.broadcasted_iota(jnp.int32, (1, r), 1) & (tq - 1))) >> MOBA_SHIFT
    nblk_t = lax.broadcasted_iota(jnp.int32, (nb_pad, 1), 0)
    diag = t0 // KEY_TILE

    def expand(j):
        return jnp.broadcast_to(nblk_t == j, (nb_pad, KEY_TILE)).astype(BF16)

    for kvh in range(KV_HEADS):
        lo = kvh * HEAD_DIM
        qf = _stack_heads(qb_ref, kvh)
        gate_t = _dot_nt_exact(km_ref[0, :, lo:lo + HEAD_DIM], qf)
        gate_t = jnp.where(nblk_t < bt_t, gate_t, NEG_INF)
        chosen = jnp.where(nblk_t == bt_t, 1.0, _topk_mask(gate_t, MOBA_TOPK, 0)).T
        o = _flash((qf * SCALE).astype(BF16), mk_ref, mv_ref, lo, diag, t_row, _block_bias(chosen), expand)
        _write_heads(o_ref, kvh, tq, lambda g, h, rows: o[rows])


def _moba(qb, kmean, full, tq):
    b, t, _ = qb.shape
    l = full.shape[1]
    qspec = pl.BlockSpec((1, tq, MOBA_WIDTH), lambda bi, i: (bi, i, 0))
    return pl.pallas_call(
        functools.partial(_moba_kernel, tq=tq),
        grid=(b, t // tq),
        in_specs=[
            qspec,
            pl.BlockSpec((1,) + kmean.shape[1:], lambda bi, i: (bi, 0, 0)),
            pl.BlockSpec((1, l, KV_WIDTH), lambda bi, i: (bi, 0, 4)),
            pl.BlockSpec((1, l, KV_WIDTH), lambda bi, i: (bi, 0, 5)),
        ],
        out_specs=qspec,
        out_shape=jax.ShapeDtypeStruct((b, t, MOBA_WIDTH), F32),
        compiler_params=_params(("parallel", "arbitrary")),
        name="moba",
    )(qb, kmean, full, full)


def _softmax_pv_t(s, vt):
    e = jnp.exp(s - jnp.max(s, axis=-1, keepdims=True))
    return _dot_nt(e.astype(BF16), vt) / jnp.sum(e, axis=-1, keepdims=True)


def _sample_kernel(pt_ref, *refs, past_len, ts, span):
    pages = refs[:PAGES_PER_STEP]
    (new_ref, qa_ref, qar_ref, gate_ref, qb_ref, win_ref, map_ref, esel_ref, emoba_ref,
     posk_ref, w1k_ref, w2k_ref, posv_ref, w1v_ref, w2v_ref, oa_ref, ob_ref, tok_scr, kv_scr) = refs[PAGES_PER_STEP:]
    del pt_ref
    step = pl.program_id(1)
    n_steps = pl.num_programs(1)
    page = pages[0].shape[2]
    cmp_rows = 2 * KV_WIDTH

    for j in range(PAGES_PER_STEP):
        tok0 = pl.multiple_of((step * PAGES_PER_STEP + j) * page, page)
        for slot in range(2):
            tok_scr[slot, pl.ds(tok0, page), :] = pages[j][0, slot * KV_WIDTH:(slot + 1) * KV_WIDTH, :].T
    for k in range(past_len // (PAGES_PER_STEP * page)):
        @pl.when(step == k)
        def _(k=k):
            for j in range(PAGES_PER_STEP):
                c0 = (k * PAGES_PER_STEP + j) * page
                kv_scr[:, c0:c0 + page] = pages[j][0, cmp_rows:, :].astype(BF16)

    @pl.when(step == n_steps - 1)
    def _():
        lc = kv_scr.shape[1]
        r = GROUP * ts
        new_pad = jnp.concatenate([new_ref[0], jnp.zeros((page - ts, PAGED_WIDTH), F32)], axis=0)
        kv_scr[:, past_len:past_len + page] = new_pad.T[cmp_rows:, :].astype(BF16)

        m = past_len // CMP_STRIDE
        kc_all = _compress_rows(lambda l: tok_scr[0, pl.ds(l, m, stride=CMP_STRIDE), :], posk_ref, w1k_ref, w2k_ref, m)
        vc_all = _compress_rows(lambda l: tok_scr[1, pl.ds(l, m, stride=CMP_STRIDE), :], posv_ref, w1v_ref, w2v_ref, m)

        ns_pad = map_ref.shape[1]
        nb_pad = emoba_ref.shape[0]
        row = lax.broadcasted_iota(jnp.int32, (r, 1), 0)
        t_row = past_len + (row & (ts - 1))
        t_tok = past_len + lax.broadcasted_iota(jnp.int32, (ts, 1), 0)
        causal = lax.broadcasted_iota(jnp.int32, (1, lc), 1) <= t_row
        c_end = lax.broadcasted_iota(jnp.int32, (1, m), 1) * CMP_STRIDE + CMP_LEN
        blk = lax.broadcasted_iota(jnp.int32, (1, ns_pad), 1)
        nblk = lax.broadcasted_iota(jnp.int32, (1, nb_pad), 1)
        jt = t_tok >> SEL_SHIFT
        bt = t_row >> MOBA_SHIFT
        wpos = past_len - WINDOW + lax.broadcasted_iota(jnp.int32, (1, span), 1)
        w_valid = (wpos >= 0) & (wpos <= t_row) & (wpos >= t_row - WINDOW)
        gates = gate_ref[0]

        for kvh in range(KV_HEADS):
            lo = kvh * HEAD_DIM
            q = (_stack_heads(qa_ref, kvh) * SCALE).astype(BF16)
            qr = (_stack_heads(qar_ref, kvh) * SCALE).astype(BF16)

            p_cmp = _masked_softmax(_dot_nt(q, kc_all[:, lo:lo + HEAD_DIM].astype(BF16)), c_end <= t_row + 1)
            o_cmp = _dot(p_cmp.astype(BF16), vc_all[:, lo:lo + HEAD_DIM].astype(BF16))
            p_sum = p_cmp[0:ts]
            for g in range(1, GROUP):
                p_sum = p_sum + p_cmp[g * ts:(g + 1) * ts]
            imp = jnp.dot(p_sum, map_ref[...], preferred_element_type=F32, precision=lax.Precision.HIGHEST)
            score = jnp.where((blk == jt) | (blk == 0), FORCE_SCORE, jnp.where(blk < jt, imp, NEG_INF))
            sel = _topk_mask(score, SEL_TOPK, 1)
            bias_rows = _block_bias(jnp.concatenate([sel] * GROUP, axis=0))

            s = _dot(qr, kv_scr[lo:lo + HEAD_DIM, :]) + _dot(bias_rows, esel_ref[...])
            o_sel = _softmax_pv_t(jnp.where(causal, s, NEG_INF), kv_scr[KV_WIDTH + lo:KV_WIDTH + lo + HEAD_DIM, :])
            o_win = _window_branch(qr, win_ref, 0, span, lo, w_valid)
            _write_heads(oa_ref, kvh, ts, lambda g, h, rows: (
                gates[:, 3 * h:3 * h + 1] * o_cmp[rows] + gates[:, 3 * h + 1:3 * h + 2] * o_sel[rows]
                + gates[:, 3 * h + 2:3 * h + 3] * o_win[rows]))

            qf = _stack_heads(qb_ref, kvh).astype(BF16)
            s_raw = _dot(qf, kv_scr[2 * KV_WIDTH + lo:2 * KV_WIDTH + lo + HEAD_DIM, :])
            s_hi = s_raw.astype(BF16)
            s_lo = (s_raw - s_hi.astype(F32)).astype(BF16)
            gate_s = (_dot_nt(s_hi, emoba_ref[...]) + _dot_nt(s_lo, emoba_ref[...])) * (1.0 / MOBA_BLOCK)
            gate_s = jnp.where(nblk < bt, gate_s, NEG_INF)
            chosen = jnp.where(nblk == bt, 1.0, _topk_mask(gate_s, MOBA_TOPK, 1))
            s = s_raw * SCALE + _dot(_block_bias(chosen), emoba_ref[...])
            o = _softmax_pv_t(jnp.where(causal, s, NEG_INF),
                              kv_scr[3 * KV_WIDTH + lo:3 * KV_WIDTH + lo + HEAD_DIM, :])
            _write_heads(ob_ref, kvh, ts, lambda g, h, rows: o[rows])


def _block_expansion(n_rows, block, n_keys):
    return jnp.asarray(np.arange(n_rows)[:, None] == (np.arange(n_keys)[None, :] // block), dtype=BF16)


def _sample_attn(pool_t, page_table, page_base, new_rows, qa, qar, gate, qb, win, cmp_k, cmp_v):
    bs, n_pages = page_table.shape
    page = pool_t.shape[2]
    ts = new_rows.shape[1]
    past_len = n_pages * page
    n_steps = n_pages // PAGES_PER_STEP
    assert page == LANES and n_pages % PAGES_PER_STEP == 0 and ts < CMP_STRIDE and ts & (ts - 1) == 0
    assert past_len % MOBA_BLOCK == 0 and (past_len // CMP_STRIDE) % LANES == 0
    lc = past_len + page
    ns_pad = _round_up(lc // SEL_BLOCK, LANES)
    nb_pad = _round_up(lc // MOBA_BLOCK + 1, LANES)
    span = win.shape[1]
    sel_map = jnp.asarray(_selection_map(past_len // CMP_STRIDE - 1, past_len // CMP_STRIDE, ns_pad))
    e_sel = _block_expansion(ns_pad, SEL_BLOCK, lc)
    e_moba = _block_expansion(nb_pad, MOBA_BLOCK, lc)

    def page_spec(j):
        return pl.BlockSpec((1, PAGED_WIDTH, page),
                            lambda b, s, pt: (page_base + pt[b, s * PAGES_PER_STEP + j], 0, 0))

    per_seq = lambda a: pl.BlockSpec((1,) + a.shape[1:], lambda b, s, pt: (b, 0, 0))
    consts = (sel_map, e_sel, e_moba) + tuple(cmp_k) + tuple(cmp_v)
    out_spec = pl.BlockSpec((1, ts, NSA_WIDTH), lambda b, s, pt: (b, 0, 0))
    grid_spec = pltpu.PrefetchScalarGridSpec(
        num_scalar_prefetch=1,
        grid=(bs, n_steps),
        in_specs=[page_spec(j) for j in range(PAGES_PER_STEP)]
        + [per_seq(a) for a in (new_rows, qa, qar, gate, qb, win)] + [_resident(a) for a in consts],
        out_specs=[out_spec, out_spec],
        scratch_shapes=[pltpu.VMEM((2, past_len, KV_WIDTH), F32),
                        pltpu.VMEM((PAGED_WIDTH - 2 * KV_WIDTH, lc), BF16)],
    )
    return pl.pallas_call(
        functools.partial(_sample_kernel, past_len=past_len, ts=ts, span=span),
        grid_spec=grid_spec,
        out_shape=[jax.ShapeDtypeStruct((bs, ts, NSA_WIDTH), F32)] * 2,
        compiler_params=_params(("parallel", "arbitrary")),
        name="sample_attn",
    )(page_table, *([pool_t] * PAGES_PER_STEP), new_rows, qa, qar, gate, qb, win, *consts)


def _out_kernel(x_ref, oa_ref, ob_ref, za_ref, zb_ref, wout_ref, gple_ref, wg_ref, ple_ref, wp_ref, gfin_ref,
                o_ref, *, final):
    mixed = jnp.concatenate([oa_ref[...] * za_ref[...], ob_ref[...] * zb_ref[...]], axis=-1).astype(BF16)
    x1 = x_ref[...] + _dot(mixed, wout_ref[...])
    gate = _sigmoid(_dot(_rms(x1, gple_ref[...]).astype(BF16), wg_ref[...]))
    x2 = x1 + gate * _dot(ple_ref[...].astype(BF16), wp_ref[...])
    o_ref[...] = _rms(x2, gfin_ref[...]) if final else x2


def _out(x2d, oa, ob, za, zb, w_out, g_ple, w_gate, ple, w_proj, g_final, final):
    n, d_model = x2d.shape
    tm = min(256, n)
    row = lambda a: pl.BlockSpec((tm, a.shape[1]), lambda i: (i, 0))
    args = (x2d, oa, ob, za, zb, w_out, g_ple, w_gate, ple, w_proj, g_final)
    specs = [row(x2d), row(oa), row(ob), row(za), row(zb), _resident(w_out), _resident(g_ple), _resident(w_gate),
             row(ple), _resident(w_proj), _resident(g_final)]
    return pl.pallas_call(
        functools.partial(_out_kernel, final=final),
        grid=(n // tm,),
        in_specs=specs,
        out_specs=row(x2d),
        out_shape=jax.ShapeDtypeStruct((n, d_model), F32),
        compiler_params=_params(("parallel",)),
        name="out",
    )(*args)


def _mixer_layer(x, ple, pos, past, win_prev, weights, g_final, final):
    (g_mix, w_split, w_out, cmp_k, cmp_v, g_ple, w_gate, w_proj) = weights
    b, t, d_model = x.shape
    n = b * t
    qa, qar, paged, win_new, gate, za, qb, zb = _proj(x.reshape(n, d_model), pos, g_mix, w_split)
    paged3 = paged.reshape(b, t, PAGED_WIDTH)
    win_new3 = win_new.reshape(b, t, 2 * KV_WIDTH)
    tq = min(128, t)
    assert KEY_TILE % tq == 0 and tq & (tq - 1) == 0
    span = _round_up(WINDOW + tq, LANES)
    win_rows = max(WINDOW + t, t - tq + span)
    win = jnp.concatenate([win_prev, win_new3, jnp.zeros((b, win_rows - WINDOW - t, 2 * KV_WIDTH), F32)], axis=1)
    qa3, qar3, gate3, qb3 = (a.reshape(b, t, -1) for a in (qa, qar, gate, qb))
    if past is None:
        assert t % KEY_TILE == 0
        nc_pad = _round_up(t // CMP_STRIDE, LANES)
        kc = _compress(paged3, 0, *cmp_k, nc_pad)
        vc = _compress(paged3, 1, *cmp_v, nc_pad)
        o_a = _nsa(qa3, qar3, gate3, kc, vc, paged3, win, tq)
        o_b = _moba(qb3, _kmean(paged3, LANES), paged3, tq)
    else:
        o_a, o_b = _sample_attn(*past, paged3, qa3, qar3, gate3, qb3, win, cmp_k, cmp_v)
    x_next = _out(x.reshape(n, d_model), o_a.reshape(n, -1), o_b.reshape(n, -1), za, zb, w_out, g_ple, w_gate,
                  ple.reshape(n, -1), w_proj, g_final, final)
    return x_next.reshape(b, t, d_model), paged3, win_new3


def _split_w_in(w_in):
    sizes = (NSA_WIDTH, 6 * KV_WIDTH, 3 * NSA_HEADS, NSA_WIDTH, MOBA_WIDTH, 2 * KV_WIDTH, MOBA_WIDTH)
    offs = np.concatenate([[0], np.cumsum(sizes)])
    wqa, wkva, wgate, wza, wqb, wkvb, wzb = [w_in[:, offs[k]:offs[k + 1]].astype(BF16) for k in range(len(sizes))]
    wgate = jnp.pad(wgate, ((0, 0), (0, LANES - wgate.shape[1])))
    return wqa, wkva, wgate, wza, wqb, wkvb, wzb


def kernel(x_prompt, x_sample, cache_paged_kv, cache_win_kv, page_table, p_prompt, p_sample, g_mix, w_in, w_out,
           cmp_pos_k, cmp_w1_k, cmp_w2_k, cmp_pos_v, cmp_w1_v, cmp_w2_v, g_ple, w_ple_gate, w_ple_proj, g_final):
    depth = w_in.shape[0]
    bp, tp, _ = x_prompt.shape
    bs, ts, _ = x_sample.shape
    n_pool, page = cache_paged_kv.shape[1], cache_paged_kv.shape[2]
    past_len = page_table.shape[1] * page
    wbuf = cache_win_kv.shape[2]
    pos_p = jnp.arange(tp, dtype=jnp.int32)
    pos_s = past_len + jnp.arange(ts, dtype=jnp.int32)
    pool_t = cache_paged_kv.transpose(0, 1, 3, 4, 5, 2).reshape(depth * n_pool, PAGED_WIDTH, page)
    g_fin = g_final.reshape(1, -1)
    xp, xs = x_prompt, x_sample
    new_pp, new_pw, new_sp, new_sw = [], [], [], []
    for i in range(depth):
        final = i == depth - 1
        weights = (g_mix[i].reshape(1, -1), _split_w_in(w_in[i]), w_out[i].astype(BF16),
                   _compress_weights(cmp_pos_k[i], cmp_w1_k[i], cmp_w2_k[i]),
                   _compress_weights(cmp_pos_v[i], cmp_w1_v[i], cmp_w2_v[i]),
                   g_ple[i].reshape(1, -1), w_ple_gate[i].astype(BF16), w_ple_proj[i].astype(BF16))
        win0 = jnp.zeros((bp, WINDOW, 2 * KV_WIDTH), F32)
        xp, pk, pw = _mixer_layer(xp, p_prompt[i], pos_p, None, win0, weights, g_fin, final)
        win_prev = jnp.concatenate([jnp.zeros((bs, WINDOW - wbuf, 2 * KV_WIDTH), F32),
                                    cache_win_kv[i].reshape(bs, wbuf, 2 * KV_WIDTH)], axis=1)
        xs, sk, sw = _mixer_layer(xs, p_sample[i], pos_s, (pool_t, page_table, i * n_pool), win_prev, weights,
                                  g_fin, final)
        new_pp.append(pk.reshape(bp, tp, N_PAGED_SLOTS, KV_HEADS, HEAD_DIM))
        new_pw.append(pw[:, tp - min(WINDOW, tp):].reshape(bp, min(WINDOW, tp), 2, KV_HEADS, HEAD_DIM))
        new_sp.append(sk.reshape(bs, ts, N_PAGED_SLOTS, KV_HEADS, HEAD_DIM))
        new_sw.append(sw.reshape(bs, ts, 2, KV_HEADS, HEAD_DIM))
    return (xp, xs, jnp.stack(new_pp), jnp.stack(new_pw), jnp.stack(new_sp), jnp.stack(new_sw))
```

```python
import functools

import numpy as np
import jax
import jax.numpy as jnp
from jax import lax
from jax.experimental import pallas as pl
from jax.experimental.pallas import tpu as pltpu

HEAD_DIM = 64
NSA_HEADS = 8
MOBA_HEADS = 8
KV_HEADS = 2
GROUP = NSA_HEADS // KV_HEADS
ROT_DIM = HEAD_DIM // 4
ROPE_THETA = 500000.0
CMP_LEN = 32
CMP_STRIDE = 16
SEL_BLOCK = 64
SEL_TOPK = 8
WINDOW = 512
MOBA_BLOCK = 256
MOBA_TOPK = 3
N_PAGED_SLOTS = 6
RMS_EPS = 1e-6
NEG_INF = -1e30
FORCE_SCORE = 1e4
SCALE = HEAD_DIM ** -0.5
LOG2E = 1.4426950408889634
SEL_SHIFT = SEL_BLOCK.bit_length() - 1
MOBA_SHIFT = MOBA_BLOCK.bit_length() - 1

LANES = 128
KV_WIDTH = KV_HEADS * HEAD_DIM
NSA_WIDTH = NSA_HEADS * HEAD_DIM
MOBA_WIDTH = MOBA_HEADS * HEAD_DIM
PAGED_WIDTH = N_PAGED_SLOTS * KV_WIDTH
KEY_TILE = 256
PAGES_PER_STEP = 8
VMEM_LIMIT = 56 * 1024 * 1024

F32 = jnp.float32
BF16 = jnp.bfloat16
NT_DIMS = (((1,), (1,)), ((), ()))


def _round_up(n, m):
    return -(-n // m) * m


def _params(sem):
    return pltpu.CompilerParams(dimension_semantics=sem, vmem_limit_bytes=VMEM_LIMIT)


def _rms(x, g):
    return x * lax.rsqrt(jnp.mean(x * x, axis=-1, keepdims=True) + RMS_EPS) * g


def _sigmoid(x):
    return 1.0 / (1.0 + jnp.exp(-x))


def _dot(a, b):
    return jnp.dot(a, b, preferred_element_type=F32)


def _dot_nt(a, b):
    return lax.dot_general(a, b, NT_DIMS, preferred_element_type=F32)


def _dot_nt_exact(a, b):
    return lax.dot_general(a, b, NT_DIMS, preferred_element_type=F32, precision=lax.Precision.HIGHEST)


def _resident(a):
    return pl.BlockSpec(a.shape, lambda *_: (0,) * a.ndim, pipeline_mode=pl.Buffered(1))


def _rope128(v, c, sp, sm):
    half = ROT_DIM // 2
    return v * c + pltpu.roll(v, half, axis=1) * sp + pltpu.roll(v, LANES - half, axis=1) * sm


def _rope(v, c, sp, sm):
    n = v.shape[-1] // LANES
    return jnp.concatenate([_rope128(v[:, i * LANES:(i + 1) * LANES], c, sp, sm) for i in range(n)], axis=-1)


def _proj_kernel(x_ref, g_ref, c_ref, sp_ref, sm_ref, wqa, wkva, wgate, wza, wqb, wkvb, wzb,
                 qa_o, qar_o, paged_o, win_o, gate_o, za_o, qb_o, zb_o):
    h = _rms(x_ref[...], g_ref[...]).astype(BF16)
    c, sp, sm = c_ref[...], sp_ref[...], sm_ref[...]
    qa = _dot(h, wqa[...])
    qa_o[...] = qa
    qar_o[...] = _rope(qa, c, sp, sm)
    kva = _dot(h, wkva[...])
    paged_o[:, 0:2 * KV_WIDTH] = kva[:, 0:2 * KV_WIDTH]
    paged_o[:, 2 * KV_WIDTH:3 * KV_WIDTH] = _rope(kva[:, 2 * KV_WIDTH:3 * KV_WIDTH], c, sp, sm)
    paged_o[:, 3 * KV_WIDTH:4 * KV_WIDTH] = kva[:, 3 * KV_WIDTH:4 * KV_WIDTH]
    win_o[:, 0:KV_WIDTH] = _rope(kva[:, 4 * KV_WIDTH:5 * KV_WIDTH], c, sp, sm)
    win_o[:, KV_WIDTH:2 * KV_WIDTH] = kva[:, 5 * KV_WIDTH:6 * KV_WIDTH]
    kvb = _dot(h, wkvb[...])
    paged_o[:, 4 * KV_WIDTH:5 * KV_WIDTH] = _rope(kvb[:, 0:KV_WIDTH], c, sp, sm)
    paged_o[:, 5 * KV_WIDTH:6 * KV_WIDTH] = kvb[:, KV_WIDTH:2 * KV_WIDTH]
    gate_o[...] = _sigmoid(_dot(h, wgate[...]))
    za = _dot(h, wza[...])
    za_o[...] = za * _sigmoid(za)
    qb_o[...] = _rope(_dot(h, wqb[...]), c, sp, sm)
    zb = _dot(h, wzb[...])
    zb_o[...] = zb * _sigmoid(zb)


def _rope_tables(pos, n_rows):
    half = ROT_DIM // 2
    inv_freq = ROPE_THETA ** (-jnp.arange(half, dtype=F32) / half)
    ang = pos.astype(F32)[:, None] * inv_freq[None, :]
    cos, sin = jnp.cos(ang), jnp.sin(ang)
    d = np.arange(LANES) % HEAD_DIM
    idx = d % half
    c = jnp.where(d < ROT_DIM, cos[:, idx], 1.0)
    sp = jnp.where((d >= half) & (d < ROT_DIM), sin[:, idx], 0.0)
    sm = jnp.where(d < half, -sin[:, idx], 0.0)
    rep = n_rows // pos.shape[0]
    return tuple(jnp.tile(t, (rep, 1)) for t in (c, sp, sm))


def _proj(x2d, pos, g, w):
    n, d_model = x2d.shape
    t = pos.shape[0]
    tm = min(256, n)
    tab_rows = max(t, tm)
    tabs = _rope_tables(pos, tab_rows)
    n_tab = tab_rows // tm
    row = lambda width: pl.BlockSpec((tm, width), lambda i: (i, 0))
    tab = pl.BlockSpec((tm, LANES), lambda i: (i % n_tab, 0))
    widths = (NSA_WIDTH, NSA_WIDTH, PAGED_WIDTH, 2 * KV_WIDTH, LANES, NSA_WIDTH, MOBA_WIDTH, MOBA_WIDTH)
    return pl.pallas_call(
        _proj_kernel,
        grid=(n // tm,),
        in_specs=[row(d_model), _resident(g), tab, tab, tab] + [_resident(a) for a in w],
        out_specs=[row(wd) for wd in widths],
        out_shape=[jax.ShapeDtypeStruct((n, wd), F32) for wd in widths],
        compiler_params=_params(("parallel",)),
        name="proj",
    )(x2d, g, *tabs, *w)


def _compress_rows(chunk_row, pos_ref, w1_ref, w2_ref, m):
    hid2 = w2_ref.shape[0]
    xcat = jnp.concatenate([chunk_row(l).astype(BF16) for l in range(CMP_STRIDE)], axis=-1)
    ab = _dot(xcat, w1_ref[...])
    pb = _dot(pos_ref[...], w1_ref[...])
    bias = pb[0:1, 0:hid2] + pb[1:2, hid2:2 * hid2]
    nxt = pltpu.roll(ab[:, hid2:2 * hid2], m - 1, axis=0)
    pre = ab[:, 0:hid2] + nxt + bias
    hid = pre * _sigmoid(pre)
    return _dot(hid.astype(BF16), w2_ref[...])


def _compress_kernel(src_ref, pos_ref, w1_ref, w2_ref, o_ref, *, n_chunks, nc_pad, transposed):
    m = n_chunks
    out = _compress_rows(lambda l: src_ref[0, pl.ds(l, m, stride=CMP_STRIDE), :], pos_ref, w1_ref, w2_ref, m)
    if nc_pad > m:
        out = jnp.concatenate([out, jnp.zeros((nc_pad - m, LANES), F32)], axis=0)
    o_ref[0] = out.T if transposed else out


def _compress(full, slot, pos_rows, w1, w2, nc_pad, transposed):
    b, l, _ = full.shape
    n_chunks = l // CMP_STRIDE
    out_dims = (KV_WIDTH, nc_pad) if transposed else (nc_pad, KV_WIDTH)
    return pl.pallas_call(
        functools.partial(_compress_kernel, n_chunks=n_chunks, nc_pad=nc_pad, transposed=transposed),
        grid=(b,),
        in_specs=[pl.BlockSpec((1, l, KV_WIDTH), lambda i: (i, 0, slot)),
                  _resident(pos_rows), _resident(w1), _resident(w2)],
        out_specs=pl.BlockSpec((1,) + out_dims, lambda i: (i, 0, 0)),
        out_shape=jax.ShapeDtypeStruct((b,) + out_dims, F32),
        compiler_params=_params(("parallel",)),
        name="compress",
    )(full, pos_rows, w1, w2)


def _compress_weights(pos_emb, w1, w2):
    hidden = w1.shape[-1]
    eye = jnp.eye(KV_HEADS, dtype=F32)

    def half(w):
        return jnp.einsum('ldf,hg->lhdgf', w, eye).reshape(CMP_STRIDE * KV_WIDTH, KV_HEADS * hidden)

    w1bd = jnp.concatenate([half(w1[:CMP_STRIDE]), half(w1[CMP_STRIDE:])], axis=1).astype(BF16)
    w2bd = jnp.einsum('fd,hg->hfgd', w2, eye).reshape(KV_HEADS * hidden, KV_WIDTH).astype(BF16)

    def pos_row(p):
        return jnp.broadcast_to(p[:, None, :], (CMP_STRIDE, KV_HEADS, HEAD_DIM)).reshape(-1)

    pos_rows = jnp.zeros((8, CMP_STRIDE * KV_WIDTH), F32)
    pos_rows = pos_rows.at[0].set(pos_row(pos_emb[:CMP_STRIDE])).at[1].set(pos_row(pos_emb[CMP_STRIDE:]))
    return pos_rows.astype(BF16), w1bd, w2bd


def _kmean_kernel(src_ref, o_ref, *, n_blocks):
    x = src_ref[0].reshape(n_blocks, MOBA_BLOCK, KV_WIDTH)
    o_ref[0, 0:n_blocks, :] = jnp.sum(x, axis=1) * (1.0 / MOBA_BLOCK)
    o_ref[0, n_blocks:, :] = jnp.zeros((o_ref.shape[1] - n_blocks, KV_WIDTH), F32)


def _kmean(full, nb_pad):
    b, l, _ = full.shape
    return pl.pallas_call(
        functools.partial(_kmean_kernel, n_blocks=l // MOBA_BLOCK),
        grid=(b,),
        in_specs=[pl.BlockSpec((1, l, KV_WIDTH), lambda i: (i, 0, 4))],
        out_specs=pl.BlockSpec((1, nb_pad, KV_WIDTH), lambda i: (i, 0, 0)),
        out_shape=jax.ShapeDtypeStruct((b, nb_pad, KV_WIDTH), F32),
        compiler_params=_params(("parallel",)),
        name="kmean",
    )(full)


def _stack_heads(ref, kvh):
    return jnp.concatenate(
        [ref[0, :, (GROUP * kvh + g) * HEAD_DIM:(GROUP * kvh + g + 1) * HEAD_DIM] for g in range(GROUP)], axis=0)


def _masked_softmax(s, valid):
    s = jnp.where(valid, s, NEG_INF)
    e = jnp.where(valid, jnp.exp(s - jnp.max(s, axis=-1, keepdims=True)), 0.0)
    l = jnp.sum(e, axis=-1, keepdims=True)
    return e / jnp.where(l > 0.0, l, 1.0)


def _topk_mask(score, k, axis):
    n = score.shape[axis]
    idx = lax.broadcasted_iota(jnp.int32, score.shape, axis).astype(F32)
    sel = jnp.zeros(score.shape, F32)
    s = score
    for _ in range(k):
        m = jnp.max(s, axis=axis, keepdims=True)
        first = jnp.min(jnp.where(s == m, idx, float(n)), axis=axis, keepdims=True)
        pick = idx == first
        sel = jnp.where(pick & (m > 0.5 * NEG_INF), 1.0, sel)
        s = jnp.where(pick, -3e38, s)
    return sel


def _block_bias(chosen):
    return ((chosen - 1.0) * -NEG_INF).astype(BF16)


def _flash_t(q_aug, k_ref, v_ref, diag, t_col, expand_t):
    n_heads = len(q_aug)

    def scores(j):
        k0 = pl.multiple_of(j * KEY_TILE, KEY_TILE)
        kt = k_ref[0, pl.ds(k0, KEY_TILE), :].astype(BF16)
        lhs = jnp.concatenate([kt, expand_t(j)], axis=1)
        return [_dot(lhs, qa) for qa in q_aug]

    def values_t(j):
        k0 = pl.multiple_of(j * KEY_TILE, KEY_TILE)
        vt = v_ref[0, pl.ds(k0, KEY_TILE), :].T.astype(BF16)
        return [vt[h * HEAD_DIM:(h + 1) * HEAD_DIM] for h in range(n_heads)]

    visible = diag * KEY_TILE + lax.broadcasted_iota(jnp.int32, (KEY_TILE, 1), 0) <= t_col
    init = []
    for s, vt in zip(scores(diag), values_t(diag)):
        s = jnp.where(visible, s, NEG_INF)
        m0 = jnp.max(s, axis=0, keepdims=True)
        p = jnp.exp2(s - m0)
        init += [m0, jnp.sum(p, axis=0, keepdims=True), _dot(vt, p.astype(BF16))]

    def body(j, carry):
        out = []
        for h, (s, vt) in enumerate(zip(scores(j), values_t(j))):
            m_i, l_i, acc = carry[3 * h:3 * h + 3]
            m_new = jnp.maximum(m_i, jnp.max(s, axis=0, keepdims=True))
            alpha = jnp.exp2(m_i - m_new)
            p = jnp.exp2(s - m_new)
            out += [m_new, alpha * l_i + jnp.sum(p, axis=0, keepdims=True), alpha * acc + _dot(vt, p.astype(BF16))]
        return tuple(out)

    res = lax.fori_loop(0, diag, body, tuple(init))
    return [res[3 * h + 2] / res[3 * h + 1] for h in range(n_heads)]


def _head_cols(x_t, kvh):
    x = jnp.concatenate([x_t[(GROUP * kvh + g) * HEAD_DIM:(GROUP * kvh + g + 1) * HEAD_DIM] for g in range(GROUP)],
                        axis=1)
    return jnp.concatenate([x if k == kvh else jnp.zeros_like(x) for k in range(KV_HEADS)], axis=0)


def _selection_map(n_cmp, nc_pad, ns_pad):
    c0 = np.arange(nc_pad)[:, None] * CMP_STRIDE
    j0 = np.arange(ns_pad)[None, :] * SEL_BLOCK
    ov = np.clip(np.minimum(c0 + CMP_LEN, j0 + SEL_BLOCK) - np.maximum(c0, j0), 0, None) / CMP_STRIDE
    return np.where(np.arange(nc_pad)[:, None] < n_cmp, ov, 0.0).astype(np.float32)


def _window_branch(qr, win_ref, w0, span, lo, w_valid):
    wk = win_ref[0, pl.ds(w0, span), lo:lo + HEAD_DIM].astype(BF16)
    wv = win_ref[0, pl.ds(w0, span), KV_WIDTH + lo:KV_WIDTH + lo + HEAD_DIM].astype(BF16)
    s = jnp.where(w_valid, _dot_nt(qr, wk), NEG_INF)
    e = jnp.exp(s - jnp.max(s, axis=-1, keepdims=True))
    return _dot(e.astype(BF16), wv) / jnp.sum(e, axis=-1, keepdims=True)


def _write_heads(o_ref, kvh, tq, head_out):
    for g in range(GROUP):
        h = GROUP * kvh + g
        o_ref[0, :, h * HEAD_DIM:(h + 1) * HEAD_DIM] = head_out(g, h, slice(g * tq, (g + 1) * tq))


def _nsa_kernel(qa_ref, qar_ref, gate_ref, kc_ref, vct_ref, sk_ref, sv_ref, win_ref, mapt_ref, o_ref, *, tq, span):
    i = pl.program_id(1)
    t0 = i * tq
    r = GROUP * tq
    nc_pad = kc_ref.shape[1]
    ns_pad = mapt_ref.shape[0]
    t_col = t0 + (lax.broadcasted_iota(jnp.int32, (1, r), 1) & (tq - 1))
    c_end = lax.broadcasted_iota(jnp.int32, (nc_pad, 1), 0) * CMP_STRIDE + CMP_LEN
    blk_t = lax.broadcasted_iota(jnp.int32, (ns_pad, 1), 0)
    blk_lane = lax.broadcasted_iota(jnp.int32, (1, ns_pad), 1)
    key_blk = lax.broadcasted_iota(jnp.int32, (KEY_TILE, 1), 0) >> SEL_SHIFT
    jt_t = (t0 + lax.broadcasted_iota(jnp.int32, (1, tq), 1)) >> SEL_SHIFT
    diag = t0 // KEY_TILE
    w0 = pl.multiple_of(i * tq, 8)
    wpos = t0 - WINDOW + lax.broadcasted_iota(jnp.int32, (span, 1), 0)
    w_valid = (wpos >= 0) & (wpos <= t_col) & (wpos >= t_col - WINDOW)
    c_valid = c_end <= t_col + 1

    qa_t = qa_ref[0].T
    qar_t = qar_ref[0].T
    gates_t = gate_ref[0].T
    kc = kc_ref[0].astype(BF16)
    wk = win_ref[0, pl.ds(w0, span), 0:KV_WIDTH].astype(BF16)
    wv_t = win_ref[0, pl.ds(w0, span), KV_WIDTH:2 * KV_WIDTH].T.astype(BF16)

    def expand_t(j):
        return (key_blk + j * (KEY_TILE // SEL_BLOCK) == blk_lane).astype(BF16)

    q_aug, o_cmp_t, o_win_t = [], [], []
    for kvh in range(KV_HEADS):
        rows = slice(kvh * HEAD_DIM, (kvh + 1) * HEAD_DIM)
        q_t = (_head_cols(qa_t, kvh) * SCALE).astype(BF16)
        s = jnp.where(c_valid, _dot(kc, q_t), NEG_INF)
        e = jnp.where(c_valid, jnp.exp(s - jnp.max(s, axis=0, keepdims=True)), 0.0)
        l = jnp.sum(e, axis=0, keepdims=True)
        p_cmp = e / jnp.where(l > 0.0, l, 1.0)
        o_cmp_t.append(_dot(vct_ref[0, rows, :].astype(BF16), p_cmp.astype(BF16)))

        p_sum = p_cmp[:, 0:tq]
        for g in range(1, GROUP):
            p_sum = p_sum + p_cmp[:, g * tq:(g + 1) * tq]
        imp_t = jnp.dot(mapt_ref[...], p_sum, preferred_element_type=F32, precision=lax.Precision.HIGHEST)
        score_t = jnp.where((blk_t == jt_t) | (blk_t == 0), FORCE_SCORE, jnp.where(blk_t < jt_t, imp_t, NEG_INF))
        bias_t = _block_bias(_topk_mask(score_t, SEL_TOPK, 0))
        qr_t = (_head_cols(qar_t, kvh) * (SCALE * LOG2E)).astype(BF16)
        q_aug.append(jnp.concatenate([qr_t, jnp.concatenate([bias_t] * GROUP, axis=1)], axis=0))

        s = jnp.where(w_valid, _dot(wk, qr_t), NEG_INF)
        e = jnp.exp2(s - jnp.max(s, axis=0, keepdims=True))
        o_win_t.append(_dot(wv_t[rows], e.astype(BF16)) / jnp.sum(e, axis=0, keepdims=True))

    o_sel_t = _flash_t(q_aug, sk_ref, sv_ref, diag, t_col, expand_t)

    heads = []
    for kvh in range(KV_HEADS):
        for g in range(GROUP):
            h = GROUP * kvh + g
            cols = slice(g * tq, (g + 1) * tq)
            heads.append(gates_t[3 * h:3 * h + 1] * o_cmp_t[kvh][:, cols]
                         + gates_t[3 * h + 1:3 * h + 2] * o_sel_t[kvh][:, cols]
                         + gates_t[3 * h + 2:3 * h + 3] * o_win_t[kvh][:, cols])
    o_ref[0] = jnp.concatenate(heads, axis=0).T


def _nsa(qa, qar, gate, kc, vc_t, full, win, tq):
    b, t, _ = qa.shape
    l = full.shape[1]
    nc_pad = kc.shape[1]
    ns_pad = _round_up(l // SEL_BLOCK, LANES)
    span = _round_up(WINDOW + tq, LANES)
    map_t = jnp.asarray(_selection_map(l // CMP_STRIDE - 1, nc_pad, ns_pad).T)
    qspec = pl.BlockSpec((1, tq, NSA_WIDTH), lambda bi, i: (bi, i, 0))
    seq = lambda a: pl.BlockSpec((1,) + a.shape[1:], lambda bi, i: (bi, 0, 0))
    return pl.pallas_call(
        functools.partial(_nsa_kernel, tq=tq, span=span),
        grid=(b, t // tq),
        in_specs=[
            qspec, qspec,
            pl.BlockSpec((1, tq, LANES), lambda bi, i: (bi, i, 0)),
            seq(kc), seq(vc_t),
            pl.BlockSpec((1, l, KV_WIDTH), lambda bi, i: (bi, 0, 2)),
            pl.BlockSpec((1, l, KV_WIDTH), lambda bi, i: (bi, 0, 3)),
            seq(win),
            _resident(map_t),
        ],
        out_specs=qspec,
        out_shape=jax.ShapeDtypeStruct((b, t, NSA_WIDTH), F32),
        compiler_params=_params(("parallel", "arbitrary")),
        name="nsa",
    )(qa, qar, gate, kc, vc_t, full, full, win, map_t)


def _moba_kernel(qb_ref, km_ref, mk_ref, mv_ref, o_ref, *, tq):
    i = pl.program_id(1)
    t0 = i * tq
    r = GROUP * tq
    nb_pad = km_ref.shape[1]
    t_col = t0 + (lax.broadcasted_iota(jnp.int32, (1, r), 1) & (tq - 1))
    bt_t = t_col >> MOBA_SHIFT
    nblk_t = lax.broadcasted_iota(jnp.int32, (nb_pad, 1), 0)
    nblk_lane = lax.broadcasted_iota(jnp.int32, (1, nb_pad), 1)
    diag = t0 // KEY_TILE
    qb_t = qb_ref[0].T
    km = km_ref[0]

    def expand_t(j):
        return jnp.broadcast_to(nblk_lane == j, (KEY_TILE, nb_pad)).astype(BF16)

    q_aug = []
    for kvh in range(KV_HEADS):
        q_t = _head_cols(qb_t, kvh)
        gate_t = jnp.dot(km, q_t, preferred_element_type=F32, precision=lax.Precision.HIGHEST)
        gate_t = jnp.where(nblk_t < bt_t, gate_t, NEG_INF)
        chosen_t = jnp.where(nblk_t == bt_t, 1.0, _topk_mask(gate_t, MOBA_TOPK, 0))
        q_aug.append(jnp.concatenate([(q_t * (SCALE * LOG2E)).astype(BF16), _block_bias(chosen_t)], axis=0))

    o_t = _flash_t(q_aug, mk_ref, mv_ref, diag, t_col, expand_t)
    heads = [o_t[kvh][:, g * tq:(g + 1) * tq] for kvh in range(KV_HEADS) for g in range(GROUP)]
    o_ref[0] = jnp.concatenate(heads, axis=0).T


def _moba(qb, kmean, full, tq):
    b, t, _ = qb.shape
    l = full.shape[1]
    qspec = pl.BlockSpec((1, tq, MOBA_WIDTH), lambda bi, i: (bi, i, 0))
    return pl.pallas_call(
        functools.partial(_moba_kernel, tq=tq),
        grid=(b, t // tq),
        in_specs=[
            qspec,
            pl.BlockSpec((1,) + kmean.shape[1:], lambda bi, i: (bi, 0, 0)),
            pl.BlockSpec((1, l, KV_WIDTH), lambda bi, i: (bi, 0, 4)),
            pl.BlockSpec((1, l, KV_WIDTH), lambda bi, i: (bi, 0, 5)),
        ],
        out_specs=qspec,
        out_shape=jax.ShapeDtypeStruct((b, t, MOBA_WIDTH), F32),
        compiler_params=_params(("parallel", "arbitrary")),
        name="moba",
    )(qb, kmean, full, full)


def _softmax_pv_t(s, vt):
    e = jnp.exp(s - jnp.max(s, axis=-1, keepdims=True))
    return _dot_nt(e.astype(BF16), vt) / jnp.sum(e, axis=-1, keepdims=True)


def _sample_kernel(pt_ref, *refs, past_len, ts, span):
    pages = refs[:PAGES_PER_STEP]
    (new_ref, qa_ref, qar_ref, gate_ref, qb_ref, win_ref, map_ref, esel_ref, emoba_ref,
     posk_ref, w1k_ref, w2k_ref, posv_ref, w1v_ref, w2v_ref, oa_ref, ob_ref, tok_scr, kv_scr) = refs[PAGES_PER_STEP:]
    del pt_ref
    step = pl.program_id(1)
    n_steps = pl.num_programs(1)
    page = pages[0].shape[2]
    cmp_rows = 2 * KV_WIDTH

    for j in range(PAGES_PER_STEP):
        tok0 = pl.multiple_of((step * PAGES_PER_STEP + j) * page, page)
        for slot in range(2):
            tok_scr[slot, pl.ds(tok0, page), :] = pages[j][0, slot * KV_WIDTH:(slot + 1) * KV_WIDTH, :].T
    for k in range(past_len // (PAGES_PER_STEP * page)):
        @pl.when(step == k)
        def _(k=k):
            for j in range(PAGES_PER_STEP):
                c0 = (k * PAGES_PER_STEP + j) * page
                kv_scr[:, c0:c0 + page] = pages[j][0, cmp_rows:, :].astype(BF16)

    @pl.when(step == n_steps - 1)
    def _():
        lc = kv_scr.shape[1]
        r = GROUP * ts
        new_pad = jnp.concatenate([new_ref[0], jnp.zeros((page - ts, PAGED_WIDTH), F32)], axis=0)
        kv_scr[:, past_len:past_len + page] = new_pad.T[cmp_rows:, :].astype(BF16)

        m = past_len // CMP_STRIDE
        kc_all = _compress_rows(lambda l: tok_scr[0, pl.ds(l, m, stride=CMP_STRIDE), :], posk_ref, w1k_ref, w2k_ref, m)
        vc_all = _compress_rows(lambda l: tok_scr[1, pl.ds(l, m, stride=CMP_STRIDE), :], posv_ref, w1v_ref, w2v_ref, m)

        ns_pad = map_ref.shape[1]
        nb_pad = emoba_ref.shape[0]
        row = lax.broadcasted_iota(jnp.int32, (r, 1), 0)
        t_row = past_len + (row & (ts - 1))
        t_tok = past_len + lax.broadcasted_iota(jnp.int32, (ts, 1), 0)
        causal = lax.broadcasted_iota(jnp.int32, (1, lc), 1) <= t_row
        c_end = lax.broadcasted_iota(jnp.int32, (1, m), 1) * CMP_STRIDE + CMP_LEN
        blk = lax.broadcasted_iota(jnp.int32, (1, ns_pad), 1)
        nblk = lax.broadcasted_iota(jnp.int32, (1, nb_pad), 1)
        jt = t_tok >> SEL_SHIFT
        bt = t_row >> MOBA_SHIFT
        wpos = past_len - WINDOW + lax.broadcasted_iota(jnp.int32, (1, span), 1)
        w_valid = (wpos >= 0) & (wpos <= t_row) & (wpos >= t_row - WINDOW)
        gates = gate_ref[0]

        for kvh in range(KV_HEADS):
            lo = kvh * HEAD_DIM
            q = (_stack_heads(qa_ref, kvh) * SCALE).astype(BF16)
            qr = (_stack_heads(qar_ref, kvh) * SCALE).astype(BF16)

            p_cmp = _masked_softmax(_dot_nt(q, kc_all[:, lo:lo + HEAD_DIM].astype(BF16)), c_end <= t_row + 1)
            o_cmp = _dot(p_cmp.astype(BF16), vc_all[:, lo:lo + HEAD_DIM].astype(BF16))
            p_sum = p_cmp[0:ts]
            for g in range(1, GROUP):
                p_sum = p_sum + p_cmp[g * ts:(g + 1) * ts]
            imp = jnp.dot(p_sum, map_ref[...], preferred_element_type=F32, precision=lax.Precision.HIGHEST)
            score = jnp.where((blk == jt) | (blk == 0), FORCE_SCORE, jnp.where(blk < jt, imp, NEG_INF))
            sel = _topk_mask(score, SEL_TOPK, 1)
            bias_rows = _block_bias(jnp.concatenate([sel] * GROUP, axis=0))

            s = _dot(qr, kv_scr[lo:lo + HEAD_DIM, :]) + _dot(bias_rows, esel_ref[...])
            o_sel = _softmax_pv_t(jnp.where(causal, s, NEG_INF), kv_scr[KV_WIDTH + lo:KV_WIDTH + lo + HEAD_DIM, :])
            o_win = _window_branch(qr, win_ref, 0, span, lo, w_valid)
            _write_heads(oa_ref, kvh, ts, lambda g, h, rows: (
                gates[:, 3 * h:3 * h + 1] * o_cmp[rows] + gates[:, 3 * h + 1:3 * h + 2] * o_sel[rows]
                + gates[:, 3 * h + 2:3 * h + 3] * o_win[rows]))

            qf = _stack_heads(qb_ref, kvh).astype(BF16)
            s_raw = _dot(qf, kv_scr[2 * KV_WIDTH + lo:2 * KV_WIDTH + lo + HEAD_DIM, :])
            s_hi = s_raw.astype(BF16)
            s_lo = (s_raw - s_hi.astype(F32)).astype(BF16)
            gate_s = (_dot_nt(s_hi, emoba_ref[...]) + _dot_nt(s_lo, emoba_ref[...])) * (1.0 / MOBA_BLOCK)
            gate_s = jnp.where(nblk < bt, gate_s, NEG_INF)
            chosen = jnp.where(nblk == bt, 1.0, _topk_mask(gate_s, MOBA_TOPK, 1))
            s = s_raw * SCALE + _dot(_block_bias(chosen), emoba_ref[...])
            o = _softmax_pv_t(jnp.where(causal, s, NEG_INF),
                              kv_scr[3 * KV_WIDTH + lo:3 * KV_WIDTH + lo + HEAD_DIM, :])
            _write_heads(ob_ref, kvh, ts, lambda g, h, rows: o[rows])


def _block_expansion(n_rows, block, n_keys):
    return jnp.asarray(np.arange(n_rows)[:, None] == (np.arange(n_keys)[None, :] // block), dtype=BF16)


def _sample_attn(pool_t, page_table, page_base, new_rows, qa, qar, gate, qb, win, cmp_k, cmp_v):
    bs, n_pages = page_table.shape
    page = pool_t.shape[2]
    ts = new_rows.shape[1]
    past_len = n_pages * page
    n_steps = n_pages // PAGES_PER_STEP
    assert page == LANES and n_pages % PAGES_PER_STEP == 0 and ts < CMP_STRIDE and ts & (ts - 1) == 0
    assert past_len % MOBA_BLOCK == 0 and (past_len // CMP_STRIDE) % LANES == 0
    lc = past_len + page
    ns_pad = _round_up(lc // SEL_BLOCK, LANES)
    nb_pad = _round_up(lc // MOBA_BLOCK + 1, LANES)
    span = win.shape[1]
    sel_map = jnp.asarray(_selection_map(past_len // CMP_STRIDE - 1, past_len // CMP_STRIDE, ns_pad))
    e_sel = _block_expansion(ns_pad, SEL_BLOCK, lc)
    e_moba = _block_expansion(nb_pad, MOBA_BLOCK, lc)

    def page_spec(j):
        return pl.BlockSpec((1, PAGED_WIDTH, page),
                            lambda b, s, pt: (page_base + pt[b, s * PAGES_PER_STEP + j], 0, 0))

    per_seq = lambda a: pl.BlockSpec((1,) + a.shape[1:], lambda b, s, pt: (b, 0, 0))
    consts = (sel_map, e_sel, e_moba) + tuple(cmp_k) + tuple(cmp_v)
    out_spec = pl.BlockSpec((1, ts, NSA_WIDTH), lambda b, s, pt: (b, 0, 0))
    grid_spec = pltpu.PrefetchScalarGridSpec(
        num_scalar_prefetch=1,
        grid=(bs, n_steps),
        in_specs=[page_spec(j) for j in range(PAGES_PER_STEP)]
        + [per_seq(a) for a in (new_rows, qa, qar, gate, qb, win)] + [_resident(a) for a in consts],
        out_specs=[out_spec, out_spec],
        scratch_shapes=[pltpu.VMEM((2, past_len, KV_WIDTH), F32),
                        pltpu.VMEM((PAGED_WIDTH - 2 * KV_WIDTH, lc), BF16)],
    )
    return pl.pallas_call(
        functools.partial(_sample_kernel, past_len=past_len, ts=ts, span=span),
        grid_spec=grid_spec,
        out_shape=[jax.ShapeDtypeStruct((bs, ts, NSA_WIDTH), F32)] * 2,
        compiler_params=_params(("parallel", "arbitrary")),
        name="sample_attn",
    )(page_table, *([pool_t] * PAGES_PER_STEP), new_rows, qa, qar, gate, qb, win, *consts)


def _out_kernel(x_ref, oa_ref, ob_ref, za_ref, zb_ref, wout_ref, gple_ref, wg_ref, ple_ref, wp_ref, gfin_ref,
                o_ref, *, final):
    mixed = jnp.concatenate([oa_ref[...] * za_ref[...], ob_ref[...] * zb_ref[...]], axis=-1).astype(BF16)
    x1 = x_ref[...] + _dot(mixed, wout_ref[...])
    gate = _sigmoid(_dot(_rms(x1, gple_ref[...]).astype(BF16), wg_ref[...]))
    x2 = x1 + gate * _dot(ple_ref[...].astype(BF16), wp_ref[...])
    o_ref[...] = _rms(x2, gfin_ref[...]) if final else x2


def _out(x2d, oa, ob, za, zb, w_out, g_ple, w_gate, ple, w_proj, g_final, final):
    n, d_model = x2d.shape
    tm = min(256, n)
    row = lambda a: pl.BlockSpec((tm, a.shape[1]), lambda i: (i, 0))
    args = (x2d, oa, ob, za, zb, w_out, g_ple, w_gate, ple, w_proj, g_final)
    specs = [row(x2d), row(oa), row(ob), row(za), row(zb), _resident(w_out), _resident(g_ple), _resident(w_gate),
             row(ple), _resident(w_proj), _resident(g_final)]
    return pl.pallas_call(
        functools.partial(_out_kernel, final=final),
        grid=(n // tm,),
        in_specs=specs,
        out_specs=row(x2d),
        out_shape=jax.ShapeDtypeStruct((n, d_model), F32),
        compiler_params=_params(("parallel",)),
        name="out",
    )(*args)


def _mixer_layer(x, ple, pos, past, win_prev, weights, g_final, final):
    (g_mix, w_split, w_out, cmp_k, cmp_v, g_ple, w_gate, w_proj) = weights
    b, t, d_model = x.shape
    n = b * t
    qa, qar, paged, win_new, gate, za, qb, zb = _proj(x.reshape(n, d_model), pos, g_mix, w_split)
    paged3 = paged.reshape(b, t, PAGED_WIDTH)
    win_new3 = win_new.reshape(b, t, 2 * KV_WIDTH)
    tq = min(128, t)
    assert KEY_TILE % tq == 0 and tq & (tq - 1) == 0
    span = _round_up(WINDOW + tq, LANES)
    win_rows = max(WINDOW + t, t - tq + span)
    win = jnp.concatenate([win_prev, win_new3, jnp.zeros((b, win_rows - WINDOW - t, 2 * KV_WIDTH), F32)], axis=1)
    qa3, qar3, gate3, qb3 = (a.reshape(b, t, -1) for a in (qa, qar, gate, qb))
    if past is None:
        assert t % KEY_TILE == 0
        nc_pad = _round_up(t // CMP_STRIDE, LANES)
        kc = _compress(paged3, 0, *cmp_k, nc_pad, False)
        vc_t = _compress(paged3, 1, *cmp_v, nc_pad, True)
        o_a = _nsa(qa3, qar3, gate3, kc, vc_t, paged3, win, tq)
        o_b = _moba(qb3, _kmean(paged3, LANES), paged3, tq)
    else:
        o_a, o_b = _sample_attn(*past, paged3, qa3, qar3, gate3, qb3, win, cmp_k, cmp_v)
    x_next = _out(x.reshape(n, d_model), o_a.reshape(n, -1), o_b.reshape(n, -1), za, zb, w_out, g_ple, w_gate,
                  ple.reshape(n, -1), w_proj, g_final, final)
    return x_next.reshape(b, t, d_model), paged3, win_new3


def _split_w_in(w_in):
    sizes = (NSA_WIDTH, 6 * KV_WIDTH, 3 * NSA_HEADS, NSA_WIDTH, MOBA_WIDTH, 2 * KV_WIDTH, MOBA_WIDTH)
    offs = np.concatenate([[0], np.cumsum(sizes)])
    wqa, wkva, wgate, wza, wqb, wkvb, wzb = [w_in[:, offs[k]:offs[k + 1]].astype(BF16) for k in range(len(sizes))]
    wgate = jnp.pad(wgate, ((0, 0), (0, LANES - wgate.shape[1])))
    return wqa, wkva, wgate, wza, wqb, wkvb, wzb


def kernel(x_prompt, x_sample, cache_paged_kv, cache_win_kv, page_table, p_prompt, p_sample, g_mix, w_in, w_out,
           cmp_pos_k, cmp_w1_k, cmp_w2_k, cmp_pos_v, cmp_w1_v, cmp_w2_v, g_ple, w_ple_gate, w_ple_proj, g_final):
    depth = w_in.shape[0]
    bp, tp, _ = x_prompt.shape
    bs, ts, _ = x_sample.shape
    n_pool, page = cache_paged_kv.shape[1], cache_paged_kv.shape[2]
    past_len = page_table.shape[1] * page
    wbuf = cache_win_kv.shape[2]
    pos_p = jnp.arange(tp, dtype=jnp.int32)
    pos_s = past_len + jnp.arange(ts, dtype=jnp.int32)
    pool_t = cache_paged_kv.transpose(0, 1, 3, 4, 5, 2).reshape(depth * n_pool, PAGED_WIDTH, page)
    g_fin = g_final.reshape(1, -1)
    xp, xs = x_prompt, x_sample
    new_pp, new_pw, new_sp, new_sw = [], [], [], []
    for i in range(depth):
        final = i == depth - 1
        weights = (g_mix[i].reshape(1, -1), _split_w_in(w_in[i]), w_out[i].astype(BF16),
                   _compress_weights(cmp_pos_k[i], cmp_w1_k[i], cmp_w2_k[i]),
                   _compress_weights(cmp_pos_v[i], cmp_w1_v[i], cmp_w2_v[i]),
                   g_ple[i].reshape(1, -1), w_ple_gate[i].astype(BF16), w_ple_proj[i].astype(BF16))
        win0 = jnp.zeros((bp, WINDOW, 2 * KV_WIDTH), F32)
        xp, pk, pw = _mixer_layer(xp, p_prompt[i], pos_p, None, win0, weights, g_fin, final)
        win_prev = jnp.concatenate([jnp.zeros((bs, WINDOW - wbuf, 2 * KV_WIDTH), F32),
                                    cache_win_kv[i].reshape(bs, wbuf, 2 * KV_WIDTH)], axis=1)
        xs, sk, sw = _mixer_layer(xs, p_sample[i], pos_s, (pool_t, page_table, i * n_pool), win_prev, weights,
                                  g_fin, final)
        new_pp.append(pk.reshape(bp, tp, N_PAGED_SLOTS, KV_HEADS, HEAD_DIM))
        new_pw.append(pw[:, tp - min(WINDOW, tp):].reshape(bp, min(WINDOW, tp), 2, KV_HEADS, HEAD_DIM))
        new_sp.append(sk.reshape(bs, ts, N_PAGED_SLOTS, KV_HEADS, HEAD_DIM))
        new_sw.append(sw.reshape(bs, ts, 2, KV_HEADS, HEAD_DIM))
    return (xp, xs, jnp.stack(new_pp), jnp.stack(new_pw), jnp.stack(new_sp), jnp.stack(new_sw))
```

```python
import functools

import numpy as np
import jax
import jax.numpy as jnp
from jax import lax
from jax.experimental import pallas as pl
from jax.experimental.pallas import tpu as pltpu

HEAD_DIM = 64
NSA_HEADS = 8
MOBA_HEADS = 8
KV_HEADS = 2
GROUP = NSA_HEADS // KV_HEADS
ROT_DIM = HEAD_DIM // 4
ROPE_THETA = 500000.0
CMP_LEN = 32
CMP_STRIDE = 16
SEL_BLOCK = 64
SEL_TOPK = 8
WINDOW = 512
MOBA_BLOCK = 256
MOBA_TOPK = 3
N_PAGED_SLOTS = 6
RMS_EPS = 1e-6
NEG_INF = -1e30
FORCE_SCORE = 1e4
SCALE = HEAD_DIM ** -0.5
LOG2E = 1.4426950408889634
SEL_SHIFT = SEL_BLOCK.bit_length() - 1
MOBA_SHIFT = MOBA_BLOCK.bit_length() - 1

LANES = 128
KV_WIDTH = KV_HEADS * HEAD_DIM
NSA_WIDTH = NSA_HEADS * HEAD_DIM
MOBA_WIDTH = MOBA_HEADS * HEAD_DIM
PAGED_WIDTH = N_PAGED_SLOTS * KV_WIDTH
KEY_TILE = 256
PAGES_PER_STEP = 8
BIAS_SEG = 2048
VMEM_LIMIT = 56 * 1024 * 1024

F32 = jnp.float32
BF16 = jnp.bfloat16
NT_DIMS = (((1,), (1,)), ((), ()))


def _round_up(n, m):
    return -(-n // m) * m


def _params(sem):
    return pltpu.CompilerParams(dimension_semantics=sem, vmem_limit_bytes=VMEM_LIMIT)


def _rms(x, g):
    return x * lax.rsqrt(jnp.mean(x * x, axis=-1, keepdims=True) + RMS_EPS) * g


def _sigmoid(x):
    return 1.0 / (1.0 + jnp.exp(-x))


def _dot(a, b):
    return jnp.dot(a, b, preferred_element_type=F32)


def _dot_nt(a, b):
    return lax.dot_general(a, b, NT_DIMS, preferred_element_type=F32)


def _dot_nt_exact(a, b):
    return lax.dot_general(a, b, NT_DIMS, preferred_element_type=F32, precision=lax.Precision.HIGHEST)


def _resident(a):
    return pl.BlockSpec(a.shape, lambda *_: (0,) * a.ndim, pipeline_mode=pl.Buffered(1))


def _rope128(v, c, sp, sm):
    half = ROT_DIM // 2
    return v * c + pltpu.roll(v, half, axis=1) * sp + pltpu.roll(v, LANES - half, axis=1) * sm


def _rope(v, c, sp, sm):
    n = v.shape[-1] // LANES
    return jnp.concatenate([_rope128(v[:, i * LANES:(i + 1) * LANES], c, sp, sm) for i in range(n)], axis=-1)


def _proj_kernel(x_ref, g_ref, c_ref, sp_ref, sm_ref, wqa, wkva, wgate, wza, wqb, wkvb, wzb,
                 qa_o, qar_o, paged_o, win_o, gate_o, za_o, qb_o, zb_o):
    h = _rms(x_ref[...], g_ref[...]).astype(BF16)
    c, sp, sm = c_ref[...], sp_ref[...], sm_ref[...]
    qa = _dot(h, wqa[...])
    qa_o[...] = qa
    qar_o[...] = _rope(qa, c, sp, sm)
    kva = _dot(h, wkva[...])
    paged_o[:, 0:2 * KV_WIDTH] = kva[:, 0:2 * KV_WIDTH]
    paged_o[:, 2 * KV_WIDTH:3 * KV_WIDTH] = _rope(kva[:, 2 * KV_WIDTH:3 * KV_WIDTH], c, sp, sm)
    paged_o[:, 3 * KV_WIDTH:4 * KV_WIDTH] = kva[:, 3 * KV_WIDTH:4 * KV_WIDTH]
    win_o[:, 0:KV_WIDTH] = _rope(kva[:, 4 * KV_WIDTH:5 * KV_WIDTH], c, sp, sm)
    win_o[:, KV_WIDTH:2 * KV_WIDTH] = kva[:, 5 * KV_WIDTH:6 * KV_WIDTH]
    kvb = _dot(h, wkvb[...])
    paged_o[:, 4 * KV_WIDTH:5 * KV_WIDTH] = _rope(kvb[:, 0:KV_WIDTH], c, sp, sm)
    paged_o[:, 5 * KV_WIDTH:6 * KV_WIDTH] = kvb[:, KV_WIDTH:2 * KV_WIDTH]
    gate_o[...] = _sigmoid(_dot(h, wgate[...]))
    za = _dot(h, wza[...])
    za_o[...] = za * _sigmoid(za)
    qb_o[...] = _rope(_dot(h, wqb[...]), c, sp, sm)
    zb = _dot(h, wzb[...])
    zb_o[...] = zb * _sigmoid(zb)


def _rope_tables(pos, n_rows):
    half = ROT_DIM // 2
    inv_freq = ROPE_THETA ** (-jnp.arange(half, dtype=F32) / half)
    ang = pos.astype(F32)[:, None] * inv_freq[None, :]
    cos, sin = jnp.cos(ang), jnp.sin(ang)
    d = np.arange(LANES) % HEAD_DIM
    idx = d % half
    c = jnp.where(d < ROT_DIM, cos[:, idx], 1.0)
    sp = jnp.where((d >= half) & (d < ROT_DIM), sin[:, idx], 0.0)
    sm = jnp.where(d < half, -sin[:, idx], 0.0)
    rep = n_rows // pos.shape[0]
    return tuple(jnp.tile(t, (rep, 1)) for t in (c, sp, sm))


def _proj(x2d, pos, g, w):
    n, d_model = x2d.shape
    t = pos.shape[0]
    tm = min(256, n)
    tab_rows = max(t, tm)
    tabs = _rope_tables(pos, tab_rows)
    n_tab = tab_rows // tm
    row = lambda width: pl.BlockSpec((tm, width), lambda i: (i, 0))
    tab = pl.BlockSpec((tm, LANES), lambda i: (i % n_tab, 0))
    widths = (NSA_WIDTH, NSA_WIDTH, PAGED_WIDTH, 2 * KV_WIDTH, LANES, NSA_WIDTH, MOBA_WIDTH, MOBA_WIDTH)
    return pl.pallas_call(
        _proj_kernel,
        grid=(n // tm,),
        in_specs=[row(d_model), _resident(g), tab, tab, tab] + [_resident(a) for a in w],
        out_specs=[row(wd) for wd in widths],
        out_shape=[jax.ShapeDtypeStruct((n, wd), F32) for wd in widths],
        compiler_params=_params(("parallel",)),
        name="proj",
    )(x2d, g, *tabs, *w)


def _compress_rows(chunk_row, pos_ref, w1_ref, w2_ref, m):
    hid2 = w2_ref.shape[0]
    xcat = jnp.concatenate([chunk_row(l).astype(BF16) for l in range(CMP_STRIDE)], axis=-1)
    ab = _dot(xcat, w1_ref[...])
    pb = _dot(pos_ref[...], w1_ref[...])
    bias = pb[0:1, 0:hid2] + pb[1:2, hid2:2 * hid2]
    nxt = pltpu.roll(ab[:, hid2:2 * hid2], m - 1, axis=0)
    pre = ab[:, 0:hid2] + nxt + bias
    hid = pre * _sigmoid(pre)
    return _dot(hid.astype(BF16), w2_ref[...])


def _compress_kernel(src_ref, pos_ref, w1_ref, w2_ref, o_ref, *, n_chunks, nc_pad, transposed):
    m = n_chunks
    out = _compress_rows(lambda l: src_ref[0, pl.ds(l, m, stride=CMP_STRIDE), :], pos_ref, w1_ref, w2_ref, m)
    if nc_pad > m:
        out = jnp.concatenate([out, jnp.zeros((nc_pad - m, LANES), F32)], axis=0)
    o_ref[0] = out.T if transposed else out


def _compress(full, slot, pos_rows, w1, w2, nc_pad, transposed):
    b, l, _ = full.shape
    n_chunks = l // CMP_STRIDE
    out_dims = (KV_WIDTH, nc_pad) if transposed else (nc_pad, KV_WIDTH)
    return pl.pallas_call(
        functools.partial(_compress_kernel, n_chunks=n_chunks, nc_pad=nc_pad, transposed=transposed),
        grid=(b,),
        in_specs=[pl.BlockSpec((1, l, KV_WIDTH), lambda i: (i, 0, slot)),
                  _resident(pos_rows), _resident(w1), _resident(w2)],
        out_specs=pl.BlockSpec((1,) + out_dims, lambda i: (i, 0, 0)),
        out_shape=jax.ShapeDtypeStruct((b,) + out_dims, F32),
        compiler_params=_params(("parallel",)),
        name="compress",
    )(full, pos_rows, w1, w2)


def _compress_weights(pos_emb, w1, w2):
    hidden = w1.shape[-1]
    eye = jnp.eye(KV_HEADS, dtype=F32)

    def half(w):
        return jnp.einsum('ldf,hg->lhdgf', w, eye).reshape(CMP_STRIDE * KV_WIDTH, KV_HEADS * hidden)

    w1bd = jnp.concatenate([half(w1[:CMP_STRIDE]), half(w1[CMP_STRIDE:])], axis=1).astype(BF16)
    w2bd = jnp.einsum('fd,hg->hfgd', w2, eye).reshape(KV_HEADS * hidden, KV_WIDTH).astype(BF16)

    def pos_row(p):
        return jnp.broadcast_to(p[:, None, :], (CMP_STRIDE, KV_HEADS, HEAD_DIM)).reshape(-1)

    pos_rows = jnp.zeros((8, CMP_STRIDE * KV_WIDTH), F32)
    pos_rows = pos_rows.at[0].set(pos_row(pos_emb[:CMP_STRIDE])).at[1].set(pos_row(pos_emb[CMP_STRIDE:]))
    return pos_rows.astype(BF16), w1bd, w2bd


def _kmean_kernel(src_ref, o_ref, *, n_blocks):
    x = src_ref[0].reshape(n_blocks, MOBA_BLOCK, KV_WIDTH)
    o_ref[0, 0:n_blocks, :] = jnp.sum(x, axis=1) * (1.0 / MOBA_BLOCK)
    o_ref[0, n_blocks:, :] = jnp.zeros((o_ref.shape[1] - n_blocks, KV_WIDTH), F32)


def _kmean(full, nb_pad):
    b, l, _ = full.shape
    return pl.pallas_call(
        functools.partial(_kmean_kernel, n_blocks=l // MOBA_BLOCK),
        grid=(b,),
        in_specs=[pl.BlockSpec((1, l, KV_WIDTH), lambda i: (i, 0, 4))],
        out_specs=pl.BlockSpec((1, nb_pad, KV_WIDTH), lambda i: (i, 0, 0)),
        out_shape=jax.ShapeDtypeStruct((b, nb_pad, KV_WIDTH), F32),
        compiler_params=_params(("parallel",)),
        name="kmean",
    )(full)


def _stack_heads(ref, kvh):
    return jnp.concatenate(
        [ref[0, :, (GROUP * kvh + g) * HEAD_DIM:(GROUP * kvh + g + 1) * HEAD_DIM] for g in range(GROUP)], axis=0)


def _masked_softmax(s, valid):
    s = jnp.where(valid, s, NEG_INF)
    e = jnp.where(valid, jnp.exp(s - jnp.max(s, axis=-1, keepdims=True)), 0.0)
    l = jnp.sum(e, axis=-1, keepdims=True)
    return e / jnp.where(l > 0.0, l, 1.0)


def _topk_mask(score, k, axis):
    n = score.shape[axis]
    idx = lax.broadcasted_iota(jnp.int32, score.shape, axis).astype(F32)
    sel = jnp.zeros(score.shape, F32)
    s = score
    for _ in range(k):
        m = jnp.max(s, axis=axis, keepdims=True)
        first = jnp.min(jnp.where(s == m, idx, float(n)), axis=axis, keepdims=True)
        pick = idx == first
        sel = jnp.where(pick & (m > 0.5 * NEG_INF), 1.0, sel)
        s = jnp.where(pick, -3e38, s)
    return sel


def _block_bias(chosen):
    return ((chosen - 1.0) * -NEG_INF).astype(BF16)


def _flash_t(q_aug, k_ref, v_ref, diag, t_col, expand_t):
    n_heads = len(q_aug)

    def scores(j):
        k0 = pl.multiple_of(j * KEY_TILE, KEY_TILE)
        kt = k_ref[0, pl.ds(k0, KEY_TILE), :].astype(BF16)
        lhs = jnp.concatenate([kt, expand_t(j)], axis=1)
        return [_dot(lhs, qa) for qa in q_aug]

    def values_t(j):
        k0 = pl.multiple_of(j * KEY_TILE, KEY_TILE)
        vt = v_ref[0, pl.ds(k0, KEY_TILE), :].T.astype(BF16)
        return [vt[h * HEAD_DIM:(h + 1) * HEAD_DIM] for h in range(n_heads)]

    visible = diag * KEY_TILE + lax.broadcasted_iota(jnp.int32, (KEY_TILE, 1), 0) <= t_col
    init = []
    for s, vt in zip(scores(diag), values_t(diag)):
        s = jnp.where(visible, s, NEG_INF)
        m0 = jnp.max(s, axis=0, keepdims=True)
        p = jnp.exp2(s - m0)
        init += [m0, jnp.sum(p, axis=0, keepdims=True), _dot(vt, p.astype(BF16))]

    def update(stats, s, vt):
        m_i, l_i, acc = stats
        m_new = jnp.maximum(m_i, jnp.max(s, axis=0, keepdims=True))
        alpha = jnp.exp2(m_i - m_new)
        p = jnp.exp2(s - m_new)
        return [m_new, alpha * l_i + jnp.sum(p, axis=0, keepdims=True), alpha * acc + _dot(vt, p.astype(BF16))]

    def body(i, carry):
        ja, jb = 2 * i, 2 * i + 1
        live = jb < diag
        jb = jnp.minimum(jb, diag)
        sa, va = scores(ja), values_t(ja)
        sb, vb = [jnp.where(live, s, NEG_INF) for s in scores(jb)], values_t(jb)
        out = []
        for h in range(n_heads):
            out += update(update(carry[3 * h:3 * h + 3], sa[h], va[h]), sb[h], vb[h])
        return tuple(out)

    res = lax.fori_loop(0, (diag + 1) // 2, body, tuple(init))
    return [res[3 * h + 2] / res[3 * h + 1] for h in range(n_heads)]


def _head_cols(x_t, kvh):
    x = jnp.concatenate([x_t[(GROUP * kvh + g) * HEAD_DIM:(GROUP * kvh + g + 1) * HEAD_DIM] for g in range(GROUP)],
                        axis=1)
    return jnp.concatenate([x if k == kvh else jnp.zeros_like(x) for k in range(KV_HEADS)], axis=0)


def _selection_map(n_cmp, nc_pad, ns_pad):
    c0 = np.arange(nc_pad)[:, None] * CMP_STRIDE
    j0 = np.arange(ns_pad)[None, :] * SEL_BLOCK
    ov = np.clip(np.minimum(c0 + CMP_LEN, j0 + SEL_BLOCK) - np.maximum(c0, j0), 0, None) / CMP_STRIDE
    return np.where(np.arange(nc_pad)[:, None] < n_cmp, ov, 0.0).astype(np.float32)


def _window_branch(qr, win_ref, w0, span, lo, w_valid):
    wk = win_ref[0, pl.ds(w0, span), lo:lo + HEAD_DIM].astype(BF16)
    wv = win_ref[0, pl.ds(w0, span), KV_WIDTH + lo:KV_WIDTH + lo + HEAD_DIM].astype(BF16)
    s = jnp.where(w_valid, _dot_nt(qr, wk), NEG_INF)
    e = jnp.exp(s - jnp.max(s, axis=-1, keepdims=True))
    return _dot(e.astype(BF16), wv) / jnp.sum(e, axis=-1, keepdims=True)


def _write_heads(o_ref, kvh, tq, head_out):
    for g in range(GROUP):
        h = GROUP * kvh + g
        o_ref[0, :, h * HEAD_DIM:(h + 1) * HEAD_DIM] = head_out(g, h, slice(g * tq, (g + 1) * tq))


def _nsa_kernel(qa_ref, qar_ref, gate_ref, kc_ref, vct_ref, sk_ref, sv_ref, win_ref, mapt_ref, o_ref, *, tq, span):
    i = pl.program_id(1)
    t0 = i * tq
    r = GROUP * tq
    nc_pad = kc_ref.shape[1]
    ns_pad = mapt_ref.shape[0]
    t_col = t0 + (lax.broadcasted_iota(jnp.int32, (1, r), 1) & (tq - 1))
    c_end = lax.broadcasted_iota(jnp.int32, (nc_pad, 1), 0) * CMP_STRIDE + CMP_LEN
    blk_t = lax.broadcasted_iota(jnp.int32, (ns_pad, 1), 0)
    blk_lane = lax.broadcasted_iota(jnp.int32, (1, ns_pad), 1)
    key_blk = lax.broadcasted_iota(jnp.int32, (KEY_TILE, 1), 0) >> SEL_SHIFT
    jt_t = (t0 + lax.broadcasted_iota(jnp.int32, (1, tq), 1)) >> SEL_SHIFT
    diag = t0 // KEY_TILE
    w0 = pl.multiple_of(i * tq, 8)
    wpos = t0 - WINDOW + lax.broadcasted_iota(jnp.int32, (span, 1), 0)
    w_valid = (wpos >= 0) & (wpos <= t_col) & (wpos >= t_col - WINDOW)
    c_valid = c_end <= t_col + 1

    qa_t = qa_ref[0].T
    qar_t = qar_ref[0].T
    gates_t = gate_ref[0].T
    kc = kc_ref[0].astype(BF16)
    wk = win_ref[0, pl.ds(w0, span), 0:KV_WIDTH].astype(BF16)
    wv_t = win_ref[0, pl.ds(w0, span), KV_WIDTH:2 * KV_WIDTH].T.astype(BF16)

    def expand_t(j):
        return (key_blk + j * (KEY_TILE // SEL_BLOCK) == blk_lane).astype(BF16)

    q_aug, o_cmp_t, o_win_t = [], [], []
    for kvh in range(KV_HEADS):
        rows = slice(kvh * HEAD_DIM, (kvh + 1) * HEAD_DIM)
        q_t = (_head_cols(qa_t, kvh) * SCALE).astype(BF16)
        s = jnp.where(c_valid, _dot(kc, q_t), NEG_INF)
        e = jnp.where(c_valid, jnp.exp(s - jnp.max(s, axis=0, keepdims=True)), 0.0)
        l = jnp.sum(e, axis=0, keepdims=True)
        p_cmp = e / jnp.where(l > 0.0, l, 1.0)
        o_cmp_t.append(_dot(vct_ref[0, rows, :].astype(BF16), p_cmp.astype(BF16)))

        p_sum = p_cmp[:, 0:tq]
        for g in range(1, GROUP):
            p_sum = p_sum + p_cmp[:, g * tq:(g + 1) * tq]
        imp_t = jnp.dot(mapt_ref[...], p_sum, preferred_element_type=F32, precision=lax.Precision.HIGHEST)
        score_t = jnp.where((blk_t == jt_t) | (blk_t == 0), FORCE_SCORE, jnp.where(blk_t < jt_t, imp_t, NEG_INF))
        bias_t = _block_bias(_topk_mask(score_t, SEL_TOPK, 0))
        qr_t = (_head_cols(qar_t, kvh) * (SCALE * LOG2E)).astype(BF16)
        q_aug.append(jnp.concatenate([qr_t, jnp.concatenate([bias_t] * GROUP, axis=1)], axis=0))

        s = jnp.where(w_valid, _dot(wk, qr_t), NEG_INF)
        e = jnp.exp2(s - jnp.max(s, axis=0, keepdims=True))
        o_win_t.append(_dot(wv_t[rows], e.astype(BF16)) / jnp.sum(e, axis=0, keepdims=True))

    o_sel_t = _flash_t(q_aug, sk_ref, sv_ref, diag, t_col, expand_t)

    heads = []
    for kvh in range(KV_HEADS):
        for g in range(GROUP):
            h = GROUP * kvh + g
            cols = slice(g * tq, (g + 1) * tq)
            heads.append(gates_t[3 * h:3 * h + 1] * o_cmp_t[kvh][:, cols]
                         + gates_t[3 * h + 1:3 * h + 2] * o_sel_t[kvh][:, cols]
                         + gates_t[3 * h + 2:3 * h + 3] * o_win_t[kvh][:, cols])
    o_ref[0] = jnp.concatenate(heads, axis=0).T


def _nsa(qa, qar, gate, kc, vc_t, full, win, tq):
    b, t, _ = qa.shape
    l = full.shape[1]
    nc_pad = kc.shape[1]
    ns_pad = _round_up(l // SEL_BLOCK, LANES)
    span = _round_up(WINDOW + tq, LANES)
    map_t = jnp.asarray(_selection_map(l // CMP_STRIDE - 1, nc_pad, ns_pad).T)
    qspec = pl.BlockSpec((1, tq, NSA_WIDTH), lambda bi, i: (bi, i, 0))
    seq = lambda a: pl.BlockSpec((1,) + a.shape[1:], lambda bi, i: (bi, 0, 0))
    return pl.pallas_call(
        functools.partial(_nsa_kernel, tq=tq, span=span),
        grid=(b, t // tq),
        in_specs=[
            qspec, qspec,
            pl.BlockSpec((1, tq, LANES), lambda bi, i: (bi, i, 0)),
            seq(kc), seq(vc_t),
            pl.BlockSpec((1, l, KV_WIDTH), lambda bi, i: (bi, 0, 2)),
            pl.BlockSpec((1, l, KV_WIDTH), lambda bi, i: (bi, 0, 3)),
            seq(win),
            _resident(map_t),
        ],
        out_specs=qspec,
        out_shape=jax.ShapeDtypeStruct((b, t, NSA_WIDTH), F32),
        compiler_params=_params(("parallel", "arbitrary")),
        name="nsa",
    )(qa, qar, gate, kc, vc_t, full, full, win, map_t)


def _moba_kernel(qb_ref, km_ref, mk_ref, mv_ref, o_ref, *, tq):
    i = pl.program_id(1)
    t0 = i * tq
    r = GROUP * tq
    nb_pad = km_ref.shape[1]
    t_col = t0 + (lax.broadcasted_iota(jnp.int32, (1, r), 1) & (tq - 1))
    bt_t = t_col >> MOBA_SHIFT
    nblk_t = lax.broadcasted_iota(jnp.int32, (nb_pad, 1), 0)
    nblk_lane = lax.broadcasted_iota(jnp.int32, (1, nb_pad), 1)
    diag = t0 // KEY_TILE
    qb_t = qb_ref[0].T
    km = km_ref[0]

    def expand_t(j):
        return jnp.broadcast_to(nblk_lane == j, (KEY_TILE, nb_pad)).astype(BF16)

    q_aug = []
    for kvh in range(KV_HEADS):
        q_t = _head_cols(qb_t, kvh)
        gate_t = jnp.dot(km, q_t, preferred_element_type=F32, precision=lax.Precision.HIGHEST)
        gate_t = jnp.where(nblk_t < bt_t, gate_t, NEG_INF)
        chosen_t = jnp.where(nblk_t == bt_t, 1.0, _topk_mask(gate_t, MOBA_TOPK, 0))
        q_aug.append(jnp.concatenate([(q_t * (SCALE * LOG2E)).astype(BF16), _block_bias(chosen_t)], axis=0))

    o_t = _flash_t(q_aug, mk_ref, mv_ref, diag, t_col, expand_t)
    heads = [o_t[kvh][:, g * tq:(g + 1) * tq] for kvh in range(KV_HEADS) for g in range(GROUP)]
    o_ref[0] = jnp.concatenate(heads, axis=0).T


def _moba(qb, kmean, full, tq):
    b, t, _ = qb.shape
    l = full.shape[1]
    qspec = pl.BlockSpec((1, tq, MOBA_WIDTH), lambda bi, i: (bi, i, 0))
    return pl.pallas_call(
        functools.partial(_moba_kernel, tq=tq),
        grid=(b, t // tq),
        in_specs=[
            qspec,
            pl.BlockSpec((1,) + kmean.shape[1:], lambda bi, i: (bi, 0, 0)),
            pl.BlockSpec((1, l, KV_WIDTH), lambda bi, i: (bi, 0, 4)),
            pl.BlockSpec((1, l, KV_WIDTH), lambda bi, i: (bi, 0, 5)),
        ],
        out_specs=qspec,
        out_shape=jax.ShapeDtypeStruct((b, t, MOBA_WIDTH), F32),
        compiler_params=_params(("parallel", "arbitrary")),
        name="moba",
    )(qb, kmean, full, full)


def _softmax_pv_t(s, vt):
    e = jnp.exp(s - jnp.max(s, axis=-1, keepdims=True))
    return _dot_nt(e.astype(BF16), vt) / jnp.sum(e, axis=-1, keepdims=True)


def _segment_bias(bias_rows, e_ref, block, n_keys):
    rows, seg = e_ref.shape
    parts = []
    for s in range(-(-n_keys // seg)):
        n = min(seg, n_keys - s * seg)
        b0 = s * (seg // block)
        parts.append(_dot(bias_rows[:, b0:b0 + rows], e_ref[:, 0:n]))
    return jnp.concatenate(parts, axis=1)


def _block_means_t(kt, n_keys, n_out):
    lane = lax.broadcasted_iota(jnp.int32, (1, n_out), 1)
    out = jnp.zeros((kt.shape[0], n_out), F32)
    for n in range(n_keys // MOBA_BLOCK):
        blk = kt[:, n * MOBA_BLOCK:(n + 1) * MOBA_BLOCK].astype(F32)
        out = jnp.where(lane == n, jnp.sum(blk, axis=1, keepdims=True) * (1.0 / MOBA_BLOCK), out)
    return out


def _sample_kernel(pt_ref, *refs, past_len, ts, span):
    pages = refs[:PAGES_PER_STEP]
    (new_ref, qa_ref, qar_ref, gate_ref, qb_ref, win_ref, map_ref, esel_ref, emoba_ref,
     posk_ref, w1k_ref, w2k_ref, posv_ref, w1v_ref, w2v_ref, oa_ref, ob_ref, tok_scr, kv_scr) = refs[PAGES_PER_STEP:]
    del pt_ref
    step = pl.program_id(1)
    n_steps = pl.num_programs(1)
    page = pages[0].shape[2]
    cmp_rows = 2 * KV_WIDTH

    for j in range(PAGES_PER_STEP):
        tok0 = pl.multiple_of((step * PAGES_PER_STEP + j) * page, page)
        for slot in range(2):
            tok_scr[slot, pl.ds(tok0, page), :] = pages[j][0, slot * KV_WIDTH:(slot + 1) * KV_WIDTH, :].T
    for k in range(past_len // (PAGES_PER_STEP * page)):
        @pl.when(step == k)
        def _(k=k):
            for j in range(PAGES_PER_STEP):
                c0 = (k * PAGES_PER_STEP + j) * page
                kv_scr[:, c0:c0 + page] = pages[j][0, cmp_rows:, :].astype(BF16)

    @pl.when(step == n_steps - 1)
    def _():
        lc = kv_scr.shape[1]
        r = GROUP * ts
        new_pad = jnp.concatenate([new_ref[0], jnp.zeros((page - ts, PAGED_WIDTH), F32)], axis=0)
        kv_scr[:, past_len:past_len + page] = new_pad.T[cmp_rows:, :].astype(BF16)

        m = past_len // CMP_STRIDE
        kc_all = _compress_rows(lambda l: tok_scr[0, pl.ds(l, m, stride=CMP_STRIDE), :], posk_ref, w1k_ref, w2k_ref, m)
        vc_all = _compress_rows(lambda l: tok_scr[1, pl.ds(l, m, stride=CMP_STRIDE), :], posv_ref, w1v_ref, w2v_ref, m)

        ns_pad = map_ref.shape[1]
        nb_pad = LANES
        row = lax.broadcasted_iota(jnp.int32, (r, 1), 0)
        t_row = past_len + (row & (ts - 1))
        t_tok = past_len + lax.broadcasted_iota(jnp.int32, (ts, 1), 0)
        causal = lax.broadcasted_iota(jnp.int32, (1, lc), 1) <= t_row
        c_end = lax.broadcasted_iota(jnp.int32, (1, m), 1) * CMP_STRIDE + CMP_LEN
        blk = lax.broadcasted_iota(jnp.int32, (1, ns_pad), 1)
        nblk = lax.broadcasted_iota(jnp.int32, (1, nb_pad), 1)
        jt = t_tok >> SEL_SHIFT
        bt = t_row >> MOBA_SHIFT
        wpos = past_len - WINDOW + lax.broadcasted_iota(jnp.int32, (1, span), 1)
        w_valid = (wpos >= 0) & (wpos <= t_row) & (wpos >= t_row - WINDOW)
        gates = gate_ref[0]

        for kvh in range(KV_HEADS):
            lo = kvh * HEAD_DIM
            q = (_stack_heads(qa_ref, kvh) * SCALE).astype(BF16)
            qr = (_stack_heads(qar_ref, kvh) * SCALE).astype(BF16)

            p_cmp = _masked_softmax(_dot_nt(q, kc_all[:, lo:lo + HEAD_DIM].astype(BF16)), c_end <= t_row + 1)
            o_cmp = _dot(p_cmp.astype(BF16), vc_all[:, lo:lo + HEAD_DIM].astype(BF16))
            p_sum = p_cmp[0:ts]
            for g in range(1, GROUP):
                p_sum = p_sum + p_cmp[g * ts:(g + 1) * ts]
            imp = jnp.dot(p_sum, map_ref[...], preferred_element_type=F32, precision=lax.Precision.HIGHEST)
            score = jnp.where((blk == jt) | (blk == 0), FORCE_SCORE, jnp.where(blk < jt, imp, NEG_INF))
            sel = _topk_mask(score, SEL_TOPK, 1)
            bias_rows = _block_bias(jnp.concatenate([sel] * GROUP, axis=0))

            s = _dot(qr, kv_scr[lo:lo + HEAD_DIM, :]) + _segment_bias(bias_rows, esel_ref, SEL_BLOCK, lc)
            o_sel = _softmax_pv_t(jnp.where(causal, s, NEG_INF), kv_scr[KV_WIDTH + lo:KV_WIDTH + lo + HEAD_DIM, :])
            o_win = _window_branch(qr, win_ref, 0, span, lo, w_valid)
            _write_heads(oa_ref, kvh, ts, lambda g, h, rows: (
                gates[:, 3 * h:3 * h + 1] * o_cmp[rows] + gates[:, 3 * h + 1:3 * h + 2] * o_sel[rows]
                + gates[:, 3 * h + 2:3 * h + 3] * o_win[rows]))

            qf = _stack_heads(qb_ref, kvh)
            kt = kv_scr[2 * KV_WIDTH + lo:2 * KV_WIDTH + lo + HEAD_DIM, :]
            gate_s = jnp.dot(qf, _block_means_t(kt, past_len, nb_pad), preferred_element_type=F32,
                             precision=lax.Precision.HIGHEST)
            gate_s = jnp.where(nblk < bt, gate_s, NEG_INF)
            chosen = jnp.where(nblk == bt, 1.0, _topk_mask(gate_s, MOBA_TOPK, 1))
            s = _dot((qf * SCALE).astype(BF16), kt) + _segment_bias(_block_bias(chosen), emoba_ref, MOBA_BLOCK, lc)
            o = _softmax_pv_t(jnp.where(causal, s, NEG_INF),
                              kv_scr[3 * KV_WIDTH + lo:3 * KV_WIDTH + lo + HEAD_DIM, :])
            _write_heads(ob_ref, kvh, ts, lambda g, h, rows: o[rows])


def _block_expansion(n_rows, block, n_keys):
    return jnp.asarray(np.arange(n_rows)[:, None] == (np.arange(n_keys)[None, :] // block), dtype=BF16)


def _sample_attn(pool_t, page_table, page_base, new_rows, qa, qar, gate, qb, win, cmp_k, cmp_v):
    bs, n_pages = page_table.shape
    page = pool_t.shape[2]
    ts = new_rows.shape[1]
    past_len = n_pages * page
    n_steps = n_pages // PAGES_PER_STEP
    assert page == LANES and n_pages % PAGES_PER_STEP == 0 and ts < CMP_STRIDE and ts & (ts - 1) == 0
    assert past_len % MOBA_BLOCK == 0 and (past_len // CMP_STRIDE) % LANES == 0
    lc = past_len + page
    ns_pad = _round_up(lc // SEL_BLOCK, LANES)
    assert lc // MOBA_BLOCK + 1 <= LANES and past_len % BIAS_SEG == 0
    span = win.shape[1]
    sel_map = jnp.asarray(_selection_map(past_len // CMP_STRIDE - 1, past_len // CMP_STRIDE, ns_pad))
    e_sel = _block_expansion(BIAS_SEG // SEL_BLOCK, SEL_BLOCK, BIAS_SEG)
    e_moba = _block_expansion(2 * (BIAS_SEG // MOBA_BLOCK), MOBA_BLOCK, BIAS_SEG)

    def page_spec(j):
        return pl.BlockSpec((1, PAGED_WIDTH, page),
                            lambda b, s, pt: (page_base + pt[b, s * PAGES_PER_STEP + j], 0, 0))

    per_seq = lambda a: pl.BlockSpec((1,) + a.shape[1:], lambda b, s, pt: (b, 0, 0))
    consts = (sel_map, e_sel, e_moba) + tuple(cmp_k) + tuple(cmp_v)
    out_spec = pl.BlockSpec((1, ts, NSA_WIDTH), lambda b, s, pt: (b, 0, 0))
    grid_spec = pltpu.PrefetchScalarGridSpec(
        num_scalar_prefetch=1,
        grid=(bs, n_steps),
        in_specs=[page_spec(j) for j in range(PAGES_PER_STEP)]
        + [per_seq(a) for a in (new_rows, qa, qar, gate, qb, win)] + [_resident(a) for a in consts],
        out_specs=[out_spec, out_spec],
        scratch_shapes=[pltpu.VMEM((2, past_len, KV_WIDTH), F32),
                        pltpu.VMEM((PAGED_WIDTH - 2 * KV_WIDTH, lc), BF16)],
    )
    return pl.pallas_call(
        functools.partial(_sample_kernel, past_len=past_len, ts=ts, span=span),
        grid_spec=grid_spec,
        out_shape=[jax.ShapeDtypeStruct((bs, ts, NSA_WIDTH), F32)] * 2,
        compiler_params=_params(("parallel", "arbitrary")),
        name="sample_attn",
    )(page_table, *([pool_t] * PAGES_PER_STEP), new_rows, qa, qar, gate, qb, win, *consts)


def _out_kernel(x_ref, oa_ref, ob_ref, za_ref, zb_ref, wout_ref, gple_ref, wg_ref, ple_ref, wp_ref, gfin_ref,
                o_ref, *, final):
    mixed = jnp.concatenate([oa_ref[...] * za_ref[...], ob_ref[...] * zb_ref[...]], axis=-1).astype(BF16)
    x1 = x_ref[...] + _dot(mixed, wout_ref[...])
    gate = _sigmoid(_dot(_rms(x1, gple_ref[...]).astype(BF16), wg_ref[...]))
    x2 = x1 + gate * _dot(ple_ref[...].astype(BF16), wp_ref[...])
    o_ref[...] = _rms(x2, gfin_ref[...]) if final else x2


def _out(x2d, oa, ob, za, zb, w_out, g_ple, w_gate, ple, w_proj, g_final, final):
    n, d_model = x2d.shape
    tm = min(256, n)
    row = lambda a: pl.BlockSpec((tm, a.shape[1]), lambda i: (i, 0))
    args = (x2d, oa, ob, za, zb, w_out, g_ple, w_gate, ple, w_proj, g_final)
    specs = [row(x2d), row(oa), row(ob), row(za), row(zb), _resident(w_out), _resident(g_ple), _resident(w_gate),
             row(ple), _resident(w_proj), _resident(g_final)]
    return pl.pallas_call(
        functools.partial(_out_kernel, final=final),
        grid=(n // tm,),
        in_specs=specs,
        out_specs=row(x2d),
        out_shape=jax.ShapeDtypeStruct((n, d_model), F32),
        compiler_params=_params(("parallel",)),
        name="out",
    )(*args)


def _mixer_layer(x, ple, pos, past, win_prev, weights, g_final, final):
    (g_mix, w_split, w_out, cmp_k, cmp_v, g_ple, w_gate, w_proj) = weights
    b, t, d_model = x.shape
    n = b * t
    qa, qar, paged, win_new, gate, za, qb, zb = _proj(x.reshape(n, d_model), pos, g_mix, w_split)
    paged3 = paged.reshape(b, t, PAGED_WIDTH)
    win_new3 = win_new.reshape(b, t, 2 * KV_WIDTH)
    tq = min(KEY_TILE, t)
    assert KEY_TILE % tq == 0 and tq & (tq - 1) == 0
    span = _round_up(WINDOW + tq, LANES)
    win_rows = max(WINDOW + t, t - tq + span)
    win = jnp.concatenate([win_prev, win_new3, jnp.zeros((b, win_rows - WINDOW - t, 2 * KV_WIDTH), F32)], axis=1)
    qa3, qar3, gate3, qb3 = (a.reshape(b, t, -1) for a in (qa, qar, gate, qb))
    if past is None:
        assert t % KEY_TILE == 0
        nc_pad = _round_up(t // CMP_STRIDE, LANES)
        kc = _compress(paged3, 0, *cmp_k, nc_pad, False)
        vc_t = _compress(paged3, 1, *cmp_v, nc_pad, True)
        o_a = _nsa(qa3, qar3, gate3, kc, vc_t, paged3, win, tq)
        o_b = _moba(qb3, _kmean(paged3, LANES), paged3, tq)
    else:
        o_a, o_b = _sample_attn(*past, paged3, qa3, qar3, gate3, qb3, win, cmp_k, cmp_v)
    x_next = _out(x.reshape(n, d_model), o_a.reshape(n, -1), o_b.reshape(n, -1), za, zb, w_out, g_ple, w_gate,
                  ple.reshape(n, -1), w_proj, g_final, final)
    return x_next.reshape(b, t, d_model), paged3, win_new3


def _split_w_in(w_in):
    sizes = (NSA_WIDTH, 6 * KV_WIDTH, 3 * NSA_HEADS, NSA_WIDTH, MOBA_WIDTH, 2 * KV_WIDTH, MOBA_WIDTH)
    offs = np.concatenate([[0], np.cumsum(sizes)])
    wqa, wkva, wgate, wza, wqb, wkvb, wzb = [w_in[:, offs[k]:offs[k + 1]].astype(BF16) for k in range(len(sizes))]
    wgate = jnp.pad(wgate, ((0, 0), (0, LANES - wgate.shape[1])))
    return wqa, wkva, wgate, wza, wqb, wkvb, wzb


def kernel(x_prompt, x_sample, cache_paged_kv, cache_win_kv, page_table, p_prompt, p_sample, g_mix, w_in, w_out,
           cmp_pos_k, cmp_w1_k, cmp_w2_k, cmp_pos_v, cmp_w1_v, cmp_w2_v, g_ple, w_ple_gate, w_ple_proj, g_final):
    depth = w_in.shape[0]
    bp, tp, _ = x_prompt.shape
    bs, ts, _ = x_sample.shape
    n_pool, page = cache_paged_kv.shape[1], cache_paged_kv.shape[2]
    past_len = page_table.shape[1] * page
    wbuf = cache_win_kv.shape[2]
    pos_p = jnp.arange(tp, dtype=jnp.int32)
    pos_s = past_len + jnp.arange(ts, dtype=jnp.int32)
    pool_t = cache_paged_kv.transpose(0, 1, 3, 4, 5, 2).reshape(depth * n_pool, PAGED_WIDTH, page)
    g_fin = g_final.reshape(1, -1)
    xp, xs = x_prompt, x_sample
    new_pp, new_pw, new_sp, new_sw = [], [], [], []
    for i in range(depth):
        final = i == depth - 1
        weights = (g_mix[i].reshape(1, -1), _split_w_in(w_in[i]), w_out[i].astype(BF16),
                   _compress_weights(cmp_pos_k[i], cmp_w1_k[i], cmp_w2_k[i]),
                   _compress_weights(cmp_pos_v[i], cmp_w1_v[i], cmp_w2_v[i]),
                   g_ple[i].reshape(1, -1), w_ple_gate[i].astype(BF16), w_ple_proj[i].astype(BF16))
        win0 = jnp.zeros((bp, WINDOW, 2 * KV_WIDTH), F32)
        xp, pk, pw = _mixer_layer(xp, p_prompt[i], pos_p, None, win0, weights, g_fin, final)
        win_prev = jnp.concatenate([jnp.zeros((bs, WINDOW - wbuf, 2 * KV_WIDTH), F32),
                                    cache_win_kv[i].reshape(bs, wbuf, 2 * KV_WIDTH)], axis=1)
        xs, sk, sw = _mixer_layer(xs, p_sample[i], pos_s, (pool_t, page_table, i * n_pool), win_prev, weights,
                                  g_fin, final)
        new_pp.append(pk.reshape(bp, tp, N_PAGED_SLOTS, KV_HEADS, HEAD_DIM))
        new_pw.append(pw[:, tp - min(WINDOW, tp):].reshape(bp, min(WINDOW, tp), 2, KV_HEADS, HEAD_DIM))
        new_sp.append(sk.reshape(bs, ts, N_PAGED_SLOTS, KV_HEADS, HEAD_DIM))
        new_sw.append(sw.reshape(bs, ts, 2, KV_HEADS, HEAD_DIM))
    return (xp, xs, jnp.stack(new_pp), jnp.stack(new_pw), jnp.stack(new_sp), jnp.stack(new_sw))
```

```python
import functools

import numpy as np
import jax
import jax.numpy as jnp
from jax import lax
from jax.experimental import pallas as pl
from jax.experimental.pallas import tpu as pltpu

HEAD_DIM = 64
NSA_HEADS = 8
MOBA_HEADS = 8
KV_HEADS = 2
GROUP = NSA_HEADS // KV_HEADS
ROT_DIM = HEAD_DIM // 4
ROPE_THETA = 500000.0
CMP_LEN = 32
CMP_STRIDE = 16
SEL_BLOCK = 64
SEL_TOPK = 8
WINDOW = 512
MOBA_BLOCK = 256
MOBA_TOPK = 3
N_PAGED_SLOTS = 6
RMS_EPS = 1e-6
NEG_INF = -1e30
FORCE_SCORE = 1e4
SCALE = HEAD_DIM ** -0.5
LOG2E = 1.4426950408889634
SEL_SHIFT = SEL_BLOCK.bit_length() - 1
MOBA_SHIFT = MOBA_BLOCK.bit_length() - 1

LANES = 128
KV_WIDTH = KV_HEADS * HEAD_DIM
NSA_WIDTH = NSA_HEADS * HEAD_DIM
MOBA_WIDTH = MOBA_HEADS * HEAD_DIM
PAGED_WIDTH = N_PAGED_SLOTS * KV_WIDTH
KEY_TILE = 256
PAGES_PER_STEP = 8
BIAS_SEG = 2048
VMEM_LIMIT = 56 * 1024 * 1024

F32 = jnp.float32
BF16 = jnp.bfloat16
NT_DIMS = (((1,), (1,)), ((), ()))


def _round_up(n, m):
    return -(-n // m) * m


def _params(sem):
    return pltpu.CompilerParams(dimension_semantics=sem, vmem_limit_bytes=VMEM_LIMIT)


def _rms(x, g):
    return x * lax.rsqrt(jnp.mean(x * x, axis=-1, keepdims=True) + RMS_EPS) * g


def _sigmoid(x):
    return 1.0 / (1.0 + jnp.exp(-x))


def _dot(a, b):
    return jnp.dot(a, b, preferred_element_type=F32)


def _dot_nt(a, b):
    return lax.dot_general(a, b, NT_DIMS, preferred_element_type=F32)


def _dot_nt_exact(a, b):
    return lax.dot_general(a, b, NT_DIMS, preferred_element_type=F32, precision=lax.Precision.HIGHEST)


def _resident(a):
    return pl.BlockSpec(a.shape, lambda *_: (0,) * a.ndim, pipeline_mode=pl.Buffered(1))


def _rope128(v, c, sp, sm):
    half = ROT_DIM // 2
    return v * c + pltpu.roll(v, half, axis=1) * sp + pltpu.roll(v, LANES - half, axis=1) * sm


def _rope(v, c, sp, sm):
    n = v.shape[-1] // LANES
    return jnp.concatenate([_rope128(v[:, i * LANES:(i + 1) * LANES], c, sp, sm) for i in range(n)], axis=-1)


def _proj_kernel(x_ref, g_ref, c_ref, sp_ref, sm_ref, wqa, wkva, wgate, wza, wqb, wkvb, wzb,
                 qa_o, qar_o, paged_o, win_o, gate_o, za_o, qb_o, zb_o, *transposed_o):
    h = _rms(x_ref[...], g_ref[...]).astype(BF16)
    c, sp, sm = c_ref[...], sp_ref[...], sm_ref[...]
    qa = _dot(h, wqa[...])
    qa_o[...] = qa
    qar_o[...] = _rope(qa, c, sp, sm)
    kva = _dot(h, wkva[...])
    paged_o[:, 0:2 * KV_WIDTH] = kva[:, 0:2 * KV_WIDTH]
    paged_o[:, 2 * KV_WIDTH:3 * KV_WIDTH] = _rope(kva[:, 2 * KV_WIDTH:3 * KV_WIDTH], c, sp, sm)
    paged_o[:, 3 * KV_WIDTH:4 * KV_WIDTH] = kva[:, 3 * KV_WIDTH:4 * KV_WIDTH]
    win_o[:, 0:KV_WIDTH] = _rope(kva[:, 4 * KV_WIDTH:5 * KV_WIDTH], c, sp, sm)
    win_o[:, KV_WIDTH:2 * KV_WIDTH] = kva[:, 5 * KV_WIDTH:6 * KV_WIDTH]
    kvb = _dot(h, wkvb[...])
    paged_o[:, 4 * KV_WIDTH:5 * KV_WIDTH] = _rope(kvb[:, 0:KV_WIDTH], c, sp, sm)
    paged_o[:, 5 * KV_WIDTH:6 * KV_WIDTH] = kvb[:, KV_WIDTH:2 * KV_WIDTH]
    gate_o[...] = _sigmoid(_dot(h, wgate[...]))
    za = _dot(h, wza[...])
    za_o[...] = za * _sigmoid(za)
    qb_o[...] = _rope(_dot(h, wqb[...]), c, sp, sm)
    zb = _dot(h, wzb[...])
    zb_o[...] = zb * _sigmoid(zb)
    if transposed_o:
        paged_t_o, win_t_o = transposed_o
        paged_t_o[0] = paged_o[...].T
        win_t_o[0] = win_o[...].T


def _rope_tables(pos, n_rows):
    half = ROT_DIM // 2
    inv_freq = ROPE_THETA ** (-jnp.arange(half, dtype=F32) / half)
    ang = pos.astype(F32)[:, None] * inv_freq[None, :]
    cos, sin = jnp.cos(ang), jnp.sin(ang)
    d = np.arange(LANES) % HEAD_DIM
    idx = d % half
    c = jnp.where(d < ROT_DIM, cos[:, idx], 1.0)
    sp = jnp.where((d >= half) & (d < ROT_DIM), sin[:, idx], 0.0)
    sm = jnp.where(d < half, -sin[:, idx], 0.0)
    rep = n_rows // pos.shape[0]
    return tuple(jnp.tile(t, (rep, 1)) for t in (c, sp, sm))


def _proj(x2d, pos, g, w):
    n, d_model = x2d.shape
    t = pos.shape[0]
    tm = min(256, n)
    tab_rows = max(t, tm)
    tabs = _rope_tables(pos, tab_rows)
    n_tab = tab_rows // tm
    row = lambda width: pl.BlockSpec((tm, width), lambda i: (i, 0))
    tab = pl.BlockSpec((tm, LANES), lambda i: (i % n_tab, 0))
    widths = (NSA_WIDTH, NSA_WIDTH, PAGED_WIDTH, 2 * KV_WIDTH, LANES, NSA_WIDTH, MOBA_WIDTH, MOBA_WIDTH)
    out_specs = [row(wd) for wd in widths]
    out_shape = [jax.ShapeDtypeStruct((n, wd), F32) for wd in widths]
    if t % tm == 0:
        tiles = t // tm
        for wd in (PAGED_WIDTH, 2 * KV_WIDTH):
            out_specs.append(pl.BlockSpec((1, wd, tm), lambda i: (i // tiles, 0, i % tiles)))
            out_shape.append(jax.ShapeDtypeStruct((n // t, wd, t), F32))
    return pl.pallas_call(
        _proj_kernel,
        grid=(n // tm,),
        in_specs=[row(d_model), _resident(g), tab, tab, tab] + [_resident(a) for a in w],
        out_specs=out_specs,
        out_shape=out_shape,
        compiler_params=_params(("parallel",)),
        name="proj",
    )(x2d, g, *tabs, *w)


def _compress_rows(chunk_row, pos_ref, w1_ref, w2_ref, m):
    hid2 = w2_ref.shape[0]
    xcat = jnp.concatenate([chunk_row(l).astype(BF16) for l in range(CMP_STRIDE)], axis=-1)
    ab = _dot(jnp.concatenate([xcat, pos_ref[...]], axis=0), w1_ref[...])
    bias = ab[m:m + 1, 0:hid2] + ab[m + 1:m + 2, hid2:2 * hid2]
    nxt = pltpu.roll(ab[0:m, hid2:2 * hid2], m - 1, axis=0)
    pre = ab[0:m, 0:hid2] + nxt + bias
    hid = pre * _sigmoid(pre)
    return _dot(hid.astype(BF16), w2_ref[...])


def _compress_kernel(src_ref, pos_ref, w1_ref, w2_ref, o_ref, *, n_chunks, nc_pad, transposed):
    m = n_chunks
    out = _compress_rows(lambda l: src_ref[0, pl.ds(l, m, stride=CMP_STRIDE), :], pos_ref, w1_ref, w2_ref, m)
    if nc_pad > m:
        out = jnp.concatenate([out, jnp.zeros((nc_pad - m, LANES), F32)], axis=0)
    o_ref[0] = out.T if transposed else out


def _compress(full, slot, pos_rows, w1, w2, nc_pad, transposed):
    b, l, _ = full.shape
    n_chunks = l // CMP_STRIDE
    out_dims = (KV_WIDTH, nc_pad) if transposed else (nc_pad, KV_WIDTH)
    return pl.pallas_call(
        functools.partial(_compress_kernel, n_chunks=n_chunks, nc_pad=nc_pad, transposed=transposed),
        grid=(b,),
        in_specs=[pl.BlockSpec((1, l, KV_WIDTH), lambda i: (i, 0, slot)),
                  _resident(pos_rows), _resident(w1), _resident(w2)],
        out_specs=pl.BlockSpec((1,) + out_dims, lambda i: (i, 0, 0)),
        out_shape=jax.ShapeDtypeStruct((b,) + out_dims, F32),
        compiler_params=_params(("parallel",)),
        name="compress",
    )(full, pos_rows, w1, w2)


def _compress_weights(pos_emb, w1, w2):
    hidden = w1.shape[-1]
    eye = jnp.eye(KV_HEADS, dtype=F32)

    def half(w):
        return jnp.einsum('ldf,hg->lhdgf', w, eye).reshape(CMP_STRIDE * KV_WIDTH, KV_HEADS * hidden)

    w1bd = jnp.concatenate([half(w1[:CMP_STRIDE]), half(w1[CMP_STRIDE:])], axis=1).astype(BF16)
    w2bd = jnp.einsum('fd,hg->hfgd', w2, eye).reshape(KV_HEADS * hidden, KV_WIDTH).astype(BF16)

    def pos_row(p):
        return jnp.broadcast_to(p[:, None, :], (CMP_STRIDE, KV_HEADS, HEAD_DIM)).reshape(-1)

    pos_rows = jnp.zeros((16, CMP_STRIDE * KV_WIDTH), F32)
    pos_rows = pos_rows.at[0].set(pos_row(pos_emb[:CMP_STRIDE])).at[1].set(pos_row(pos_emb[CMP_STRIDE:]))
    return pos_rows.astype(BF16), w1bd, w2bd


def _kmean_kernel(src_ref, o_ref, *, n_blocks):
    x = src_ref[0].reshape(n_blocks, MOBA_BLOCK, KV_WIDTH)
    o_ref[0, 0:n_blocks, :] = jnp.sum(x, axis=1) * (1.0 / MOBA_BLOCK)
    o_ref[0, n_blocks:, :] = jnp.zeros((o_ref.shape[1] - n_blocks, KV_WIDTH), F32)


def _kmean(full, nb_pad):
    b, l, _ = full.shape
    return pl.pallas_call(
        functools.partial(_kmean_kernel, n_blocks=l // MOBA_BLOCK),
        grid=(b,),
        in_specs=[pl.BlockSpec((1, l, KV_WIDTH), lambda i: (i, 0, 4))],
        out_specs=pl.BlockSpec((1, nb_pad, KV_WIDTH), lambda i: (i, 0, 0)),
        out_shape=jax.ShapeDtypeStruct((b, nb_pad, KV_WIDTH), F32),
        compiler_params=_params(("parallel",)),
        name="kmean",
    )(full)


def _stack_heads(ref, kvh):
    return jnp.concatenate(
        [ref[0, :, (GROUP * kvh + g) * HEAD_DIM:(GROUP * kvh + g + 1) * HEAD_DIM] for g in range(GROUP)], axis=0)


def _masked_softmax(s, valid):
    s = jnp.where(valid, s, NEG_INF)
    e = jnp.where(valid, jnp.exp(s - jnp.max(s, axis=-1, keepdims=True)), 0.0)
    l = jnp.sum(e, axis=-1, keepdims=True)
    return e / jnp.where(l > 0.0, l, 1.0)


def _topk_mask(score, k, axis):
    n = score.shape[axis]
    idx = lax.broadcasted_iota(jnp.int32, score.shape, axis).astype(F32)
    sel = jnp.zeros(score.shape, F32)
    s = score
    for _ in range(k):
        m = jnp.max(s, axis=axis, keepdims=True)
        first = jnp.min(jnp.where(s == m, idx, float(n)), axis=axis, keepdims=True)
        pick = idx == first
        sel = jnp.where(pick & (m > 0.5 * NEG_INF), 1.0, sel)
        s = jnp.where(pick, -3e38, s)
    return sel


def _block_bias(chosen):
    return ((chosen - 1.0) * -NEG_INF).astype(BF16)


def _flash_t(q_aug, k_ref, v_ref, diag, t_col, expand_t):
    n_heads = len(q_aug)

    def scores(j):
        k0 = pl.multiple_of(j * KEY_TILE, KEY_TILE)
        kt = k_ref[0, pl.ds(k0, KEY_TILE), :].astype(BF16)
        lhs = jnp.concatenate([kt, expand_t(j)], axis=1)
        return [_dot(lhs, qa) for qa in q_aug]

    def values_t(j):
        k0 = pl.multiple_of(j * KEY_TILE, KEY_TILE)
        vt = v_ref[0, pl.ds(k0, KEY_TILE), :].T.astype(BF16)
        return [vt[h * HEAD_DIM:(h + 1) * HEAD_DIM] for h in range(n_heads)]

    visible = diag * KEY_TILE + lax.broadcasted_iota(jnp.int32, (KEY_TILE, 1), 0) <= t_col
    init = []
    for s, vt in zip(scores(diag), values_t(diag)):
        s = jnp.where(visible, s, NEG_INF)
        m0 = jnp.max(s, axis=0, keepdims=True)
        p = jnp.exp2(s - m0)
        init += [m0, jnp.sum(p, axis=0, keepdims=True), _dot(vt, p.astype(BF16))]

    def update(stats, s, vt):
        m_i, l_i, acc = stats
        m_new = jnp.maximum(m_i, jnp.max(s, axis=0, keepdims=True))
        alpha = jnp.exp2(m_i - m_new)
        p = jnp.exp2(s - m_new)
        return [m_new, alpha * l_i + jnp.sum(p, axis=0, keepdims=True), alpha * acc + _dot(vt, p.astype(BF16))]

    def body(i, carry):
        ja, jb = 2 * i, 2 * i + 1
        live = jb < diag
        jb = jnp.minimum(jb, diag)
        sa, va = scores(ja), values_t(ja)
        sb, vb = [jnp.where(live, s, NEG_INF) for s in scores(jb)], values_t(jb)
        out = []
        for h in range(n_heads):
            out += update(update(carry[3 * h:3 * h + 3], sa[h], va[h]), sb[h], vb[h])
        return tuple(out)

    res = lax.fori_loop(0, (diag + 1) // 2, body, tuple(init))
    return [res[3 * h + 2] / res[3 * h + 1] for h in range(n_heads)]


def _head_cols(x_t, kvh):
    x = jnp.concatenate([x_t[(GROUP * kvh + g) * HEAD_DIM:(GROUP * kvh + g + 1) * HEAD_DIM] for g in range(GROUP)],
                        axis=1)
    return jnp.concatenate([x if k == kvh else jnp.zeros_like(x) for k in range(KV_HEADS)], axis=0)


def _selection_map(n_cmp, nc_pad, ns_pad):
    c0 = np.arange(nc_pad)[:, None] * CMP_STRIDE
    j0 = np.arange(ns_pad)[None, :] * SEL_BLOCK
    ov = np.clip(np.minimum(c0 + CMP_LEN, j0 + SEL_BLOCK) - np.maximum(c0, j0), 0, None) / CMP_STRIDE
    return np.where(np.arange(nc_pad)[:, None] < n_cmp, ov, 0.0).astype(np.float32)


def _window_branch(qr, win_ref, w0, span, lo, w_valid):
    wk = win_ref[0, pl.ds(w0, span), lo:lo + HEAD_DIM].astype(BF16)
    wv = win_ref[0, pl.ds(w0, span), KV_WIDTH + lo:KV_WIDTH + lo + HEAD_DIM].astype(BF16)
    s = jnp.where(w_valid, _dot_nt(qr, wk), NEG_INF)
    e = jnp.exp(s - jnp.max(s, axis=-1, keepdims=True))
    return _dot(e.astype(BF16), wv) / jnp.sum(e, axis=-1, keepdims=True)


def _write_heads(o_ref, kvh, tq, head_out):
    for g in range(GROUP):
        h = GROUP * kvh + g
        o_ref[0, :, h * HEAD_DIM:(h + 1) * HEAD_DIM] = head_out(g, h, slice(g * tq, (g + 1) * tq))


def _nsa_kernel(qa_ref, qar_ref, gate_ref, kc_ref, vct_ref, sk_ref, sv_ref, win_ref, mapt_ref, o_ref, *, tq, span):
    i = pl.program_id(1)
    t0 = i * tq
    r = GROUP * tq
    nc_pad = kc_ref.shape[1]
    ns_pad = mapt_ref.shape[0]
    t_col = t0 + (lax.broadcasted_iota(jnp.int32, (1, r), 1) & (tq - 1))
    c_end = lax.broadcasted_iota(jnp.int32, (nc_pad, 1), 0) * CMP_STRIDE + CMP_LEN
    blk_t = lax.broadcasted_iota(jnp.int32, (ns_pad, 1), 0)
    blk_lane = lax.broadcasted_iota(jnp.int32, (1, ns_pad), 1)
    key_blk = lax.broadcasted_iota(jnp.int32, (KEY_TILE, 1), 0) >> SEL_SHIFT
    jt_t = (t0 + lax.broadcasted_iota(jnp.int32, (1, tq), 1)) >> SEL_SHIFT
    diag = t0 // KEY_TILE
    w0 = pl.multiple_of(i * tq, 8)
    wpos = t0 - WINDOW + lax.broadcasted_iota(jnp.int32, (span, 1), 0)
    w_valid = (wpos >= 0) & (wpos <= t_col) & (wpos >= t_col - WINDOW)
    c_valid = c_end <= t_col + 1

    qa_t = qa_ref[0].T
    qar_t = qar_ref[0].T
    gates_t = gate_ref[0].T
    kc = kc_ref[0].astype(BF16)
    wk = win_ref[0, pl.ds(w0, span), 0:KV_WIDTH].astype(BF16)
    wv_t = win_ref[0, pl.ds(w0, span), KV_WIDTH:2 * KV_WIDTH].T.astype(BF16)

    def expand_t(j):
        return (key_blk + j * (KEY_TILE // SEL_BLOCK) == blk_lane).astype(BF16)

    q_aug, o_cmp_t, o_win_t = [], [], []
    for kvh in range(KV_HEADS):
        rows = slice(kvh * HEAD_DIM, (kvh + 1) * HEAD_DIM)
        q_t = (_head_cols(qa_t, kvh) * SCALE).astype(BF16)
        s = jnp.where(c_valid, _dot(kc, q_t), NEG_INF)
        e = jnp.where(c_valid, jnp.exp(s - jnp.max(s, axis=0, keepdims=True)), 0.0)
        l = jnp.sum(e, axis=0, keepdims=True)
        p_cmp = e / jnp.where(l > 0.0, l, 1.0)
        o_cmp_t.append(_dot(vct_ref[0, rows, :].astype(BF16), p_cmp.astype(BF16)))

        p_sum = p_cmp[:, 0:tq]
        for g in range(1, GROUP):
            p_sum = p_sum + p_cmp[:, g * tq:(g + 1) * tq]
        imp_t = jnp.dot(mapt_ref[...], p_sum, preferred_element_type=F32, precision=lax.Precision.HIGHEST)
        score_t = jnp.where((blk_t == jt_t) | (blk_t == 0), FORCE_SCORE, jnp.where(blk_t < jt_t, imp_t, NEG_INF))
        bias_t = _block_bias(_topk_mask(score_t, SEL_TOPK, 0))
        qr_t = (_head_cols(qar_t, kvh) * (SCALE * LOG2E)).astype(BF16)
        q_aug.append(jnp.concatenate([qr_t, jnp.concatenate([bias_t] * GROUP, axis=1)], axis=0))

        s = jnp.where(w_valid, _dot(wk, qr_t), NEG_INF)
        e = jnp.exp2(s - jnp.max(s, axis=0, keepdims=True))
        o_win_t.append(_dot(wv_t[rows], e.astype(BF16)) / jnp.sum(e, axis=0, keepdims=True))

    o_sel_t = _flash_t(q_aug, sk_ref, sv_ref, diag, t_col, expand_t)

    heads = []
    for kvh in range(KV_HEADS):
        for g in range(GROUP):
            h = GROUP * kvh + g
            cols = slice(g * tq, (g + 1) * tq)
            heads.append(gates_t[3 * h:3 * h + 1] * o_cmp_t[kvh][:, cols]
                         + gates_t[3 * h + 1:3 * h + 2] * o_sel_t[kvh][:, cols]
                         + gates_t[3 * h + 2:3 * h + 3] * o_win_t[kvh][:, cols])
    o_ref[0] = jnp.concatenate(heads, axis=0).T


def _nsa(qa, qar, gate, kc, vc_t, full, win, tq):
    b, t, _ = qa.shape
    l = full.shape[1]
    nc_pad = kc.shape[1]
    ns_pad = _round_up(l // SEL_BLOCK, LANES)
    span = _round_up(WINDOW + tq, LANES)
    map_t = jnp.asarray(_selection_map(l // CMP_STRIDE - 1, nc_pad, ns_pad).T)
    qspec = pl.BlockSpec((1, tq, NSA_WIDTH), lambda bi, i: (bi, i, 0))
    seq = lambda a: pl.BlockSpec((1,) + a.shape[1:], lambda bi, i: (bi, 0, 0))
    return pl.pallas_call(
        functools.partial(_nsa_kernel, tq=tq, span=span),
        grid=(b, t // tq),
        in_specs=[
            qspec, qspec,
            pl.BlockSpec((1, tq, LANES), lambda bi, i: (bi, i, 0)),
            seq(kc), seq(vc_t),
            pl.BlockSpec((1, l, KV_WIDTH), lambda bi, i: (bi, 0, 2)),
            pl.BlockSpec((1, l, KV_WIDTH), lambda bi, i: (bi, 0, 3)),
            seq(win),
            _resident(map_t),
        ],
        out_specs=qspec,
        out_shape=jax.ShapeDtypeStruct((b, t, NSA_WIDTH), F32),
        compiler_params=_params(("parallel", "arbitrary")),
        name="nsa",
    )(qa, qar, gate, kc, vc_t, full, full, win, map_t)


def _moba_kernel(qb_ref, km_ref, mk_ref, mv_ref, o_ref, *, tq):
    i = pl.program_id(1)
    t0 = i * tq
    r = GROUP * tq
    nb_pad = km_ref.shape[1]
    t_col = t0 + (lax.broadcasted_iota(jnp.int32, (1, r), 1) & (tq - 1))
    bt_t = t_col >> MOBA_SHIFT
    nblk_t = lax.broadcasted_iota(jnp.int32, (nb_pad, 1), 0)
    nblk_lane = lax.broadcasted_iota(jnp.int32, (1, nb_pad), 1)
    diag = t0 // KEY_TILE
    qb_t = qb_ref[0].T
    km = km_ref[0]

    def expand_t(j):
        return jnp.broadcast_to(nblk_lane == j, (KEY_TILE, nb_pad)).astype(BF16)

    q_aug = []
    for kvh in range(KV_HEADS):
        q_t = _head_cols(qb_t, kvh)
        gate_t = jnp.dot(km, q_t, preferred_element_type=F32, precision=lax.Precision.HIGHEST)
        gate_t = jnp.where(nblk_t < bt_t, gate_t, NEG_INF)
        chosen_t = jnp.where(nblk_t == bt_t, 1.0, _topk_mask(gate_t, MOBA_TOPK, 0))
        q_aug.append(jnp.concatenate([(q_t * (SCALE * LOG2E)).astype(BF16), _block_bias(chosen_t)], axis=0))

    o_t = _flash_t(q_aug, mk_ref, mv_ref, diag, t_col, expand_t)
    heads = [o_t[kvh][:, g * tq:(g + 1) * tq] for kvh in range(KV_HEADS) for g in range(GROUP)]
    o_ref[0] = jnp.concatenate(heads, axis=0).T


def _moba(qb, kmean, full, tq):
    b, t, _ = qb.shape
    l = full.shape[1]
    qspec = pl.BlockSpec((1, tq, MOBA_WIDTH), lambda bi, i: (bi, i, 0))
    return pl.pallas_call(
        functools.partial(_moba_kernel, tq=tq),
        grid=(b, t // tq),
        in_specs=[
            qspec,
            pl.BlockSpec((1,) + kmean.shape[1:], lambda bi, i: (bi, 0, 0)),
            pl.BlockSpec((1, l, KV_WIDTH), lambda bi, i: (bi, 0, 4)),
            pl.BlockSpec((1, l, KV_WIDTH), lambda bi, i: (bi, 0, 5)),
        ],
        out_specs=qspec,
        out_shape=jax.ShapeDtypeStruct((b, t, MOBA_WIDTH), F32),
        compiler_params=_params(("parallel", "arbitrary")),
        name="moba",
    )(qb, kmean, full, full)


def _softmax_pv_t(s, vt):
    e = jnp.exp(s - jnp.max(s, axis=-1, keepdims=True))
    return _dot_nt(e.astype(BF16), vt) / jnp.sum(e, axis=-1, keepdims=True)


def _segment_bias(bias_rows, e_ref, block, n_keys):
    rows, seg = e_ref.shape
    parts = []
    for s in range(-(-n_keys // seg)):
        n = min(seg, n_keys - s * seg)
        b0 = s * (seg // block)
        parts.append(_dot(bias_rows[:, b0:b0 + rows], e_ref[:, 0:n]))
    return jnp.concatenate(parts, axis=1)


def _block_means_t(kt, n_keys, n_out):
    lane = lax.broadcasted_iota(jnp.int32, (1, n_out), 1)
    out = jnp.zeros((kt.shape[0], n_out), F32)
    for n in range(n_keys // MOBA_BLOCK):
        blk = kt[:, n * MOBA_BLOCK:(n + 1) * MOBA_BLOCK].astype(F32)
        out = jnp.where(lane == n, jnp.sum(blk, axis=1, keepdims=True) * (1.0 / MOBA_BLOCK), out)
    return out


def _sample_kernel(pt_ref, *refs, past_len, ts, span):
    pages = refs[:PAGES_PER_STEP]
    (new_ref, qa_ref, qar_ref, gate_ref, qb_ref, win_ref, map_ref, esel_ref, emoba_ref,
     posk_ref, w1k_ref, w2k_ref, posv_ref, w1v_ref, w2v_ref, oa_ref, ob_ref, tok_scr, kv_scr) = refs[PAGES_PER_STEP:]
    del pt_ref
    step = pl.program_id(1)
    n_steps = pl.num_programs(1)
    page = pages[0].shape[2]
    cmp_rows = 2 * KV_WIDTH

    m = past_len // CMP_STRIDE
    pitch = tok_scr.shape[1] // CMP_STRIDE
    chunks_per_page = page // CMP_STRIDE
    for j in range(PAGES_PER_STEP):
        chunk0 = (step * PAGES_PER_STEP + j) * chunks_per_page
        for slot in range(2):
            x_t = pages[j][0, slot * KV_WIDTH:(slot + 1) * KV_WIDTH, :].T
            for k in range(chunks_per_page):
                tok_scr[slot, pl.ds(chunk0 + k, CMP_STRIDE, stride=pitch), :] = (
                    x_t[k * CMP_STRIDE:(k + 1) * CMP_STRIDE])
    for k in range(past_len // (PAGES_PER_STEP * page)):
        @pl.when(step == k)
        def _(k=k):
            for j in range(PAGES_PER_STEP):
                c0 = (k * PAGES_PER_STEP + j) * page
                kv_scr[:, c0:c0 + page] = pages[j][0, cmp_rows:, :].astype(BF16)

    @pl.when(step == n_steps - 1)
    def _():
        lc = kv_scr.shape[1]
        r = GROUP * ts
        new_pad = jnp.concatenate([new_ref[0], jnp.zeros((page - ts, PAGED_WIDTH), F32)], axis=0)
        kv_scr[:, past_len:past_len + page] = new_pad.T[cmp_rows:, :].astype(BF16)

        kc_all = _compress_rows(lambda l: tok_scr[0, l * pitch:l * pitch + m, :], posk_ref, w1k_ref, w2k_ref, m)
        vc_all = _compress_rows(lambda l: tok_scr[1, l * pitch:l * pitch + m, :], posv_ref, w1v_ref, w2v_ref, m)

        ns_pad = map_ref.shape[1]
        nb_pad = LANES
        row = lax.broadcasted_iota(jnp.int32, (r, 1), 0)
        t_row = past_len + (row & (ts - 1))
        t_tok = past_len + lax.broadcasted_iota(jnp.int32, (ts, 1), 0)
        causal = lax.broadcasted_iota(jnp.int32, (1, lc), 1) <= t_row
        c_end = lax.broadcasted_iota(jnp.int32, (1, m), 1) * CMP_STRIDE + CMP_LEN
        blk = lax.broadcasted_iota(jnp.int32, (1, ns_pad), 1)
        nblk = lax.broadcasted_iota(jnp.int32, (1, nb_pad), 1)
        jt = t_tok >> SEL_SHIFT
        bt = t_row >> MOBA_SHIFT
        wpos = past_len - WINDOW + lax.broadcasted_iota(jnp.int32, (1, span), 1)
        w_valid = (wpos >= 0) & (wpos <= t_row) & (wpos >= t_row - WINDOW)
        gates = gate_ref[0]

        for kvh in range(KV_HEADS):
            lo = kvh * HEAD_DIM
            q = (_stack_heads(qa_ref, kvh) * SCALE).astype(BF16)
            qr = (_stack_heads(qar_ref, kvh) * SCALE).astype(BF16)

            p_cmp = _masked_softmax(_dot_nt(q, kc_all[:, lo:lo + HEAD_DIM].astype(BF16)), c_end <= t_row + 1)
            o_cmp = _dot(p_cmp.astype(BF16), vc_all[:, lo:lo + HEAD_DIM].astype(BF16))
            p_sum = p_cmp[0:ts]
            for g in range(1, GROUP):
                p_sum = p_sum + p_cmp[g * ts:(g + 1) * ts]
            imp = jnp.dot(p_sum, map_ref[...], preferred_element_type=F32, precision=lax.Precision.HIGHEST)
            score = jnp.where((blk == jt) | (blk == 0), FORCE_SCORE, jnp.where(blk < jt, imp, NEG_INF))
            sel = _topk_mask(score, SEL_TOPK, 1)
            bias_rows = _block_bias(jnp.concatenate([sel] * GROUP, axis=0))

            s = _dot(qr, kv_scr[lo:lo + HEAD_DIM, :]) + _segment_bias(bias_rows, esel_ref, SEL_BLOCK, lc)
            o_sel = _softmax_pv_t(jnp.where(causal, s, NEG_INF), kv_scr[KV_WIDTH + lo:KV_WIDTH + lo + HEAD_DIM, :])
            o_win = _window_branch(qr, win_ref, 0, span, lo, w_valid)
            _write_heads(oa_ref, kvh, ts, lambda g, h, rows: (
                gates[:, 3 * h:3 * h + 1] * o_cmp[rows] + gates[:, 3 * h + 1:3 * h + 2] * o_sel[rows]
                + gates[:, 3 * h + 2:3 * h + 3] * o_win[rows]))

            qf = _stack_heads(qb_ref, kvh)
            kt = kv_scr[2 * KV_WIDTH + lo:2 * KV_WIDTH + lo + HEAD_DIM, :]
            gate_s = jnp.dot(qf, _block_means_t(kt, past_len, nb_pad), preferred_element_type=F32,
                             precision=lax.Precision.HIGHEST)
            gate_s = jnp.where(nblk < bt, gate_s, NEG_INF)
            chosen = jnp.where(nblk == bt, 1.0, _topk_mask(gate_s, MOBA_TOPK, 1))
            s = _dot((qf * SCALE).astype(BF16), kt) + _segment_bias(_block_bias(chosen), emoba_ref, MOBA_BLOCK, lc)
            o = _softmax_pv_t(jnp.where(causal, s, NEG_INF),
                              kv_scr[3 * KV_WIDTH + lo:3 * KV_WIDTH + lo + HEAD_DIM, :])
            _write_heads(ob_ref, kvh, ts, lambda g, h, rows: o[rows])


def _block_expansion(n_rows, block, n_keys):
    return jnp.asarray(np.arange(n_rows)[:, None] == (np.arange(n_keys)[None, :] // block), dtype=BF16)


def _sample_attn(pool_t, page_table, page_base, new_rows, qa, qar, gate, qb, win, cmp_k, cmp_v):
    bs, n_pages = page_table.shape
    page = pool_t.shape[2]
    ts = new_rows.shape[1]
    past_len = n_pages * page
    n_steps = n_pages // PAGES_PER_STEP
    assert page == LANES and n_pages % PAGES_PER_STEP == 0 and ts < CMP_STRIDE and ts & (ts - 1) == 0
    assert past_len % MOBA_BLOCK == 0 and (past_len // CMP_STRIDE) % LANES == 0
    lc = past_len + page
    ns_pad = _round_up(lc // SEL_BLOCK, LANES)
    assert lc // MOBA_BLOCK + 1 <= LANES and past_len % BIAS_SEG == 0
    span = win.shape[1]
    sel_map = jnp.asarray(_selection_map(past_len // CMP_STRIDE - 1, past_len // CMP_STRIDE, ns_pad))
    e_sel = _block_expansion(BIAS_SEG // SEL_BLOCK, SEL_BLOCK, BIAS_SEG)
    e_moba = _block_expansion(2 * (BIAS_SEG // MOBA_BLOCK), MOBA_BLOCK, BIAS_SEG)

    def page_spec(j):
        return pl.BlockSpec((1, PAGED_WIDTH, page),
                            lambda b, s, pt: (page_base + pt[b, s * PAGES_PER_STEP + j], 0, 0))

    per_seq = lambda a: pl.BlockSpec((1,) + a.shape[1:], lambda b, s, pt: (b, 0, 0))
    consts = (sel_map, e_sel, e_moba) + tuple(cmp_k) + tuple(cmp_v)
    out_spec = pl.BlockSpec((1, ts, NSA_WIDTH), lambda b, s, pt: (b, 0, 0))
    grid_spec = pltpu.PrefetchScalarGridSpec(
        num_scalar_prefetch=1,
        grid=(bs, n_steps),
        in_specs=[page_spec(j) for j in range(PAGES_PER_STEP)]
        + [per_seq(a) for a in (new_rows, qa, qar, gate, qb, win)] + [_resident(a) for a in consts],
        out_specs=[out_spec, out_spec],
        scratch_shapes=[pltpu.VMEM((2, CMP_STRIDE * (past_len // CMP_STRIDE + 8), KV_WIDTH), F32),
                        pltpu.VMEM((PAGED_WIDTH - 2 * KV_WIDTH, lc), BF16)],
    )
    return pl.pallas_call(
        functools.partial(_sample_kernel, past_len=past_len, ts=ts, span=span),
        grid_spec=grid_spec,
        out_shape=[jax.ShapeDtypeStruct((bs, ts, NSA_WIDTH), F32)] * 2,
        compiler_params=_params(("parallel", "arbitrary")),
        name="sample_attn",
    )(page_table, *([pool_t] * PAGES_PER_STEP), new_rows, qa, qar, gate, qb, win, *consts)


def _out_kernel(x_ref, oa_ref, ob_ref, za_ref, zb_ref, wout_ref, gple_ref, wg_ref, ple_ref, wp_ref, gfin_ref,
                o_ref, *, final):
    mixed = jnp.concatenate([oa_ref[...] * za_ref[...], ob_ref[...] * zb_ref[...]], axis=-1).astype(BF16)
    x1 = x_ref[...] + _dot(mixed, wout_ref[...])
    gate = _sigmoid(_dot(_rms(x1, gple_ref[...]).astype(BF16), wg_ref[...]))
    x2 = x1 + gate * _dot(ple_ref[...].astype(BF16), wp_ref[...])
    o_ref[...] = _rms(x2, gfin_ref[...]) if final else x2


def _out(x2d, oa, ob, za, zb, w_out, g_ple, w_gate, ple, w_proj, g_final, final):
    n, d_model = x2d.shape
    tm = min(256, n)
    row = lambda a: pl.BlockSpec((tm, a.shape[1]), lambda i: (i, 0))
    args = (x2d, oa, ob, za, zb, w_out, g_ple, w_gate, ple, w_proj, g_final)
    specs = [row(x2d), row(oa), row(ob), row(za), row(zb), _resident(w_out), _resident(g_ple), _resident(w_gate),
             row(ple), _resident(w_proj), _resident(g_final)]
    return pl.pallas_call(
        functools.partial(_out_kernel, final=final),
        grid=(n // tm,),
        in_specs=specs,
        out_specs=row(x2d),
        out_shape=jax.ShapeDtypeStruct((n, d_model), F32),
        compiler_params=_params(("parallel",)),
        name="out",
    )(*args)


def _mixer_layer(x, ple, pos, past, win_prev, weights, g_final, final):
    (g_mix, w_split, w_out, cmp_k, cmp_v, g_ple, w_gate, w_proj) = weights
    b, t, d_model = x.shape
    n = b * t
    qa, qar, paged, win_new, gate, za, qb, zb, *transposed = _proj(x.reshape(n, d_model), pos, g_mix, w_split)
    paged3 = paged.reshape(b, t, PAGED_WIDTH)
    win_new3 = win_new.reshape(b, t, 2 * KV_WIDTH)
    tq = min(KEY_TILE, t)
    assert KEY_TILE % tq == 0 and tq & (tq - 1) == 0
    span = _round_up(WINDOW + tq, LANES)
    win_rows = max(WINDOW + t, t - tq + span)
    win = jnp.concatenate([win_prev, win_new3, jnp.zeros((b, win_rows - WINDOW - t, 2 * KV_WIDTH), F32)], axis=1)
    qa3, qar3, gate3, qb3 = (a.reshape(b, t, -1) for a in (qa, qar, gate, qb))
    if past is None:
        assert t % KEY_TILE == 0
        nc_pad = _round_up(t // CMP_STRIDE, LANES)
        kc = _compress(paged3, 0, *cmp_k, nc_pad, False)
        vc_t = _compress(paged3, 1, *cmp_v, nc_pad, True)
        o_a = _nsa(qa3, qar3, gate3, kc, vc_t, paged3, win, tq)
        o_b = _moba(qb3, _kmean(paged3, LANES), paged3, tq)
    else:
        o_a, o_b = _sample_attn(*past, paged3, qa3, qar3, gate3, qb3, win, cmp_k, cmp_v)
    x_next = _out(x.reshape(n, d_model), o_a.reshape(n, -1), o_b.reshape(n, -1), za, zb, w_out, g_ple, w_gate,
                  ple.reshape(n, -1), w_proj, g_final, final)
    if transposed:
        new_paged, new_win = (a.reshape(b, -1, KV_HEADS, HEAD_DIM, t).transpose(0, 4, 1, 2, 3) for a in transposed)
    else:
        new_paged, new_win = (a.reshape(b, t, -1, KV_HEADS, HEAD_DIM) for a in (paged3, win_new3))
    return x_next.reshape(b, t, d_model), new_paged, new_win


def _split_w_in(w_in):
    sizes = (NSA_WIDTH, 6 * KV_WIDTH, 3 * NSA_HEADS, NSA_WIDTH, MOBA_WIDTH, 2 * KV_WIDTH, MOBA_WIDTH)
    offs = np.concatenate([[0], np.cumsum(sizes)])
    wqa, wkva, wgate, wza, wqb, wkvb, wzb = [w_in[:, offs[k]:offs[k + 1]].astype(BF16) for k in range(len(sizes))]
    wgate = jnp.pad(wgate, ((0, 0), (0, LANES - wgate.shape[1])))
    return wqa, wkva, wgate, wza, wqb, wkvb, wzb


def kernel(x_prompt, x_sample, cache_paged_kv, cache_win_kv, page_table, p_prompt, p_sample, g_mix, w_in, w_out,
           cmp_pos_k, cmp_w1_k, cmp_w2_k, cmp_pos_v, cmp_w1_v, cmp_w2_v, g_ple, w_ple_gate, w_ple_proj, g_final):
    depth = w_in.shape[0]
    bp, tp, _ = x_prompt.shape
    bs, ts, _ = x_sample.shape
    n_pool, page = cache_paged_kv.shape[1], cache_paged_kv.shape[2]
    past_len = page_table.shape[1] * page
    wbuf = cache_win_kv.shape[2]
    pos_p = jnp.arange(tp, dtype=jnp.int32)
    pos_s = past_len + jnp.arange(ts, dtype=jnp.int32)
    pool_t = cache_paged_kv.transpose(0, 1, 3, 4, 5, 2).reshape(depth * n_pool, PAGED_WIDTH, page)
    g_fin = g_final.reshape(1, -1)
    xp, xs = x_prompt, x_sample
    new_pp, new_pw, new_sp, new_sw = [], [], [], []
    for i in range(depth):
        final = i == depth - 1
        weights = (g_mix[i].reshape(1, -1), _split_w_in(w_in[i]), w_out[i].astype(BF16),
                   _compress_weights(cmp_pos_k[i], cmp_w1_k[i], cmp_w2_k[i]),
                   _compress_weights(cmp_pos_v[i], cmp_w1_v[i], cmp_w2_v[i]),
                   g_ple[i].reshape(1, -1), w_ple_gate[i].astype(BF16), w_ple_proj[i].astype(BF16))
        win0 = jnp.zeros((bp, WINDOW, 2 * KV_WIDTH), F32)
        xp, pk, pw = _mixer_layer(xp, p_prompt[i], pos_p, None, win0, weights, g_fin, final)
        win_prev = jnp.concatenate([jnp.zeros((bs, WINDOW - wbuf, 2 * KV_WIDTH), F32),
                                    cache_win_kv[i].reshape(bs, wbuf, 2 * KV_WIDTH)], axis=1)
        xs, sk, sw = _mixer_layer(xs, p_sample[i], pos_s, (pool_t, page_table, i * n_pool), win_prev, weights,
                                  g_fin, final)
        new_pp.append(pk)
        new_pw.append(pw[:, tp - min(WINDOW, tp):])
        new_sp.append(sk)
        new_sw.append(sw)
    return (xp, xs, jnp.stack(new_pp), jnp.stack(new_pw), jnp.stack(new_sp), jnp.stack(new_sw))
```

```python
import functools

import numpy as np
import jax
import jax.numpy as jnp
from jax import lax
from jax.experimental import pallas as pl
from jax.experimental.pallas import tpu as pltpu

HEAD_DIM = 64
NSA_HEADS = 8
MOBA_HEADS = 8
KV_HEADS = 2
GROUP = NSA_HEADS // KV_HEADS
ROT_DIM = HEAD_DIM // 4
ROPE_THETA = 500000.0
CMP_LEN = 32
CMP_STRIDE = 16
SEL_BLOCK = 64
SEL_TOPK = 8
WINDOW = 512
MOBA_BLOCK = 256
MOBA_TOPK = 3
N_PAGED_SLOTS = 6
RMS_EPS = 1e-6
NEG_INF = -1e30
FORCE_SCORE = 1e4
SCALE = HEAD_DIM ** -0.5
LOG2E = 1.4426950408889634
SEL_SHIFT = SEL_BLOCK.bit_length() - 1
MOBA_SHIFT = MOBA_BLOCK.bit_length() - 1

LANES = 128
KV_WIDTH = KV_HEADS * HEAD_DIM
NSA_WIDTH = NSA_HEADS * HEAD_DIM
MOBA_WIDTH = MOBA_HEADS * HEAD_DIM
PAGED_WIDTH = N_PAGED_SLOTS * KV_WIDTH
KEY_TILE = 256
PAGES_PER_STEP = 8
BIAS_SEG = 2048
VMEM_LIMIT = 56 * 1024 * 1024

F32 = jnp.float32
BF16 = jnp.bfloat16
NT_DIMS = (((1,), (1,)), ((), ()))


def _round_up(n, m):
    return -(-n // m) * m


def _params(sem):
    return pltpu.CompilerParams(dimension_semantics=sem, vmem_limit_bytes=VMEM_LIMIT)


def _rms(x, g):
    return x * lax.rsqrt(jnp.mean(x * x, axis=-1, keepdims=True) + RMS_EPS) * g


def _sigmoid(x):
    return 1.0 / (1.0 + jnp.exp(-x))


def _dot(a, b):
    return jnp.dot(a, b, preferred_element_type=F32)


def _dot_nt(a, b):
    return lax.dot_general(a, b, NT_DIMS, preferred_element_type=F32)


def _dot_nt_exact(a, b):
    return lax.dot_general(a, b, NT_DIMS, preferred_element_type=F32, precision=lax.Precision.HIGHEST)


def _resident(a):
    return pl.BlockSpec(a.shape, lambda *_: (0,) * a.ndim, pipeline_mode=pl.Buffered(1))


def _rope128(v, c, sp, sm):
    half = ROT_DIM // 2
    return v * c + pltpu.roll(v, half, axis=1) * sp + pltpu.roll(v, LANES - half, axis=1) * sm


def _rope(v, c, sp, sm):
    n = v.shape[-1] // LANES
    return jnp.concatenate([_rope128(v[:, i * LANES:(i + 1) * LANES], c, sp, sm) for i in range(n)], axis=-1)


def _proj_kernel(x_ref, g_ref, c_ref, sp_ref, sm_ref, wqa, wkva, wgate, wza, wqb, wkvb, wzb,
                 qa_o, qar_o, paged_o, win_o, gate_o, za_o, qb_o, zb_o, *transposed_o):
    h = _rms(x_ref[...], g_ref[...]).astype(BF16)
    c, sp, sm = c_ref[...], sp_ref[...], sm_ref[...]
    qa = _dot(h, wqa[...])
    qa_o[...] = qa
    qar_o[...] = _rope(qa, c, sp, sm)
    kva = _dot(h, wkva[...])
    paged_o[:, 0:2 * KV_WIDTH] = kva[:, 0:2 * KV_WIDTH]
    paged_o[:, 2 * KV_WIDTH:3 * KV_WIDTH] = _rope(kva[:, 2 * KV_WIDTH:3 * KV_WIDTH], c, sp, sm)
    paged_o[:, 3 * KV_WIDTH:4 * KV_WIDTH] = kva[:, 3 * KV_WIDTH:4 * KV_WIDTH]
    win_o[:, 0:KV_WIDTH] = _rope(kva[:, 4 * KV_WIDTH:5 * KV_WIDTH], c, sp, sm)
    win_o[:, KV_WIDTH:2 * KV_WIDTH] = kva[:, 5 * KV_WIDTH:6 * KV_WIDTH]
    kvb = _dot(h, wkvb[...])
    paged_o[:, 4 * KV_WIDTH:5 * KV_WIDTH] = _rope(kvb[:, 0:KV_WIDTH], c, sp, sm)
    paged_o[:, 5 * KV_WIDTH:6 * KV_WIDTH] = kvb[:, KV_WIDTH:2 * KV_WIDTH]
    gate_o[...] = _sigmoid(_dot(h, wgate[...]))
    za = _dot(h, wza[...])
    za_o[...] = za * _sigmoid(za)
    qb_o[...] = _rope(_dot(h, wqb[...]), c, sp, sm)
    zb = _dot(h, wzb[...])
    zb_o[...] = zb * _sigmoid(zb)
    if transposed_o:
        paged_t_o, win_t_o = transposed_o
        paged_t_o[0] = paged_o[...].T
        win_t_o[0] = win_o[...].T


def _rope_tables(pos, n_rows):
    half = ROT_DIM // 2
    inv_freq = ROPE_THETA ** (-jnp.arange(half, dtype=F32) / half)
    ang = pos.astype(F32)[:, None] * inv_freq[None, :]
    cos, sin = jnp.cos(ang), jnp.sin(ang)
    d = np.arange(LANES) % HEAD_DIM
    idx = d % half
    c = jnp.where(d < ROT_DIM, cos[:, idx], 1.0)
    sp = jnp.where((d >= half) & (d < ROT_DIM), sin[:, idx], 0.0)
    sm = jnp.where(d < half, -sin[:, idx], 0.0)
    rep = n_rows // pos.shape[0]
    return tuple(jnp.tile(t, (rep, 1)) for t in (c, sp, sm))


def _proj(x2d, pos, g, w):
    n, d_model = x2d.shape
    t = pos.shape[0]
    tm = min(256, n)
    tab_rows = max(t, tm)
    tabs = _rope_tables(pos, tab_rows)
    n_tab = tab_rows // tm
    row = lambda width: pl.BlockSpec((tm, width), lambda i: (i, 0))
    tab = pl.BlockSpec((tm, LANES), lambda i: (i % n_tab, 0))
    widths = (NSA_WIDTH, NSA_WIDTH, PAGED_WIDTH, 2 * KV_WIDTH, LANES, NSA_WIDTH, MOBA_WIDTH, MOBA_WIDTH)
    out_specs = [row(wd) for wd in widths]
    out_shape = [jax.ShapeDtypeStruct((n, wd), F32) for wd in widths]
    if t % tm == 0:
        tiles = t // tm
        for wd in (PAGED_WIDTH, 2 * KV_WIDTH):
            out_specs.append(pl.BlockSpec((1, wd, tm), lambda i: (i // tiles, 0, i % tiles)))
            out_shape.append(jax.ShapeDtypeStruct((n // t, wd, t), F32))
    return pl.pallas_call(
        _proj_kernel,
        grid=(n // tm,),
        in_specs=[row(d_model), _resident(g), tab, tab, tab] + [_resident(a) for a in w],
        out_specs=out_specs,
        out_shape=out_shape,
        compiler_params=_params(("parallel",)),
        name="proj",
    )(x2d, g, *tabs, *w)


def _compress_rows(chunk_row, pos_ref, w1_ref, w2_ref, m):
    assert KV_HEADS == 2 and KV_WIDTH == LANES
    hidden = w2_ref.shape[0]
    low = lax.broadcasted_iota(jnp.int32, (1, LANES), 1) < HEAD_DIM
    heads = [[], []]
    for l in range(0, CMP_STRIDE, 2):
        a, b = chunk_row(l), chunk_row(l + 1)
        heads[0].append(jnp.where(low, a, pltpu.roll(b, HEAD_DIM, axis=1)).astype(BF16))
        heads[1].append(jnp.where(low, pltpu.roll(a, HEAD_DIM, axis=1), b).astype(BF16))
    x = jnp.concatenate([jnp.concatenate(h, axis=1) for h in heads] + [pos_ref[...]], axis=0)
    ab = _dot(x, w1_ref[...])
    bias = ab[2 * m:2 * m + 1, 0:hidden] + ab[2 * m + 1:2 * m + 2, hidden:2 * hidden]
    hid = []
    for h in range(KV_HEADS):
        rows = slice(h * m, (h + 1) * m)
        nxt = pltpu.roll(ab[rows, hidden:2 * hidden], m - 1, axis=0)
        pre = ab[rows, 0:hidden] + nxt + bias
        hid.append((pre * _sigmoid(pre)).astype(BF16))
    out = _dot(jnp.concatenate(hid, axis=0), w2_ref[...])
    return [out[h * m:(h + 1) * m] for h in range(KV_HEADS)]


def _compress_kernel(src_ref, pos_ref, w1_ref, w2_ref, o_ref, *, n_chunks, nc_pad, transposed):
    m = n_chunks
    out = jnp.concatenate(
        _compress_rows(lambda l: src_ref[0, pl.ds(l, m, stride=CMP_STRIDE), :], pos_ref, w1_ref, w2_ref, m), axis=1)
    if nc_pad > m:
        out = jnp.concatenate([out, jnp.zeros((nc_pad - m, LANES), F32)], axis=0)
    o_ref[0] = out.T if transposed else out


def _compress(full, slot, pos_rows, w1, w2, nc_pad, transposed):
    b, l, _ = full.shape
    n_chunks = l // CMP_STRIDE
    out_dims = (KV_WIDTH, nc_pad) if transposed else (nc_pad, KV_WIDTH)
    return pl.pallas_call(
        functools.partial(_compress_kernel, n_chunks=n_chunks, nc_pad=nc_pad, transposed=transposed),
        grid=(b,),
        in_specs=[pl.BlockSpec((1, l, KV_WIDTH), lambda i: (i, 0, slot)),
                  _resident(pos_rows), _resident(w1), _resident(w2)],
        out_specs=pl.BlockSpec((1,) + out_dims, lambda i: (i, 0, 0)),
        out_shape=jax.ShapeDtypeStruct((b,) + out_dims, F32),
        compiler_params=_params(("parallel",)),
        name="compress",
    )(full, pos_rows, w1, w2)


def _compress_weights(pos_emb, w1, w2):
    hidden = w1.shape[-1]
    rows = CMP_STRIDE * HEAD_DIM
    w1cat = jnp.concatenate([w1[:CMP_STRIDE].reshape(rows, hidden), w1[CMP_STRIDE:].reshape(rows, hidden)], axis=1)
    pos_rows = jnp.zeros((16, rows), F32)
    pos_rows = pos_rows.at[0].set(pos_emb[:CMP_STRIDE].reshape(-1)).at[1].set(pos_emb[CMP_STRIDE:].reshape(-1))
    return pos_rows.astype(BF16), w1cat.astype(BF16), w2.astype(BF16)


def _kmean_kernel(src_ref, o_ref, *, n_blocks):
    x = src_ref[0].reshape(n_blocks, MOBA_BLOCK, KV_WIDTH)
    o_ref[0, 0:n_blocks, :] = jnp.sum(x, axis=1) * (1.0 / MOBA_BLOCK)
    o_ref[0, n_blocks:, :] = jnp.zeros((o_ref.shape[1] - n_blocks, KV_WIDTH), F32)


def _kmean(full, nb_pad):
    b, l, _ = full.shape
    return pl.pallas_call(
        functools.partial(_kmean_kernel, n_blocks=l // MOBA_BLOCK),
        grid=(b,),
        in_specs=[pl.BlockSpec((1, l, KV_WIDTH), lambda i: (i, 0, 4))],
        out_specs=pl.BlockSpec((1, nb_pad, KV_WIDTH), lambda i: (i, 0, 0)),
        out_shape=jax.ShapeDtypeStruct((b, nb_pad, KV_WIDTH), F32),
        compiler_params=_params(("parallel",)),
        name="kmean",
    )(full)


def _stack_heads(ref, kvh):
    return jnp.concatenate(
        [ref[0, :, (GROUP * kvh + g) * HEAD_DIM:(GROUP * kvh + g + 1) * HEAD_DIM] for g in range(GROUP)], axis=0)


def _masked_softmax(s, valid):
    s = jnp.where(valid, s, NEG_INF)
    e = jnp.where(valid, jnp.exp(s - jnp.max(s, axis=-1, keepdims=True)), 0.0)
    l = jnp.sum(e, axis=-1, keepdims=True)
    return e / jnp.where(l > 0.0, l, 1.0)


def _topk_mask(score, k, axis):
    n = score.shape[axis]
    idx = lax.broadcasted_iota(jnp.int32, score.shape, axis).astype(F32)
    sel = jnp.zeros(score.shape, F32)
    s = score
    for _ in range(k):
        m = jnp.max(s, axis=axis, keepdims=True)
        first = jnp.min(jnp.where(s == m, idx, float(n)), axis=axis, keepdims=True)
        pick = idx == first
        sel = jnp.where(pick & (m > 0.5 * NEG_INF), 1.0, sel)
        s = jnp.where(pick, -3e38, s)
    return sel


def _block_bias(chosen):
    return ((chosen - 1.0) * -NEG_INF).astype(BF16)


def _flash_t(q_aug, k_ref, v_ref, diag, t_col, expand_t):
    n_heads = len(q_aug)

    def scores(j):
        k0 = pl.multiple_of(j * KEY_TILE, KEY_TILE)
        kt = k_ref[0, pl.ds(k0, KEY_TILE), :].astype(BF16)
        lhs = jnp.concatenate([kt, expand_t(j)], axis=1)
        return [_dot(lhs, qa) for qa in q_aug]

    def values_t(j):
        k0 = pl.multiple_of(j * KEY_TILE, KEY_TILE)
        vt = v_ref[0, pl.ds(k0, KEY_TILE), :].T.astype(BF16)
        return [vt[h * HEAD_DIM:(h + 1) * HEAD_DIM] for h in range(n_heads)]

    visible = diag * KEY_TILE + lax.broadcasted_iota(jnp.int32, (KEY_TILE, 1), 0) <= t_col
    init = []
    for s, vt in zip(scores(diag), values_t(diag)):
        s = jnp.where(visible, s, NEG_INF)
        m0 = jnp.max(s, axis=0, keepdims=True)
        p = jnp.exp2(s - m0)
        init += [m0, jnp.sum(p, axis=0, keepdims=True), _dot(vt, p.astype(BF16))]

    def update(stats, s, vt):
        m_i, l_i, acc = stats
        m_new = jnp.maximum(m_i, jnp.max(s, axis=0, keepdims=True))
        alpha = jnp.exp2(m_i - m_new)
        p = jnp.exp2(s - m_new)
        return [m_new, alpha * l_i + jnp.sum(p, axis=0, keepdims=True), alpha * acc + _dot(vt, p.astype(BF16))]

    def body(i, carry):
        ja, jb = 2 * i, 2 * i + 1
        live = jb < diag
        jb = jnp.minimum(jb, diag)
        sa, va = scores(ja), values_t(ja)
        sb, vb = [jnp.where(live, s, NEG_INF) for s in scores(jb)], values_t(jb)
        out = []
        for h in range(n_heads):
            out += update(update(carry[3 * h:3 * h + 3], sa[h], va[h]), sb[h], vb[h])
        return tuple(out)

    res = lax.fori_loop(0, (diag + 1) // 2, body, tuple(init))
    return [res[3 * h + 2] / res[3 * h + 1] for h in range(n_heads)]


def _head_cols(x_t, kvh):
    x = jnp.concatenate([x_t[(GROUP * kvh + g) * HEAD_DIM:(GROUP * kvh + g + 1) * HEAD_DIM] for g in range(GROUP)],
                        axis=1)
    return jnp.concatenate([x if k == kvh else jnp.zeros_like(x) for k in range(KV_HEADS)], axis=0)


def _selection_map(n_cmp, nc_pad, ns_pad):
    c0 = np.arange(nc_pad)[:, None] * CMP_STRIDE
    j0 = np.arange(ns_pad)[None, :] * SEL_BLOCK
    ov = np.clip(np.minimum(c0 + CMP_LEN, j0 + SEL_BLOCK) - np.maximum(c0, j0), 0, None) / CMP_STRIDE
    return np.where(np.arange(nc_pad)[:, None] < n_cmp, ov, 0.0).astype(np.float32)


def _window_branch(qr, win_ref, w0, span, lo, w_valid):
    wk = win_ref[0, pl.ds(w0, span), lo:lo + HEAD_DIM].astype(BF16)
    wv = win_ref[0, pl.ds(w0, span), KV_WIDTH + lo:KV_WIDTH + lo + HEAD_DIM].astype(BF16)
    s = jnp.where(w_valid, _dot_nt(qr, wk), NEG_INF)
    e = jnp.exp(s - jnp.max(s, axis=-1, keepdims=True))
    return _dot(e.astype(BF16), wv) / jnp.sum(e, axis=-1, keepdims=True)


def _write_heads(o_ref, kvh, tq, head_out):
    for g in range(GROUP):
        h = GROUP * kvh + g
        o_ref[0, :, h * HEAD_DIM:(h + 1) * HEAD_DIM] = head_out(g, h, slice(g * tq, (g + 1) * tq))


def _nsa_kernel(qa_ref, qar_ref, gate_ref, kc_ref, vct_ref, sk_ref, sv_ref, win_ref, mapt_ref, o_ref, *, tq, span):
    i = pl.program_id(1)
    t0 = i * tq
    r = GROUP * tq
    nc_pad = kc_ref.shape[1]
    ns_pad = mapt_ref.shape[0]
    t_col = t0 + (lax.broadcasted_iota(jnp.int32, (1, r), 1) & (tq - 1))
    c_end = lax.broadcasted_iota(jnp.int32, (nc_pad, 1), 0) * CMP_STRIDE + CMP_LEN
    blk_t = lax.broadcasted_iota(jnp.int32, (ns_pad, 1), 0)
    blk_lane = lax.broadcasted_iota(jnp.int32, (1, ns_pad), 1)
    key_blk = lax.broadcasted_iota(jnp.int32, (KEY_TILE, 1), 0) >> SEL_SHIFT
    jt_t = (t0 + lax.broadcasted_iota(jnp.int32, (1, tq), 1)) >> SEL_SHIFT
    diag = t0 // KEY_TILE
    w0 = pl.multiple_of(i * tq, 8)
    wpos = t0 - WINDOW + lax.broadcasted_iota(jnp.int32, (span, 1), 0)
    w_valid = (wpos >= 0) & (wpos <= t_col) & (wpos >= t_col - WINDOW)
    c_valid = c_end <= t_col + 1

    qa_t = qa_ref[0].T
    qar_t = qar_ref[0].T
    gates_t = gate_ref[0].T
    kc = kc_ref[0].astype(BF16)
    wk = win_ref[0, pl.ds(w0, span), 0:KV_WIDTH].astype(BF16)
    wv_t = win_ref[0, pl.ds(w0, span), KV_WIDTH:2 * KV_WIDTH].T.astype(BF16)

    def expand_t(j):
        return (key_blk + j * (KEY_TILE // SEL_BLOCK) == blk_lane).astype(BF16)

    q_aug, o_cmp_t, o_win_t = [], [], []
    for kvh in range(KV_HEADS):
        rows = slice(kvh * HEAD_DIM, (kvh + 1) * HEAD_DIM)
        q_t = (_head_cols(qa_t, kvh) * SCALE).astype(BF16)
        s = jnp.where(c_valid, _dot(kc, q_t), NEG_INF)
        e = jnp.where(c_valid, jnp.exp(s - jnp.max(s, axis=0, keepdims=True)), 0.0)
        l = jnp.sum(e, axis=0, keepdims=True)
        p_cmp = e / jnp.where(l > 0.0, l, 1.0)
        o_cmp_t.append(_dot(vct_ref[0, rows, :].astype(BF16), p_cmp.astype(BF16)))

        p_sum = p_cmp[:, 0:tq]
        for g in range(1, GROUP):
            p_sum = p_sum + p_cmp[:, g * tq:(g + 1) * tq]
        imp_t = jnp.dot(mapt_ref[...], p_sum, preferred_element_type=F32, precision=lax.Precision.HIGHEST)
        score_t = jnp.where((blk_t == jt_t) | (blk_t == 0), FORCE_SCORE, jnp.where(blk_t < jt_t, imp_t, NEG_INF))
        bias_t = _block_bias(_topk_mask(score_t, SEL_TOPK, 0))
        qr_t = (_head_cols(qar_t, kvh) * (SCALE * LOG2E)).astype(BF16)
        q_aug.append(jnp.concatenate([qr_t, jnp.concatenate([bias_t] * GROUP, axis=1)], axis=0))

        s = jnp.where(w_valid, _dot(wk, qr_t), NEG_INF)
        e = jnp.exp2(s - jnp.max(s, axis=0, keepdims=True))
        o_win_t.append(_dot(wv_t[rows], e.astype(BF16)) / jnp.sum(e, axis=0, keepdims=True))

    o_sel_t = _flash_t(q_aug, sk_ref, sv_ref, diag, t_col, expand_t)

    heads = []
    for kvh in range(KV_HEADS):
        for g in range(GROUP):
            h = GROUP * kvh + g
            cols = slice(g * tq, (g + 1) * tq)
            heads.append(gates_t[3 * h:3 * h + 1] * o_cmp_t[kvh][:, cols]
                         + gates_t[3 * h + 1:3 * h + 2] * o_sel_t[kvh][:, cols]
                         + gates_t[3 * h + 2:3 * h + 3] * o_win_t[kvh][:, cols])
    o_ref[0] = jnp.concatenate(heads, axis=0).T


def _nsa(qa, qar, gate, kc, vc_t, full, win, tq):
    b, t, _ = qa.shape
    l = full.shape[1]
    nc_pad = kc.shape[1]
    ns_pad = _round_up(l // SEL_BLOCK, LANES)
    span = _round_up(WINDOW + tq, LANES)
    map_t = jnp.asarray(_selection_map(l // CMP_STRIDE - 1, nc_pad, ns_pad).T)
    qspec = pl.BlockSpec((1, tq, NSA_WIDTH), lambda bi, i: (bi, i, 0))
    seq = lambda a: pl.BlockSpec((1,) + a.shape[1:], lambda bi, i: (bi, 0, 0))
    return pl.pallas_call(
        functools.partial(_nsa_kernel, tq=tq, span=span),
        grid=(b, t // tq),
        in_specs=[
            qspec, qspec,
            pl.BlockSpec((1, tq, LANES), lambda bi, i: (bi, i, 0)),
            seq(kc), seq(vc_t),
            pl.BlockSpec((1, l, KV_WIDTH), lambda bi, i: (bi, 0, 2)),
            pl.BlockSpec((1, l, KV_WIDTH), lambda bi, i: (bi, 0, 3)),
            seq(win),
            _resident(map_t),
        ],
        out_specs=qspec,
        out_shape=jax.ShapeDtypeStruct((b, t, NSA_WIDTH), F32),
        compiler_params=_params(("parallel", "arbitrary")),
        name="nsa",
    )(qa, qar, gate, kc, vc_t, full, full, win, map_t)


def _moba_kernel(qb_ref, km_ref, mk_ref, mv_ref, o_ref, *, tq):
    i = pl.program_id(1)
    t0 = i * tq
    r = GROUP * tq
    nb_pad = km_ref.shape[1]
    t_col = t0 + (lax.broadcasted_iota(jnp.int32, (1, r), 1) & (tq - 1))
    bt_t = t_col >> MOBA_SHIFT
    nblk_t = lax.broadcasted_iota(jnp.int32, (nb_pad, 1), 0)
    nblk_lane = lax.broadcasted_iota(jnp.int32, (1, nb_pad), 1)
    diag = t0 // KEY_TILE
    qb_t = qb_ref[0].T
    km = km_ref[0]

    def expand_t(j):
        return jnp.broadcast_to(nblk_lane == j, (KEY_TILE, nb_pad)).astype(BF16)

    q_aug = []
    for kvh in range(KV_HEADS):
        q_t = _head_cols(qb_t, kvh)
        gate_t = jnp.dot(km, q_t, preferred_element_type=F32, precision=lax.Precision.HIGHEST)
        gate_t = jnp.where(nblk_t < bt_t, gate_t, NEG_INF)
        chosen_t = jnp.where(nblk_t == bt_t, 1.0, _topk_mask(gate_t, MOBA_TOPK, 0))
        q_aug.append(jnp.concatenate([(q_t * (SCALE * LOG2E)).astype(BF16), _block_bias(chosen_t)], axis=0))

    o_t = _flash_t(q_aug, mk_ref, mv_ref, diag, t_col, expand_t)
    heads = [o_t[kvh][:, g * tq:(g + 1) * tq] for kvh in range(KV_HEADS) for g in range(GROUP)]
    o_ref[0] = jnp.concatenate(heads, axis=0).T


def _moba(qb, kmean, full, tq):
    b, t, _ = qb.shape
    l = full.shape[1]
    qspec = pl.BlockSpec((1, tq, MOBA_WIDTH), lambda bi, i: (bi, i, 0))
    return pl.pallas_call(
        functools.partial(_moba_kernel, tq=tq),
        grid=(b, t // tq),
        in_specs=[
            qspec,
            pl.BlockSpec((1,) + kmean.shape[1:], lambda bi, i: (bi, 0, 0)),
            pl.BlockSpec((1, l, KV_WIDTH), lambda bi, i: (bi, 0, 4)),
            pl.BlockSpec((1, l, KV_WIDTH), lambda bi, i: (bi, 0, 5)),
        ],
        out_specs=qspec,
        out_shape=jax.ShapeDtypeStruct((b, t, MOBA_WIDTH), F32),
        compiler_params=_params(("parallel", "arbitrary")),
        name="moba",
    )(qb, kmean, full, full)


def _softmax_pv_t(s, vt):
    e = jnp.exp(s - jnp.max(s, axis=-1, keepdims=True))
    return _dot_nt(e.astype(BF16), vt) / jnp.sum(e, axis=-1, keepdims=True)


def _segment_bias(bias_rows, e_ref, block, n_keys):
    rows, seg = e_ref.shape
    parts = []
    for s in range(-(-n_keys // seg)):
        n = min(seg, n_keys - s * seg)
        b0 = s * (seg // block)
        parts.append(_dot(bias_rows[:, b0:b0 + rows], e_ref[:, 0:n]))
    return jnp.concatenate(parts, axis=1)


def _block_means_t(kt, n_keys, n_out):
    lane = lax.broadcasted_iota(jnp.int32, (1, n_out), 1)
    out = jnp.zeros((kt.shape[0], n_out), F32)
    for n in range(n_keys // MOBA_BLOCK):
        blk = kt[:, n * MOBA_BLOCK:(n + 1) * MOBA_BLOCK].astype(F32)
        out = jnp.where(lane == n, jnp.sum(blk, axis=1, keepdims=True) * (1.0 / MOBA_BLOCK), out)
    return out


def _sample_kernel(pt_ref, *refs, past_len, ts, span):
    pages = refs[:PAGES_PER_STEP]
    (new_ref, qa_ref, qar_ref, gate_ref, qb_ref, win_ref, map_ref, esel_ref, emoba_ref,
     posk_ref, w1k_ref, w2k_ref, posv_ref, w1v_ref, w2v_ref, oa_ref, ob_ref, tok_scr, kv_scr) = refs[PAGES_PER_STEP:]
    del pt_ref
    step = pl.program_id(1)
    n_steps = pl.num_programs(1)
    page = pages[0].shape[2]
    cmp_rows = 2 * KV_WIDTH

    m = past_len // CMP_STRIDE
    pitch = tok_scr.shape[1] // CMP_STRIDE
    chunks_per_page = page // CMP_STRIDE
    for j in range(PAGES_PER_STEP):
        chunk0 = (step * PAGES_PER_STEP + j) * chunks_per_page
        for slot in range(2):
            x_t = pages[j][0, slot * KV_WIDTH:(slot + 1) * KV_WIDTH, :].T
            for k in range(chunks_per_page):
                tok_scr[slot, pl.ds(chunk0 + k, CMP_STRIDE, stride=pitch), :] = (
                    x_t[k * CMP_STRIDE:(k + 1) * CMP_STRIDE])
    for k in range(past_len // (PAGES_PER_STEP * page)):
        @pl.when(step == k)
        def _(k=k):
            for j in range(PAGES_PER_STEP):
                c0 = (k * PAGES_PER_STEP + j) * page
                kv_scr[:, c0:c0 + page] = pages[j][0, cmp_rows:, :].astype(BF16)

    @pl.when(step == n_steps - 1)
    def _():
        lc = kv_scr.shape[1]
        r = GROUP * ts
        new_pad = jnp.concatenate([new_ref[0], jnp.zeros((page - ts, PAGED_WIDTH), F32)], axis=0)
        kv_scr[:, past_len:past_len + page] = new_pad.T[cmp_rows:, :].astype(BF16)

        kc_heads = _compress_rows(lambda l: tok_scr[0, l * pitch:l * pitch + m, :], posk_ref, w1k_ref, w2k_ref, m)
        vc_heads = _compress_rows(lambda l: tok_scr[1, l * pitch:l * pitch + m, :], posv_ref, w1v_ref, w2v_ref, m)

        ns_pad = map_ref.shape[1]
        nb_pad = LANES
        row = lax.broadcasted_iota(jnp.int32, (r, 1), 0)
        t_row = past_len + (row & (ts - 1))
        t_tok = past_len + lax.broadcasted_iota(jnp.int32, (ts, 1), 0)
        causal = lax.broadcasted_iota(jnp.int32, (1, lc), 1) <= t_row
        c_end = lax.broadcasted_iota(jnp.int32, (1, m), 1) * CMP_STRIDE + CMP_LEN
        blk = lax.broadcasted_iota(jnp.int32, (1, ns_pad), 1)
        nblk = lax.broadcasted_iota(jnp.int32, (1, nb_pad), 1)
        jt = t_tok >> SEL_SHIFT
        bt = t_row >> MOBA_SHIFT
        wpos = past_len - WINDOW + lax.broadcasted_iota(jnp.int32, (1, span), 1)
        w_valid = (wpos >= 0) & (wpos <= t_row) & (wpos >= t_row - WINDOW)
        gates = gate_ref[0]

        def kv_rows(slot, kvh):
            lo = slot * KV_WIDTH + kvh * HEAD_DIM
            return kv_scr[lo:lo + HEAD_DIM, :]

        o_cmp, scores, gate_s, qf = [], [], [], []
        for kvh in range(KV_HEADS):
            q = (_stack_heads(qa_ref, kvh) * SCALE).astype(BF16)
            p_cmp = _masked_softmax(_dot_nt(q, kc_heads[kvh].astype(BF16)), c_end <= t_row + 1)
            o_cmp.append(_dot(p_cmp.astype(BF16), vc_heads[kvh].astype(BF16)))
            p_sum = p_cmp[0:ts]
            for g in range(1, GROUP):
                p_sum = p_sum + p_cmp[g * ts:(g + 1) * ts]
            imp = jnp.dot(p_sum, map_ref[...], preferred_element_type=F32, precision=lax.Precision.HIGHEST)
            scores.append(jnp.where((blk == jt) | (blk == 0), FORCE_SCORE, jnp.where(blk < jt, imp, NEG_INF)))
            qf.append(_stack_heads(qb_ref, kvh))
            g_s = jnp.dot(qf[kvh], _block_means_t(kv_rows(2, kvh), past_len, nb_pad), preferred_element_type=F32,
                          precision=lax.Precision.HIGHEST)
            gate_s.append(jnp.where(nblk < bt, g_s, NEG_INF))
        sel = _topk_mask(jnp.concatenate(scores, axis=0), SEL_TOPK, 1)
        chosen = _topk_mask(jnp.concatenate(gate_s, axis=0), MOBA_TOPK, 1)

        for kvh in range(KV_HEADS):
            lo = kvh * HEAD_DIM
            qr = (_stack_heads(qar_ref, kvh) * SCALE).astype(BF16)
            bias_rows = _block_bias(jnp.concatenate([sel[kvh * ts:(kvh + 1) * ts]] * GROUP, axis=0))
            s = _dot(qr, kv_rows(0, kvh)) + _segment_bias(bias_rows, esel_ref, SEL_BLOCK, lc)
            o_sel = _softmax_pv_t(jnp.where(causal, s, NEG_INF), kv_rows(1, kvh))
            o_win = _window_branch(qr, win_ref, 0, span, lo, w_valid)
            _write_heads(oa_ref, kvh, ts, lambda g, h, rows: (
                gates[:, 3 * h:3 * h + 1] * o_cmp[kvh][rows] + gates[:, 3 * h + 1:3 * h + 2] * o_sel[rows]
                + gates[:, 3 * h + 2:3 * h + 3] * o_win[rows]))

            own = jnp.where(nblk == bt, 1.0, chosen[kvh * r:(kvh + 1) * r])
            s = (_dot((qf[kvh] * SCALE).astype(BF16), kv_rows(2, kvh))
                 + _segment_bias(_block_bias(own), emoba_ref, MOBA_BLOCK, lc))
            o = _softmax_pv_t(jnp.where(causal, s, NEG_INF), kv_rows(3, kvh))
            _write_heads(ob_ref, kvh, ts, lambda g, h, rows: o[rows])


def _block_expansion(n_rows, block, n_keys):
    return jnp.asarray(np.arange(n_rows)[:, None] == (np.arange(n_keys)[None, :] // block), dtype=BF16)


def _sample_attn(pool_t, page_table, page_base, new_rows, qa, qar, gate, qb, win, cmp_k, cmp_v):
    bs, n_pages = page_table.shape
    page = pool_t.shape[2]
    ts = new_rows.shape[1]
    past_len = n_pages * page
    n_steps = n_pages // PAGES_PER_STEP
    assert page == LANES and n_pages % PAGES_PER_STEP == 0 and ts < CMP_STRIDE and ts & (ts - 1) == 0
    assert past_len % MOBA_BLOCK == 0 and (past_len // CMP_STRIDE) % LANES == 0
    lc = past_len + page
    ns_pad = _round_up(lc // SEL_BLOCK, LANES)
    assert lc // MOBA_BLOCK + 1 <= LANES and past_len % BIAS_SEG == 0
    span = win.shape[1]
    sel_map = jnp.asarray(_selection_map(past_len // CMP_STRIDE - 1, past_len // CMP_STRIDE, ns_pad))
    e_sel = _block_expansion(BIAS_SEG // SEL_BLOCK, SEL_BLOCK, BIAS_SEG)
    e_moba = _block_expansion(2 * (BIAS_SEG // MOBA_BLOCK), MOBA_BLOCK, BIAS_SEG)

    def page_spec(j):
        return pl.BlockSpec((1, PAGED_WIDTH, page),
                            lambda b, s, pt: (page_base + pt[b, s * PAGES_PER_STEP + j], 0, 0))

    per_seq = lambda a: pl.BlockSpec((1,) + a.shape[1:], lambda b, s, pt: (b, 0, 0))
    consts = (sel_map, e_sel, e_moba) + tuple(cmp_k) + tuple(cmp_v)
    out_spec = pl.BlockSpec((1, ts, NSA_WIDTH), lambda b, s, pt: (b, 0, 0))
    grid_spec = pltpu.PrefetchScalarGridSpec(
        num_scalar_prefetch=1,
        grid=(bs, n_steps),
        in_specs=[page_spec(j) for j in range(PAGES_PER_STEP)]
        + [per_seq(a) for a in (new_rows, qa, qar, gate, qb, win)] + [_resident(a) for a in consts],
        out_specs=[out_spec, out_spec],
        scratch_shapes=[pltpu.VMEM((2, CMP_STRIDE * (past_len // CMP_STRIDE + 8), KV_WIDTH), F32),
                        pltpu.VMEM((PAGED_WIDTH - 2 * KV_WIDTH, lc), BF16)],
    )
    return pl.pallas_call(
        functools.partial(_sample_kernel, past_len=past_len, ts=ts, span=span),
        grid_spec=grid_spec,
        out_shape=[jax.ShapeDtypeStruct((bs, ts, NSA_WIDTH), F32)] * 2,
        compiler_params=_params(("parallel", "arbitrary")),
        name="sample_attn",
    )(page_table, *([pool_t] * PAGES_PER_STEP), new_rows, qa, qar, gate, qb, win, *consts)


def _out_kernel(x_ref, oa_ref, ob_ref, za_ref, zb_ref, wout_ref, gple_ref, wg_ref, ple_ref, wp_ref, gfin_ref,
                o_ref, *, final):
    mixed = jnp.concatenate([oa_ref[...] * za_ref[...], ob_ref[...] * zb_ref[...]], axis=-1).astype(BF16)
    x1 = x_ref[...] + _dot(mixed, wout_ref[...])
    gate = _sigmoid(_dot(_rms(x1, gple_ref[...]).astype(BF16), wg_ref[...]))
    x2 = x1 + gate * _dot(ple_ref[...].astype(BF16), wp_ref[...])
    o_ref[...] = _rms(x2, gfin_ref[...]) if final else x2


def _out(x2d, oa, ob, za, zb, w_out, g_ple, w_gate, ple, w_proj, g_final, final):
    n, d_model = x2d.shape
    tm = min(256, n)
    row = lambda a: pl.BlockSpec((tm, a.shape[1]), lambda i: (i, 0))
    args = (x2d, oa, ob, za, zb, w_out, g_ple, w_gate, ple, w_proj, g_final)
    specs = [row(x2d), row(oa), row(ob), row(za), row(zb), _resident(w_out), _resident(g_ple), _resident(w_gate),
             row(ple), _resident(w_proj), _resident(g_final)]
    return pl.pallas_call(
        functools.partial(_out_kernel, final=final),
        grid=(n // tm,),
        in_specs=specs,
        out_specs=row(x2d),
        out_shape=jax.ShapeDtypeStruct((n, d_model), F32),
        compiler_params=_params(("parallel",)),
        name="out",
    )(*args)


def _mixer_layer(x, ple, pos, past, win_prev, weights, g_final, final):
    (g_mix, w_split, w_out, cmp_k, cmp_v, g_ple, w_gate, w_proj) = weights
    b, t, d_model = x.shape
    n = b * t
    qa, qar, paged, win_new, gate, za, qb, zb, *transposed = _proj(x.reshape(n, d_model), pos, g_mix, w_split)
    paged3 = paged.reshape(b, t, PAGED_WIDTH)
    win_new3 = win_new.reshape(b, t, 2 * KV_WIDTH)
    tq = min(KEY_TILE, t)
    assert KEY_TILE % tq == 0 and tq & (tq - 1) == 0
    span = _round_up(WINDOW + tq, LANES)
    win_rows = max(WINDOW + t, t - tq + span)
    win = jnp.concatenate([win_prev, win_new3, jnp.zeros((b, win_rows - WINDOW - t, 2 * KV_WIDTH), F32)], axis=1)
    qa3, qar3, gate3, qb3 = (a.reshape(b, t, -1) for a in (qa, qar, gate, qb))
    if past is None:
        assert t % KEY_TILE == 0
        nc_pad = _round_up(t // CMP_STRIDE, LANES)
        kc = _compress(paged3, 0, *cmp_k, nc_pad, False)
        vc_t = _compress(paged3, 1, *cmp_v, nc_pad, True)
        o_a = _nsa(qa3, qar3, gate3, kc, vc_t, paged3, win, tq)
        o_b = _moba(qb3, _kmean(paged3, LANES), paged3, tq)
    else:
        o_a, o_b = _sample_attn(*past, paged3, qa3, qar3, gate3, qb3, win, cmp_k, cmp_v)
    x_next = _out(x.reshape(n, d_model), o_a.reshape(n, -1), o_b.reshape(n, -1), za, zb, w_out, g_ple, w_gate,
                  ple.reshape(n, -1), w_proj, g_final, final)
    if transposed:
        new_paged, new_win = (a.reshape(b, -1, KV_HEADS, HEAD_DIM, t).transpose(0, 4, 1, 2, 3) for a in transposed)
    else:
        new_paged, new_win = (a.reshape(b, t, -1, KV_HEADS, HEAD_DIM) for a in (paged3, win_new3))
    return x_next.reshape(b, t, d_model), new_paged, new_win


def _split_w_in(w_in):
    sizes = (NSA_WIDTH, 6 * KV_WIDTH, 3 * NSA_HEADS, NSA_WIDTH, MOBA_WIDTH, 2 * KV_WIDTH, MOBA_WIDTH)
    offs = np.concatenate([[0], np.cumsum(sizes)])
    wqa, wkva, wgate, wza, wqb, wkvb, wzb = [w_in[:, offs[k]:offs[k + 1]].astype(BF16) for k in range(len(sizes))]
    wgate = jnp.pad(wgate, ((0, 0), (0, LANES - wgate.shape[1])))
    return wqa, wkva, wgate, wza, wqb, wkvb, wzb


def kernel(x_prompt, x_sample, cache_paged_kv, cache_win_kv, page_table, p_prompt, p_sample, g_mix, w_in, w_out,
           cmp_pos_k, cmp_w1_k, cmp_w2_k, cmp_pos_v, cmp_w1_v, cmp_w2_v, g_ple, w_ple_gate, w_ple_proj, g_final):
    depth = w_in.shape[0]
    bp, tp, _ = x_prompt.shape
    bs, ts, _ = x_sample.shape
    n_pool, page = cache_paged_kv.shape[1], cache_paged_kv.shape[2]
    past_len = page_table.shape[1] * page
    wbuf = cache_win_kv.shape[2]
    pos_p = jnp.arange(tp, dtype=jnp.int32)
    pos_s = past_len + jnp.arange(ts, dtype=jnp.int32)
    pool_t = cache_paged_kv.transpose(0, 1, 3, 4, 5, 2).reshape(depth * n_pool, PAGED_WIDTH, page)
    g_fin = g_final.reshape(1, -1)
    xp, xs = x_prompt, x_sample
    new_pp, new_pw, new_sp, new_sw = [], [], [], []
    for i in range(depth):
        final = i == depth - 1
        weights = (g_mix[i].reshape(1, -1), _split_w_in(w_in[i]), w_out[i].astype(BF16),
                   _compress_weights(cmp_pos_k[i], cmp_w1_k[i], cmp_w2_k[i]),
                   _compress_weights(cmp_pos_v[i], cmp_w1_v[i], cmp_w2_v[i]),
                   g_ple[i].reshape(1, -1), w_ple_gate[i].astype(BF16), w_ple_proj[i].astype(BF16))
        win0 = jnp.zeros((bp, WINDOW, 2 * KV_WIDTH), F32)
        xp, pk, pw = _mixer_layer(xp, p_prompt[i], pos_p, None, win0, weights, g_fin, final)
        win_prev = jnp.concatenate([jnp.zeros((bs, WINDOW - wbuf, 2 * KV_WIDTH), F32),
                                    cache_win_kv[i].reshape(bs, wbuf, 2 * KV_WIDTH)], axis=1)
        xs, sk, sw = _mixer_layer(xs, p_sample[i], pos_s, (pool_t, page_table, i * n_pool), win_prev, weights,
                                  g_fin, final)
        new_pp.append(pk)
        new_pw.append(pw[:, tp - min(WINDOW, tp):])
        new_sp.append(sk)
        new_sw.append(sw)
    return (xp, xs, jnp.stack(new_pp), jnp.stack(new_pw), jnp.stack(new_sp), jnp.stack(new_sw))
```

```python
import functools

import numpy as np
import jax
import jax.numpy as jnp
from jax import lax
from jax.experimental import pallas as pl
from jax.experimental.pallas import tpu as pltpu

HEAD_DIM = 64
NSA_HEADS = 8
MOBA_HEADS = 8
KV_HEADS = 2
GROUP = NSA_HEADS // KV_HEADS
ROT_DIM = HEAD_DIM // 4
ROPE_THETA = 500000.0
CMP_LEN = 32
CMP_STRIDE = 16
SEL_BLOCK = 64
SEL_TOPK = 8
WINDOW = 512
MOBA_BLOCK = 256
MOBA_TOPK = 3
N_PAGED_SLOTS = 6
RMS_EPS = 1e-6
NEG_INF = -1e30
FORCE_SCORE = 1e4
SCALE = HEAD_DIM ** -0.5
LOG2E = 1.4426950408889634
SEL_SHIFT = SEL_BLOCK.bit_length() - 1
MOBA_SHIFT = MOBA_BLOCK.bit_length() - 1

LANES = 128
KV_WIDTH = KV_HEADS * HEAD_DIM
NSA_WIDTH = NSA_HEADS * HEAD_DIM
MOBA_WIDTH = MOBA_HEADS * HEAD_DIM
PAGED_WIDTH = N_PAGED_SLOTS * KV_WIDTH
KEY_TILE = 256
PAGES_PER_STEP = 8
BIAS_SEG = 2048
VMEM_LIMIT = 56 * 1024 * 1024

F32 = jnp.float32
BF16 = jnp.bfloat16
NT_DIMS = (((1,), (1,)), ((), ()))


def _round_up(n, m):
    return -(-n // m) * m


def _params(sem):
    return pltpu.CompilerParams(dimension_semantics=sem, vmem_limit_bytes=VMEM_LIMIT)


def _rms(x, g):
    return x * lax.rsqrt(jnp.mean(x * x, axis=-1, keepdims=True) + RMS_EPS) * g


def _sigmoid(x):
    return 1.0 / (1.0 + jnp.exp(-x))


def _dot(a, b):
    return jnp.dot(a, b, preferred_element_type=F32)


def _dot_nt(a, b):
    return lax.dot_general(a, b, NT_DIMS, preferred_element_type=F32)


def _dot_nt_exact(a, b):
    return lax.dot_general(a, b, NT_DIMS, preferred_element_type=F32, precision=lax.Precision.HIGHEST)


def _resident(a):
    return pl.BlockSpec(a.shape, lambda *_: (0,) * a.ndim, pipeline_mode=pl.Buffered(1))


def _rope128(v, c, sp, sm):
    half = ROT_DIM // 2
    return v * c + pltpu.roll(v, half, axis=1) * sp + pltpu.roll(v, LANES - half, axis=1) * sm


def _rope(v, c, sp, sm):
    n = v.shape[-1] // LANES
    return jnp.concatenate([_rope128(v[:, i * LANES:(i + 1) * LANES], c, sp, sm) for i in range(n)], axis=-1)


def _proj_kernel(x_ref, g_ref, c_ref, sp_ref, sm_ref, wqa, wkva, wgate, wza, wqb, wkvb, wzb,
                 qa_o, qar_o, paged_o, win_o, gate_o, za_o, qb_o, zb_o, *transposed_o):
    h = _rms(x_ref[...], g_ref[...]).astype(BF16)
    c, sp, sm = c_ref[...], sp_ref[...], sm_ref[...]
    qa = _dot(h, wqa[...])
    qa_o[...] = qa
    qar_o[...] = _rope(qa, c, sp, sm)
    kva = _dot(h, wkva[...])
    paged_o[:, 0:2 * KV_WIDTH] = kva[:, 0:2 * KV_WIDTH]
    paged_o[:, 2 * KV_WIDTH:3 * KV_WIDTH] = _rope(kva[:, 2 * KV_WIDTH:3 * KV_WIDTH], c, sp, sm)
    paged_o[:, 3 * KV_WIDTH:4 * KV_WIDTH] = kva[:, 3 * KV_WIDTH:4 * KV_WIDTH]
    win_o[:, 0:KV_WIDTH] = _rope(kva[:, 4 * KV_WIDTH:5 * KV_WIDTH], c, sp, sm)
    win_o[:, KV_WIDTH:2 * KV_WIDTH] = kva[:, 5 * KV_WIDTH:6 * KV_WIDTH]
    kvb = _dot(h, wkvb[...])
    paged_o[:, 4 * KV_WIDTH:5 * KV_WIDTH] = _rope(kvb[:, 0:KV_WIDTH], c, sp, sm)
    paged_o[:, 5 * KV_WIDTH:6 * KV_WIDTH] = kvb[:, KV_WIDTH:2 * KV_WIDTH]
    gate_o[...] = _sigmoid(_dot(h, wgate[...]))
    za = _dot(h, wza[...])
    za_o[...] = za * _sigmoid(za)
    qb_o[...] = _rope(_dot(h, wqb[...]), c, sp, sm)
    zb = _dot(h, wzb[...])
    zb_o[...] = zb * _sigmoid(zb)
    if transposed_o:
        paged_t_o, win_t_o = transposed_o
        paged_t_o[0] = paged_o[...].T
        win_t_o[0] = win_o[...].T


def _rope_tables(pos, n_rows):
    half = ROT_DIM // 2
    inv_freq = ROPE_THETA ** (-jnp.arange(half, dtype=F32) / half)
    ang = pos.astype(F32)[:, None] * inv_freq[None, :]
    cos, sin = jnp.cos(ang), jnp.sin(ang)
    d = np.arange(LANES) % HEAD_DIM
    idx = d % half
    c = jnp.where(d < ROT_DIM, cos[:, idx], 1.0)
    sp = jnp.where((d >= half) & (d < ROT_DIM), sin[:, idx], 0.0)
    sm = jnp.where(d < half, -sin[:, idx], 0.0)
    rep = n_rows // pos.shape[0]
    return tuple(jnp.tile(t, (rep, 1)) for t in (c, sp, sm))


def _proj(x2d, pos, g, w):
    n, d_model = x2d.shape
    t = pos.shape[0]
    tm = min(256, n)
    tab_rows = max(t, tm)
    tabs = _rope_tables(pos, tab_rows)
    n_tab = tab_rows // tm
    row = lambda width: pl.BlockSpec((tm, width), lambda i: (i, 0))
    tab = pl.BlockSpec((tm, LANES), lambda i: (i % n_tab, 0))
    widths = (NSA_WIDTH, NSA_WIDTH, PAGED_WIDTH, 2 * KV_WIDTH, LANES, NSA_WIDTH, MOBA_WIDTH, MOBA_WIDTH)
    out_specs = [row(wd) for wd in widths]
    out_shape = [jax.ShapeDtypeStruct((n, wd), F32) for wd in widths]
    if t % tm == 0:
        tiles = t // tm
        for wd in (PAGED_WIDTH, 2 * KV_WIDTH):
            out_specs.append(pl.BlockSpec((1, wd, tm), lambda i: (i // tiles, 0, i % tiles)))
            out_shape.append(jax.ShapeDtypeStruct((n // t, wd, t), F32))
    return pl.pallas_call(
        _proj_kernel,
        grid=(n // tm,),
        in_specs=[row(d_model), _resident(g), tab, tab, tab] + [_resident(a) for a in w],
        out_specs=out_specs,
        out_shape=out_shape,
        compiler_params=_params(("parallel",)),
        name="proj",
    )(x2d, g, *tabs, *w)


def _compress_rows(chunk_row, pos_ref, w1_ref, w2_ref, m):
    assert KV_HEADS == 2 and KV_WIDTH == LANES
    hidden = w2_ref.shape[0]
    low = lax.broadcasted_iota(jnp.int32, (1, LANES), 1) < HEAD_DIM
    heads = [[], []]
    for l in range(0, CMP_STRIDE, 2):
        a, b = chunk_row(l), chunk_row(l + 1)
        heads[0].append(jnp.where(low, a, pltpu.roll(b, HEAD_DIM, axis=1)).astype(BF16))
        heads[1].append(jnp.where(low, pltpu.roll(a, HEAD_DIM, axis=1), b).astype(BF16))
    x = jnp.concatenate([jnp.concatenate(h, axis=1) for h in heads] + [pos_ref[...]], axis=0)
    ab = _dot(x, w1_ref[...])
    bias = ab[2 * m:2 * m + 1, 0:hidden] + ab[2 * m + 1:2 * m + 2, hidden:2 * hidden]
    hid = []
    for h in range(KV_HEADS):
        rows = slice(h * m, (h + 1) * m)
        nxt = pltpu.roll(ab[rows, hidden:2 * hidden], m - 1, axis=0)
        pre = ab[rows, 0:hidden] + nxt + bias
        hid.append((pre * _sigmoid(pre)).astype(BF16))
    out = _dot(jnp.concatenate(hid, axis=0), w2_ref[...])
    return [out[h * m:(h + 1) * m] for h in range(KV_HEADS)]


def _compress_kernel(src_ref, pos_ref, w1_ref, w2_ref, o_ref, *, n_chunks, nc_pad, transposed):
    m = n_chunks
    out = jnp.concatenate(
        _compress_rows(lambda l: src_ref[0, pl.ds(l, m, stride=CMP_STRIDE), :], pos_ref, w1_ref, w2_ref, m), axis=1)
    if nc_pad > m:
        out = jnp.concatenate([out, jnp.zeros((nc_pad - m, LANES), F32)], axis=0)
    o_ref[0] = out.T if transposed else out


def _compress(full, slot, pos_rows, w1, w2, nc_pad, transposed):
    b, l, _ = full.shape
    n_chunks = l // CMP_STRIDE
    out_dims = (KV_WIDTH, nc_pad) if transposed else (nc_pad, KV_WIDTH)
    return pl.pallas_call(
        functools.partial(_compress_kernel, n_chunks=n_chunks, nc_pad=nc_pad, transposed=transposed),
        grid=(b,),
        in_specs=[pl.BlockSpec((1, l, KV_WIDTH), lambda i: (i, 0, slot)),
                  _resident(pos_rows), _resident(w1), _resident(w2)],
        out_specs=pl.BlockSpec((1,) + out_dims, lambda i: (i, 0, 0)),
        out_shape=jax.ShapeDtypeStruct((b,) + out_dims, F32),
        compiler_params=_params(("parallel",)),
        name="compress",
    )(full, pos_rows, w1, w2)


def _compress_weights(pos_emb, w1, w2):
    hidden = w1.shape[-1]
    rows = CMP_STRIDE * HEAD_DIM
    w1cat = jnp.concatenate([w1[:CMP_STRIDE].reshape(rows, hidden), w1[CMP_STRIDE:].reshape(rows, hidden)], axis=1)
    pos_rows = jnp.zeros((16, rows), F32)
    pos_rows = pos_rows.at[0].set(pos_emb[:CMP_STRIDE].reshape(-1)).at[1].set(pos_emb[CMP_STRIDE:].reshape(-1))
    return pos_rows.astype(BF16), w1cat.astype(BF16), w2.astype(BF16)


def _kmean_kernel(src_ref, o_ref, *, n_blocks):
    x = src_ref[0].reshape(n_blocks, MOBA_BLOCK, KV_WIDTH)
    o_ref[0, 0:n_blocks, :] = jnp.sum(x, axis=1) * (1.0 / MOBA_BLOCK)
    o_ref[0, n_blocks:, :] = jnp.zeros((o_ref.shape[1] - n_blocks, KV_WIDTH), F32)


def _kmean(full, nb_pad):
    b, l, _ = full.shape
    return pl.pallas_call(
        functools.partial(_kmean_kernel, n_blocks=l // MOBA_BLOCK),
        grid=(b,),
        in_specs=[pl.BlockSpec((1, l, KV_WIDTH), lambda i: (i, 0, 4))],
        out_specs=pl.BlockSpec((1, nb_pad, KV_WIDTH), lambda i: (i, 0, 0)),
        out_shape=jax.ShapeDtypeStruct((b, nb_pad, KV_WIDTH), F32),
        compiler_params=_params(("parallel",)),
        name="kmean",
    )(full)


def _stack_heads(ref, kvh):
    return jnp.concatenate(
        [ref[0, :, (GROUP * kvh + g) * HEAD_DIM:(GROUP * kvh + g + 1) * HEAD_DIM] for g in range(GROUP)], axis=0)


def _masked_softmax(s, valid):
    s = jnp.where(valid, s, NEG_INF)
    e = jnp.where(valid, jnp.exp(s - jnp.max(s, axis=-1, keepdims=True)), 0.0)
    l = jnp.sum(e, axis=-1, keepdims=True)
    return e / jnp.where(l > 0.0, l, 1.0)


def _topk_mask(score, k, axis):
    n = score.shape[axis]
    idx = lax.broadcasted_iota(jnp.int32, score.shape, axis).astype(F32)
    sel = jnp.zeros(score.shape, F32)
    s = score
    for _ in range(k):
        m = jnp.max(s, axis=axis, keepdims=True)
        first = jnp.min(jnp.where(s == m, idx, float(n)), axis=axis, keepdims=True)
        pick = idx == first
        sel = jnp.where(pick & (m > 0.5 * NEG_INF), 1.0, sel)
        s = jnp.where(pick, -3e38, s)
    return sel


def _block_bias(chosen):
    return ((chosen - 1.0) * -NEG_INF).astype(BF16)


def _flash_t(q_aug, k_ref, v_ref, diag, t_col, expand_t):
    n_heads = len(q_aug)
    ones = jnp.ones((16, KEY_TILE), BF16)

    def scores(j, live):
        k0 = pl.multiple_of(j * KEY_TILE, KEY_TILE)
        kt = k_ref[0, pl.ds(k0, KEY_TILE), :].astype(BF16)
        lhs = jnp.concatenate([kt, expand_t(j, live)], axis=1)
        return [_dot(lhs, qa) for qa in q_aug]

    def values_t(j):
        k0 = pl.multiple_of(j * KEY_TILE, KEY_TILE)
        vt = v_ref[0, pl.ds(k0, KEY_TILE), :].T.astype(BF16)
        return [jnp.concatenate([vt[h * HEAD_DIM:(h + 1) * HEAD_DIM], ones], axis=0) for h in range(n_heads)]

    visible = diag * KEY_TILE + lax.broadcasted_iota(jnp.int32, (KEY_TILE, 1), 0) <= t_col
    init = []
    for s, vt in zip(scores(diag, True), values_t(diag)):
        s = jnp.where(visible, s, NEG_INF)
        m0 = jnp.max(s, axis=0, keepdims=True)
        init += [m0, _dot(vt, jnp.exp2(s - m0).astype(BF16))]

    def update(stats, s, vt):
        m_i, acc = stats
        m_new = jnp.maximum(m_i, jnp.max(s, axis=0, keepdims=True))
        return [m_new, jnp.exp2(m_i - m_new) * acc + _dot(vt, jnp.exp2(s - m_new).astype(BF16))]

    def body(i, carry):
        ja, jb = 2 * i, 2 * i + 1
        live = jb < diag
        jb = jnp.minimum(jb, diag)
        sa, va = scores(ja, True), values_t(ja)
        sb, vb = scores(jb, live), values_t(jb)
        out = []
        for h in range(n_heads):
            out += update(update(carry[2 * h:2 * h + 2], sa[h], va[h]), sb[h], vb[h])
        return tuple(out)

    res = lax.fori_loop(0, (diag + 1) // 2, body, tuple(init))
    return [res[2 * h + 1][0:HEAD_DIM] / res[2 * h + 1][HEAD_DIM:HEAD_DIM + 1] for h in range(n_heads)]


def _head_cols(x_t, kvh):
    x = jnp.concatenate([x_t[(GROUP * kvh + g) * HEAD_DIM:(GROUP * kvh + g + 1) * HEAD_DIM] for g in range(GROUP)],
                        axis=1)
    return jnp.concatenate([x if k == kvh else jnp.zeros_like(x) for k in range(KV_HEADS)], axis=0)


def _selection_map(n_cmp, nc_pad, ns_pad):
    c0 = np.arange(nc_pad)[:, None] * CMP_STRIDE
    j0 = np.arange(ns_pad)[None, :] * SEL_BLOCK
    ov = np.clip(np.minimum(c0 + CMP_LEN, j0 + SEL_BLOCK) - np.maximum(c0, j0), 0, None) / CMP_STRIDE
    return np.where(np.arange(nc_pad)[:, None] < n_cmp, ov, 0.0).astype(np.float32)


def _window_branch(qr, win_ref, w0, span, lo, w_valid):
    wk = win_ref[0, pl.ds(w0, span), lo:lo + HEAD_DIM].astype(BF16)
    wv = win_ref[0, pl.ds(w0, span), KV_WIDTH + lo:KV_WIDTH + lo + HEAD_DIM].astype(BF16)
    s = jnp.where(w_valid, _dot_nt(qr, wk), NEG_INF)
    e = jnp.exp(s - jnp.max(s, axis=-1, keepdims=True))
    return _dot(e.astype(BF16), wv) / jnp.sum(e, axis=-1, keepdims=True)


def _write_heads(o_ref, kvh, tq, head_out):
    for g in range(GROUP):
        h = GROUP * kvh + g
        o_ref[0, :, h * HEAD_DIM:(h + 1) * HEAD_DIM] = head_out(g, h, slice(g * tq, (g + 1) * tq))


def _nsa_kernel(qa_ref, qar_ref, gate_ref, kc_ref, vct_ref, sk_ref, sv_ref, win_ref, mapt_ref, o_ref, *, tq, span):
    i = pl.program_id(1)
    t0 = i * tq
    r = GROUP * tq
    nc_pad = kc_ref.shape[1]
    ns_pad = mapt_ref.shape[0]
    t_col = t0 + (lax.broadcasted_iota(jnp.int32, (1, r), 1) & (tq - 1))
    c_end = lax.broadcasted_iota(jnp.int32, (nc_pad, 1), 0) * CMP_STRIDE + CMP_LEN
    blk_t = lax.broadcasted_iota(jnp.int32, (ns_pad, 1), 0)
    blk_lane = lax.broadcasted_iota(jnp.int32, (1, ns_pad), 1)
    key_blk = lax.broadcasted_iota(jnp.int32, (KEY_TILE, 1), 0) >> SEL_SHIFT
    jt_t = (t0 + lax.broadcasted_iota(jnp.int32, (1, tq), 1)) >> SEL_SHIFT
    diag = t0 // KEY_TILE
    w0 = pl.multiple_of(i * tq, 8)
    wpos = t0 - WINDOW + lax.broadcasted_iota(jnp.int32, (span, 1), 0)
    w_valid = (wpos >= 0) & (wpos <= t_col) & (wpos >= t_col - WINDOW)
    c_valid = c_end <= t_col + 1

    qa_t = qa_ref[0].T
    qar_t = qar_ref[0].T
    gates_t = gate_ref[0].T
    kc = kc_ref[0].astype(BF16)
    wk = win_ref[0, pl.ds(w0, span), 0:KV_WIDTH].astype(BF16)
    wv_t = win_ref[0, pl.ds(w0, span), KV_WIDTH:2 * KV_WIDTH].T.astype(BF16)

    def expand_t(j, live):
        return (jnp.where(live, key_blk + j * (KEY_TILE // SEL_BLOCK), ns_pad - 1) == blk_lane).astype(BF16)

    o_cmp_t, o_win_t, qr_t, score_t = [], [], [], []
    for kvh in range(KV_HEADS):
        rows = slice(kvh * HEAD_DIM, (kvh + 1) * HEAD_DIM)
        q_t = (_head_cols(qa_t, kvh) * SCALE).astype(BF16)
        s = jnp.where(c_valid, _dot(kc, q_t), NEG_INF)
        e = jnp.where(c_valid, jnp.exp(s - jnp.max(s, axis=0, keepdims=True)), 0.0)
        l = jnp.sum(e, axis=0, keepdims=True)
        p_cmp = e / jnp.where(l > 0.0, l, 1.0)
        o_cmp_t.append(_dot(vct_ref[0, rows, :].astype(BF16), p_cmp.astype(BF16)))

        p_sum = p_cmp[:, 0:tq]
        for g in range(1, GROUP):
            p_sum = p_sum + p_cmp[:, g * tq:(g + 1) * tq]
        imp_t = jnp.dot(mapt_ref[...], p_sum, preferred_element_type=F32, precision=lax.Precision.HIGHEST)
        score_t.append(jnp.where((blk_t == jt_t) | (blk_t == 0), FORCE_SCORE,
                                 jnp.where(blk_t < jt_t, imp_t, NEG_INF)))

        qr_t.append((_head_cols(qar_t, kvh) * (SCALE * LOG2E)).astype(BF16))
        s = jnp.where(w_valid, _dot(wk, qr_t[kvh]), NEG_INF)
        e = jnp.exp2(s - jnp.max(s, axis=0, keepdims=True))
        o_win_t.append(_dot(wv_t[rows], e.astype(BF16)) / jnp.sum(e, axis=0, keepdims=True))

    bias_t = _block_bias(_topk_mask(jnp.concatenate(score_t, axis=1), SEL_TOPK, 0))
    q_aug = [jnp.concatenate([qr_t[kvh], jnp.concatenate([bias_t[:, kvh * tq:(kvh + 1) * tq]] * GROUP, axis=1)],
                             axis=0) for kvh in range(KV_HEADS)]

    o_sel_t = _flash_t(q_aug, sk_ref, sv_ref, diag, t_col, expand_t)

    heads = []
    for kvh in range(KV_HEADS):
        for g in range(GROUP):
            h = GROUP * kvh + g
            cols = slice(g * tq, (g + 1) * tq)
            heads.append(gates_t[3 * h:3 * h + 1] * o_cmp_t[kvh][:, cols]
                         + gates_t[3 * h + 1:3 * h + 2] * o_sel_t[kvh][:, cols]
                         + gates_t[3 * h + 2:3 * h + 3] * o_win_t[kvh][:, cols])
    o_ref[0] = jnp.concatenate(heads, axis=0).T


def _nsa(qa, qar, gate, kc, vc_t, full, win, tq):
    b, t, _ = qa.shape
    l = full.shape[1]
    nc_pad = kc.shape[1]
    ns_pad = _round_up(l // SEL_BLOCK + 1, LANES)
    span = _round_up(WINDOW + tq, LANES)
    map_t = jnp.asarray(_selection_map(l // CMP_STRIDE - 1, nc_pad, ns_pad).T)
    qspec = pl.BlockSpec((1, tq, NSA_WIDTH), lambda bi, i: (bi, i, 0))
    seq = lambda a: pl.BlockSpec((1,) + a.shape[1:], lambda bi, i: (bi, 0, 0))
    return pl.pallas_call(
        functools.partial(_nsa_kernel, tq=tq, span=span),
        grid=(b, t // tq),
        in_specs=[
            qspec, qspec,
            pl.BlockSpec((1, tq, LANES), lambda bi, i: (bi, i, 0)),
            seq(kc), seq(vc_t),
            pl.BlockSpec((1, l, KV_WIDTH), lambda bi, i: (bi, 0, 2)),
            pl.BlockSpec((1, l, KV_WIDTH), lambda bi, i: (bi, 0, 3)),
            seq(win),
            _resident(map_t),
        ],
        out_specs=qspec,
        out_shape=jax.ShapeDtypeStruct((b, t, NSA_WIDTH), F32),
        compiler_params=_params(("parallel", "arbitrary")),
        name="nsa",
    )(qa, qar, gate, kc, vc_t, full, full, win, map_t)


def _moba_kernel(qb_ref, km_ref, mk_ref, mv_ref, o_ref, *, tq):
    i = pl.program_id(1)
    t0 = i * tq
    r = GROUP * tq
    nb_pad = km_ref.shape[1]
    t_col = t0 + (lax.broadcasted_iota(jnp.int32, (1, r), 1) & (tq - 1))
    bt_t = t_col >> MOBA_SHIFT
    nblk_t = lax.broadcasted_iota(jnp.int32, (nb_pad, 1), 0)
    nblk_lane = lax.broadcasted_iota(jnp.int32, (1, nb_pad), 1)
    diag = t0 // KEY_TILE
    qb_t = qb_ref[0].T
    km = km_ref[0]

    def expand_t(j, live):
        return jnp.broadcast_to(nblk_lane == jnp.where(live, j, nb_pad - 1), (KEY_TILE, nb_pad)).astype(BF16)

    q_t, gate_t = [], []
    for kvh in range(KV_HEADS):
        q_t.append(_head_cols(qb_t, kvh))
        g_t = jnp.dot(km, q_t[kvh], preferred_element_type=F32, precision=lax.Precision.HIGHEST)
        gate_t.append(jnp.where(nblk_t < bt_t, g_t, NEG_INF))
    chosen_t = _topk_mask(jnp.concatenate(gate_t, axis=1), MOBA_TOPK, 0)
    q_aug = []
    for kvh in range(KV_HEADS):
        own_t = jnp.where(nblk_t == bt_t, 1.0, chosen_t[:, kvh * r:(kvh + 1) * r])
        q_aug.append(jnp.concatenate([(q_t[kvh] * (SCALE * LOG2E)).astype(BF16), _block_bias(own_t)], axis=0))

    o_t = _flash_t(q_aug, mk_ref, mv_ref, diag, t_col, expand_t)
    heads = [o_t[kvh][:, g * tq:(g + 1) * tq] for kvh in range(KV_HEADS) for g in range(GROUP)]
    o_ref[0] = jnp.concatenate(heads, axis=0).T


def _moba(qb, kmean, full, tq):
    b, t, _ = qb.shape
    l = full.shape[1]
    assert l // MOBA_BLOCK < kmean.shape[1]
    qspec = pl.BlockSpec((1, tq, MOBA_WIDTH), lambda bi, i: (bi, i, 0))
    return pl.pallas_call(
        functools.partial(_moba_kernel, tq=tq),
        grid=(b, t // tq),
        in_specs=[
            qspec,
            pl.BlockSpec((1,) + kmean.shape[1:], lambda bi, i: (bi, 0, 0)),
            pl.BlockSpec((1, l, KV_WIDTH), lambda bi, i: (bi, 0, 4)),
            pl.BlockSpec((1, l, KV_WIDTH), lambda bi, i: (bi, 0, 5)),
        ],
        out_specs=qspec,
        out_shape=jax.ShapeDtypeStruct((b, t, MOBA_WIDTH), F32),
        compiler_params=_params(("parallel", "arbitrary")),
        name="moba",
    )(qb, kmean, full, full)


def _softmax_pv_t(s, vt):
    e = jnp.exp(s - jnp.max(s, axis=-1, keepdims=True))
    return _dot_nt(e.astype(BF16), vt) / jnp.sum(e, axis=-1, keepdims=True)


def _segment_bias(bias_rows, e_ref, block, n_keys):
    rows, seg = e_ref.shape
    parts = []
    for s in range(-(-n_keys // seg)):
        n = min(seg, n_keys - s * seg)
        b0 = s * (seg // block)
        parts.append(_dot(bias_rows[:, b0:b0 + rows], e_ref[:, 0:n]))
    return jnp.concatenate(parts, axis=1)


def _block_means_t(kt, n_keys, n_out):
    lane = lax.broadcasted_iota(jnp.int32, (1, n_out), 1)
    out = jnp.zeros((kt.shape[0], n_out), F32)
    for n in range(n_keys // MOBA_BLOCK):
        blk = kt[:, n * MOBA_BLOCK:(n + 1) * MOBA_BLOCK].astype(F32)
        out = jnp.where(lane == n, jnp.sum(blk, axis=1, keepdims=True) * (1.0 / MOBA_BLOCK), out)
    return out


def _sample_kernel(pt_ref, *refs, past_len, ts, span):
    pages = refs[:PAGES_PER_STEP]
    (new_ref, qa_ref, qar_ref, gate_ref, qb_ref, win_ref, map_ref, esel_ref, emoba_ref,
     posk_ref, w1k_ref, w2k_ref, posv_ref, w1v_ref, w2v_ref, oa_ref, ob_ref, tok_scr, kv_scr) = refs[PAGES_PER_STEP:]
    del pt_ref
    step = pl.program_id(1)
    n_steps = pl.num_programs(1)
    page = pages[0].shape[2]
    cmp_rows = 2 * KV_WIDTH

    m = past_len // CMP_STRIDE
    pitch = tok_scr.shape[1] // CMP_STRIDE
    chunks_per_page = page // CMP_STRIDE
    for j in range(PAGES_PER_STEP):
        chunk0 = (step * PAGES_PER_STEP + j) * chunks_per_page
        for slot in range(2):
            x_t = pages[j][0, slot * KV_WIDTH:(slot + 1) * KV_WIDTH, :].T
            for k in range(chunks_per_page):
                tok_scr[slot, pl.ds(chunk0 + k, CMP_STRIDE, stride=pitch), :] = (
                    x_t[k * CMP_STRIDE:(k + 1) * CMP_STRIDE])
    for k in range(past_len // (PAGES_PER_STEP * page)):
        @pl.when(step == k)
        def _(k=k):
            for j in range(PAGES_PER_STEP):
                c0 = (k * PAGES_PER_STEP + j) * page
                kv_scr[:, c0:c0 + page] = pages[j][0, cmp_rows:, :].astype(BF16)

    @pl.when(step == n_steps - 1)
    def _():
        lc = kv_scr.shape[1]
        r = GROUP * ts
        new_pad = jnp.concatenate([new_ref[0], jnp.zeros((page - ts, PAGED_WIDTH), F32)], axis=0)
        kv_scr[:, past_len:past_len + page] = new_pad.T[cmp_rows:, :].astype(BF16)

        kc_heads = _compress_rows(lambda l: tok_scr[0, l * pitch:l * pitch + m, :], posk_ref, w1k_ref, w2k_ref, m)
        vc_heads = _compress_rows(lambda l: tok_scr[1, l * pitch:l * pitch + m, :], posv_ref, w1v_ref, w2v_ref, m)

        ns_pad = map_ref.shape[1]
        nb_pad = LANES
        row = lax.broadcasted_iota(jnp.int32, (r, 1), 0)
        t_row = past_len + (row & (ts - 1))
        t_tok = past_len + lax.broadcasted_iota(jnp.int32, (ts, 1), 0)
        causal = lax.broadcasted_iota(jnp.int32, (1, lc), 1) <= t_row
        c_end = lax.broadcasted_iota(jnp.int32, (1, m), 1) * CMP_STRIDE + CMP_LEN
        blk = lax.broadcasted_iota(jnp.int32, (1, ns_pad), 1)
        nblk = lax.broadcasted_iota(jnp.int32, (1, nb_pad), 1)
        jt = t_tok >> SEL_SHIFT
        bt = t_row >> MOBA_SHIFT
        wpos = past_len - WINDOW + lax.broadcasted_iota(jnp.int32, (1, span), 1)
        w_valid = (wpos >= 0) & (wpos <= t_row) & (wpos >= t_row - WINDOW)
        gates = gate_ref[0]

        def kv_rows(slot, kvh):
            lo = slot * KV_WIDTH + kvh * HEAD_DIM
            return kv_scr[lo:lo + HEAD_DIM, :]

        o_cmp, scores, gate_s, qf = [], [], [], []
        for kvh in range(KV_HEADS):
            q = (_stack_heads(qa_ref, kvh) * SCALE).astype(BF16)
            p_cmp = _masked_softmax(_dot_nt(q, kc_heads[kvh].astype(BF16)), c_end <= t_row + 1)
            o_cmp.append(_dot(p_cmp.astype(BF16), vc_heads[kvh].astype(BF16)))
            p_sum = p_cmp[0:ts]
            for g in range(1, GROUP):
                p_sum = p_sum + p_cmp[g * ts:(g + 1) * ts]
            imp = jnp.dot(p_sum, map_ref[...], preferred_element_type=F32, precision=lax.Precision.HIGHEST)
            scores.append(jnp.where((blk == jt) | (blk == 0), FORCE_SCORE, jnp.where(blk < jt, imp, NEG_INF)))
            qf.append(_stack_heads(qb_ref, kvh))
            g_s = jnp.dot(qf[kvh], _block_means_t(kv_rows(2, kvh), past_len, nb_pad), preferred_element_type=F32,
                          precision=lax.Precision.HIGHEST)
            gate_s.append(jnp.where(nblk < bt, g_s, NEG_INF))
        sel = _topk_mask(jnp.concatenate(scores, axis=0), SEL_TOPK, 1)
        chosen = _topk_mask(jnp.concatenate(gate_s, axis=0), MOBA_TOPK, 1)

        for kvh in range(KV_HEADS):
            lo = kvh * HEAD_DIM
            qr = (_stack_heads(qar_ref, kvh) * SCALE).astype(BF16)
            bias_rows = _block_bias(jnp.concatenate([sel[kvh * ts:(kvh + 1) * ts]] * GROUP, axis=0))
            s = _dot(qr, kv_rows(0, kvh)) + _segment_bias(bias_rows, esel_ref, SEL_BLOCK, lc)
            o_sel = _softmax_pv_t(jnp.where(causal, s, NEG_INF), kv_rows(1, kvh))
            o_win = _window_branch(qr, win_ref, 0, span, lo, w_valid)
            _write_heads(oa_ref, kvh, ts, lambda g, h, rows: (
                gates[:, 3 * h:3 * h + 1] * o_cmp[kvh][rows] + gates[:, 3 * h + 1:3 * h + 2] * o_sel[rows]
                + gates[:, 3 * h + 2:3 * h + 3] * o_win[rows]))

            own = jnp.where(nblk == bt, 1.0, chosen[kvh * r:(kvh + 1) * r])
            s = (_dot((qf[kvh] * SCALE).astype(BF16), kv_rows(2, kvh))
                 + _segment_bias(_block_bias(own), emoba_ref, MOBA_BLOCK, lc))
            o = _softmax_pv_t(jnp.where(causal, s, NEG_INF), kv_rows(3, kvh))
            _write_heads(ob_ref, kvh, ts, lambda g, h, rows: o[rows])


def _block_expansion(n_rows, block, n_keys):
    return jnp.asarray(np.arange(n_rows)[:, None] == (np.arange(n_keys)[None, :] // block), dtype=BF16)


def _sample_attn(pool_t, page_table, page_base, new_rows, qa, qar, gate, qb, win, cmp_k, cmp_v):
    bs, n_pages = page_table.shape
    page = pool_t.shape[2]
    ts = new_rows.shape[1]
    past_len = n_pages * page
    n_steps = n_pages // PAGES_PER_STEP
    assert page == LANES and n_pages % PAGES_PER_STEP == 0 and ts < CMP_STRIDE and ts & (ts - 1) == 0
    assert past_len % MOBA_BLOCK == 0 and (past_len // CMP_STRIDE) % LANES == 0
    lc = past_len + page
    ns_pad = _round_up(lc // SEL_BLOCK, LANES)
    assert lc // MOBA_BLOCK + 1 <= LANES and past_len % BIAS_SEG == 0
    span = win.shape[1]
    sel_map = jnp.asarray(_selection_map(past_len // CMP_STRIDE - 1, past_len // CMP_STRIDE, ns_pad))
    e_sel = _block_expansion(BIAS_SEG // SEL_BLOCK, SEL_BLOCK, BIAS_SEG)
    e_moba = _block_expansion(2 * (BIAS_SEG // MOBA_BLOCK), MOBA_BLOCK, BIAS_SEG)

    def page_spec(j):
        return pl.BlockSpec((1, PAGED_WIDTH, page),
                            lambda b, s, pt: (page_base + pt[b, s * PAGES_PER_STEP + j], 0, 0))

    per_seq = lambda a: pl.BlockSpec((1,) + a.shape[1:], lambda b, s, pt: (b, 0, 0))
    consts = (sel_map, e_sel, e_moba) + tuple(cmp_k) + tuple(cmp_v)
    out_spec = pl.BlockSpec((1, ts, NSA_WIDTH), lambda b, s, pt: (b, 0, 0))
    grid_spec = pltpu.PrefetchScalarGridSpec(
        num_scalar_prefetch=1,
        grid=(bs, n_steps),
        in_specs=[page_spec(j) for j in range(PAGES_PER_STEP)]
        + [per_seq(a) for a in (new_rows, qa, qar, gate, qb, win)] + [_resident(a) for a in consts],
        out_specs=[out_spec, out_spec],
        scratch_shapes=[pltpu.VMEM((2, CMP_STRIDE * (past_len // CMP_STRIDE + 8), KV_WIDTH), F32),
                        pltpu.VMEM((PAGED_WIDTH - 2 * KV_WIDTH, lc), BF16)],
    )
    return pl.pallas_call(
        functools.partial(_sample_kernel, past_len=past_len, ts=ts, span=span),
        grid_spec=grid_spec,
        out_shape=[jax.ShapeDtypeStruct((bs, ts, NSA_WIDTH), F32)] * 2,
        compiler_params=_params(("parallel", "arbitrary")),
        name="sample_attn",
    )(page_table, *([pool_t] * PAGES_PER_STEP), new_rows, qa, qar, gate, qb, win, *consts)


def _out_kernel(x_ref, oa_ref, ob_ref, za_ref, zb_ref, wout_ref, gple_ref, wg_ref, ple_ref, wp_ref, gfin_ref,
                o_ref, *, final):
    mixed = jnp.concatenate([oa_ref[...] * za_ref[...], ob_ref[...] * zb_ref[...]], axis=-1).astype(BF16)
    x1 = x_ref[...] + _dot(mixed, wout_ref[...])
    gate = _sigmoid(_dot(_rms(x1, gple_ref[...]).astype(BF16), wg_ref[...]))
    x2 = x1 + gate * _dot(ple_ref[...].astype(BF16), wp_ref[...])
    o_ref[...] = _rms(x2, gfin_ref[...]) if final else x2


def _out(x2d, oa, ob, za, zb, w_out, g_ple, w_gate, ple, w_proj, g_final, final):
    n, d_model = x2d.shape
    tm = min(256, n)
    row = lambda a: pl.BlockSpec((tm, a.shape[1]), lambda i: (i, 0))
    args = (x2d, oa, ob, za, zb, w_out, g_ple, w_gate, ple, w_proj, g_final)
    specs = [row(x2d), row(oa), row(ob), row(za), row(zb), _resident(w_out), _resident(g_ple), _resident(w_gate),
             row(ple), _resident(w_proj), _resident(g_final)]
    return pl.pallas_call(
        functools.partial(_out_kernel, final=final),
        grid=(n // tm,),
        in_specs=specs,
        out_specs=row(x2d),
        out_shape=jax.ShapeDtypeStruct((n, d_model), F32),
        compiler_params=_params(("parallel",)),
        name="out",
    )(*args)


def _mixer_layer(x, ple, pos, past, win_prev, weights, g_final, final):
    (g_mix, w_split, w_out, cmp_k, cmp_v, g_ple, w_gate, w_proj) = weights
    b, t, d_model = x.shape
    n = b * t
    qa, qar, paged, win_new, gate, za, qb, zb, *transposed = _proj(x.reshape(n, d_model), pos, g_mix, w_split)
    paged3 = paged.reshape(b, t, PAGED_WIDTH)
    win_new3 = win_new.reshape(b, t, 2 * KV_WIDTH)
    tq = min(KEY_TILE, t)
    assert KEY_TILE % tq == 0 and tq & (tq - 1) == 0
    span = _round_up(WINDOW + tq, LANES)
    win_rows = max(WINDOW + t, t - tq + span)
    win = jnp.concatenate([win_prev, win_new3, jnp.zeros((b, win_rows - WINDOW - t, 2 * KV_WIDTH), F32)], axis=1)
    qa3, qar3, gate3, qb3 = (a.reshape(b, t, -1) for a in (qa, qar, gate, qb))
    if past is None:
        assert t % KEY_TILE == 0
        nc_pad = _round_up(t // CMP_STRIDE, LANES)
        kc = _compress(paged3, 0, *cmp_k, nc_pad, False)
        vc_t = _compress(paged3, 1, *cmp_v, nc_pad, True)
        o_a = _nsa(qa3, qar3, gate3, kc, vc_t, paged3, win, tq)
        o_b = _moba(qb3, _kmean(paged3, LANES), paged3, tq)
    else:
        o_a, o_b = _sample_attn(*past, paged3, qa3, qar3, gate3, qb3, win, cmp_k, cmp_v)
    x_next = _out(x.reshape(n, d_model), o_a.reshape(n, -1), o_b.reshape(n, -1), za, zb, w_out, g_ple, w_gate,
                  ple.reshape(n, -1), w_proj, g_final, final)
    if transposed:
        new_paged, new_win = (a.reshape(b, -1, KV_HEADS, HEAD_DIM, t).transpose(0, 4, 1, 2, 3) for a in transposed)
    else:
        new_paged, new_win = (a.reshape(b, t, -1, KV_HEADS, HEAD_DIM) for a in (paged3, win_new3))
    return x_next.reshape(b, t, d_model), new_paged, new_win


def _split_w_in(w_in):
    sizes = (NSA_WIDTH, 6 * KV_WIDTH, 3 * NSA_HEADS, NSA_WIDTH, MOBA_WIDTH, 2 * KV_WIDTH, MOBA_WIDTH)
    offs = np.concatenate([[0], np.cumsum(sizes)])
    wqa, wkva, wgate, wza, wqb, wkvb, wzb = [w_in[:, offs[k]:offs[k + 1]].astype(BF16) for k in range(len(sizes))]
    wgate = jnp.pad(wgate, ((0, 0), (0, LANES - wgate.shape[1])))
    return wqa, wkva, wgate, wza, wqb, wkvb, wzb


def kernel(x_prompt, x_sample, cache_paged_kv, cache_win_kv, page_table, p_prompt, p_sample, g_mix, w_in, w_out,
           cmp_pos_k, cmp_w1_k, cmp_w2_k, cmp_pos_v, cmp_w1_v, cmp_w2_v, g_ple, w_ple_gate, w_ple_proj, g_final):
    depth = w_in.shape[0]
    bp, tp, _ = x_prompt.shape
    bs, ts, _ = x_sample.shape
    n_pool, page = cache_paged_kv.shape[1], cache_paged_kv.shape[2]
    past_len = page_table.shape[1] * page
    wbuf = cache_win_kv.shape[2]
    pos_p = jnp.arange(tp, dtype=jnp.int32)
    pos_s = past_len + jnp.arange(ts, dtype=jnp.int32)
    pool_t = cache_paged_kv.transpose(0, 1, 3, 4, 5, 2).reshape(depth * n_pool, PAGED_WIDTH, page)
    g_fin = g_final.reshape(1, -1)
    xp, xs = x_prompt, x_sample
    new_pp, new_pw, new_sp, new_sw = [], [], [], []
    for i in range(depth):
        final = i == depth - 1
        weights = (g_mix[i].reshape(1, -1), _split_w_in(w_in[i]), w_out[i].astype(BF16),
                   _compress_weights(cmp_pos_k[i], cmp_w1_k[i], cmp_w2_k[i]),
                   _compress_weights(cmp_pos_v[i], cmp_w1_v[i], cmp_w2_v[i]),
                   g_ple[i].reshape(1, -1), w_ple_gate[i].astype(BF16), w_ple_proj[i].astype(BF16))
        win0 = jnp.zeros((bp, WINDOW, 2 * KV_WIDTH), F32)
        xp, pk, pw = _mixer_layer(xp, p_prompt[i], pos_p, None, win0, weights, g_fin, final)
        win_prev = jnp.concatenate([jnp.zeros((bs, WINDOW - wbuf, 2 * KV_WIDTH), F32),
                                    cache_win_kv[i].reshape(bs, wbuf, 2 * KV_WIDTH)], axis=1)
        xs, sk, sw = _mixer_layer(xs, p_sample[i], pos_s, (pool_t, page_table, i * n_pool), win_prev, weights,
                                  g_fin, final)
        new_pp.append(pk)
        new_pw.append(pw[:, tp - min(WINDOW, tp):])
        new_sp.append(sk)
        new_sw.append(sw)
    return (xp, xs, jnp.stack(new_pp), jnp.stack(new_pw), jnp.stack(new_sp), jnp.stack(new_sw))
```

```python
import functools

import numpy as np
import jax
import jax.numpy as jnp
from jax import lax
from jax.experimental import pallas as pl
from jax.experimental.pallas import tpu as pltpu

HEAD_DIM = 64
NSA_HEADS = 8
MOBA_HEADS = 8
KV_HEADS = 2
GROUP = NSA_HEADS // KV_HEADS
ROT_DIM = HEAD_DIM // 4
ROPE_THETA = 500000.0
CMP_LEN = 32
CMP_STRIDE = 16
SEL_BLOCK = 64
SEL_TOPK = 8
WINDOW = 512
MOBA_BLOCK = 256
MOBA_TOPK = 3
N_PAGED_SLOTS = 6
RMS_EPS = 1e-6
NEG_INF = -1e30
FORCE_SCORE = 1e4
SCALE = HEAD_DIM ** -0.5
LOG2E = 1.4426950408889634
SEL_SHIFT = SEL_BLOCK.bit_length() - 1
MOBA_SHIFT = MOBA_BLOCK.bit_length() - 1

LANES = 128
KV_WIDTH = KV_HEADS * HEAD_DIM
NSA_WIDTH = NSA_HEADS * HEAD_DIM
MOBA_WIDTH = MOBA_HEADS * HEAD_DIM
PAGED_WIDTH = N_PAGED_SLOTS * KV_WIDTH
KEY_TILE = 256
PAGES_PER_STEP = 16
BIAS_SEG = 2048
VMEM_LIMIT = 56 * 1024 * 1024

F32 = jnp.float32
BF16 = jnp.bfloat16
NT_DIMS = (((1,), (1,)), ((), ()))


def _round_up(n, m):
    return -(-n // m) * m


def _params(sem):
    return pltpu.CompilerParams(dimension_semantics=sem, vmem_limit_bytes=VMEM_LIMIT)


def _rms(x, g):
    return x * lax.rsqrt(jnp.mean(x * x, axis=-1, keepdims=True) + RMS_EPS) * g


def _sigmoid(x):
    return 1.0 / (1.0 + jnp.exp(-x))


def _dot(a, b):
    return jnp.dot(a, b, preferred_element_type=F32)


def _dot_nt(a, b):
    return lax.dot_general(a, b, NT_DIMS, preferred_element_type=F32)


def _dot_nt_exact(a, b):
    return lax.dot_general(a, b, NT_DIMS, preferred_element_type=F32, precision=lax.Precision.HIGHEST)


def _resident(a):
    return pl.BlockSpec(a.shape, lambda *_: (0,) * a.ndim, pipeline_mode=pl.Buffered(1))


def _rope128(v, c, sp, sm):
    half = ROT_DIM // 2
    return v * c + pltpu.roll(v, half, axis=1) * sp + pltpu.roll(v, LANES - half, axis=1) * sm


def _rope(v, c, sp, sm):
    n = v.shape[-1] // LANES
    return jnp.concatenate([_rope128(v[:, i * LANES:(i + 1) * LANES], c, sp, sm) for i in range(n)], axis=-1)


def _proj_kernel(x_ref, g_ref, c_ref, sp_ref, sm_ref, wqa, wkva, wgate, wza, wqb, wkvb, wzb,
                 qa_o, qar_o, paged_o, win_o, gate_o, za_o, qb_o, zb_o, *transposed_o):
    h = _rms(x_ref[...], g_ref[...]).astype(BF16)
    c, sp, sm = c_ref[...], sp_ref[...], sm_ref[...]
    qa = _dot(h, wqa[...])
    qa_o[...] = qa
    qar_o[...] = _rope(qa, c, sp, sm)
    kva = _dot(h, wkva[...])
    paged_o[:, 0:2 * KV_WIDTH] = kva[:, 0:2 * KV_WIDTH]
    paged_o[:, 2 * KV_WIDTH:3 * KV_WIDTH] = _rope(kva[:, 2 * KV_WIDTH:3 * KV_WIDTH], c, sp, sm)
    paged_o[:, 3 * KV_WIDTH:4 * KV_WIDTH] = kva[:, 3 * KV_WIDTH:4 * KV_WIDTH]
    win_o[:, 0:KV_WIDTH] = _rope(kva[:, 4 * KV_WIDTH:5 * KV_WIDTH], c, sp, sm)
    win_o[:, KV_WIDTH:2 * KV_WIDTH] = kva[:, 5 * KV_WIDTH:6 * KV_WIDTH]
    kvb = _dot(h, wkvb[...])
    paged_o[:, 4 * KV_WIDTH:5 * KV_WIDTH] = _rope(kvb[:, 0:KV_WIDTH], c, sp, sm)
    paged_o[:, 5 * KV_WIDTH:6 * KV_WIDTH] = kvb[:, KV_WIDTH:2 * KV_WIDTH]
    gate_o[...] = _sigmoid(_dot(h, wgate[...]))
    za = _dot(h, wza[...])
    za_o[...] = za * _sigmoid(za)
    qb_o[...] = _rope(_dot(h, wqb[...]), c, sp, sm)
    zb = _dot(h, wzb[...])
    zb_o[...] = zb * _sigmoid(zb)
    if transposed_o:
        paged_t_o, win_t_o = transposed_o
        paged_t_o[0] = paged_o[...].T
        win_t_o[0] = win_o[...].T


def _rope_tables(pos, n_rows):
    half = ROT_DIM // 2
    inv_freq = ROPE_THETA ** (-jnp.arange(half, dtype=F32) / half)
    ang = pos.astype(F32)[:, None] * inv_freq[None, :]
    cos, sin = jnp.cos(ang), jnp.sin(ang)
    d = np.arange(LANES) % HEAD_DIM
    idx = d % half
    c = jnp.where(d < ROT_DIM, cos[:, idx], 1.0)
    sp = jnp.where((d >= half) & (d < ROT_DIM), sin[:, idx], 0.0)
    sm = jnp.where(d < half, -sin[:, idx], 0.0)
    rep = n_rows // pos.shape[0]
    return tuple(jnp.tile(t, (rep, 1)) for t in (c, sp, sm))


def _proj(x2d, pos, g, w):
    n, d_model = x2d.shape
    t = pos.shape[0]
    tm = min(256, n)
    tab_rows = max(t, tm)
    tabs = _rope_tables(pos, tab_rows)
    n_tab = tab_rows // tm
    row = lambda width: pl.BlockSpec((tm, width), lambda i: (i, 0))
    tab = pl.BlockSpec((tm, LANES), lambda i: (i % n_tab, 0))
    widths = (NSA_WIDTH, NSA_WIDTH, PAGED_WIDTH, 2 * KV_WIDTH, LANES, NSA_WIDTH, MOBA_WIDTH, MOBA_WIDTH)
    out_specs = [row(wd) for wd in widths]
    out_shape = [jax.ShapeDtypeStruct((n, wd), F32) for wd in widths]
    if t % tm == 0:
        tiles = t // tm
        for wd in (PAGED_WIDTH, 2 * KV_WIDTH):
            out_specs.append(pl.BlockSpec((1, wd, tm), lambda i: (i // tiles, 0, i % tiles)))
            out_shape.append(jax.ShapeDtypeStruct((n // t, wd, t), F32))
    return pl.pallas_call(
        _proj_kernel,
        grid=(n // tm,),
        in_specs=[row(d_model), _resident(g), tab, tab, tab] + [_resident(a) for a in w],
        out_specs=out_specs,
        out_shape=out_shape,
        compiler_params=_params(("parallel",)),
        name="proj",
    )(x2d, g, *tabs, *w)


def _compress_rows(chunk_row, pos_ref, w1_ref, w2_ref, m):
    assert KV_HEADS == 2 and KV_WIDTH == LANES
    hidden = w2_ref.shape[0]
    low = lax.broadcasted_iota(jnp.int32, (1, LANES), 1) < HEAD_DIM
    heads = [[], []]
    for l in range(0, CMP_STRIDE, 2):
        a, b = chunk_row(l), chunk_row(l + 1)
        heads[0].append(jnp.where(low, a, pltpu.roll(b, HEAD_DIM, axis=1)).astype(BF16))
        heads[1].append(jnp.where(low, pltpu.roll(a, HEAD_DIM, axis=1), b).astype(BF16))
    x = jnp.concatenate([jnp.concatenate(h, axis=1) for h in heads] + [pos_ref[...]], axis=0)
    ab = _dot(x, w1_ref[...])
    bias = ab[2 * m:2 * m + 1, 0:hidden] + ab[2 * m + 1:2 * m + 2, hidden:2 * hidden]
    hid = []
    for h in range(KV_HEADS):
        rows = slice(h * m, (h + 1) * m)
        nxt = pltpu.roll(ab[rows, hidden:2 * hidden], m - 1, axis=0)
        pre = ab[rows, 0:hidden] + nxt + bias
        hid.append((pre * _sigmoid(pre)).astype(BF16))
    out = _dot(jnp.concatenate(hid, axis=0), w2_ref[...])
    return [out[h * m:(h + 1) * m] for h in range(KV_HEADS)]


def _compress_kernel(src_ref, pos_ref, w1_ref, w2_ref, o_ref, *, n_chunks, nc_pad, transposed):
    m = n_chunks
    out = jnp.concatenate(
        _compress_rows(lambda l: src_ref[0, pl.ds(l, m, stride=CMP_STRIDE), :], pos_ref, w1_ref, w2_ref, m), axis=1)
    if nc_pad > m:
        out = jnp.concatenate([out, jnp.zeros((nc_pad - m, LANES), F32)], axis=0)
    o_ref[0] = out.T if transposed else out


def _compress(full, slot, pos_rows, w1, w2, nc_pad, transposed):
    b, l, _ = full.shape
    n_chunks = l // CMP_STRIDE
    out_dims = (KV_WIDTH, nc_pad) if transposed else (nc_pad, KV_WIDTH)
    return pl.pallas_call(
        functools.partial(_compress_kernel, n_chunks=n_chunks, nc_pad=nc_pad, transposed=transposed),
        grid=(b,),
        in_specs=[pl.BlockSpec((1, l, KV_WIDTH), lambda i: (i, 0, slot)),
                  _resident(pos_rows), _resident(w1), _resident(w2)],
        out_specs=pl.BlockSpec((1,) + out_dims, lambda i: (i, 0, 0)),
        out_shape=jax.ShapeDtypeStruct((b,) + out_dims, F32),
        compiler_params=_params(("parallel",)),
        name="compress",
    )(full, pos_rows, w1, w2)


def _compress_weights(pos_emb, w1, w2):
    hidden = w1.shape[-1]
    rows = CMP_STRIDE * HEAD_DIM
    w1cat = jnp.concatenate([w1[:CMP_STRIDE].reshape(rows, hidden), w1[CMP_STRIDE:].reshape(rows, hidden)], axis=1)
    pos_rows = jnp.zeros((16, rows), F32)
    pos_rows = pos_rows.at[0].set(pos_emb[:CMP_STRIDE].reshape(-1)).at[1].set(pos_emb[CMP_STRIDE:].reshape(-1))
    return pos_rows.astype(BF16), w1cat.astype(BF16), w2.astype(BF16)


def _kmean_kernel(src_ref, o_ref, *, n_blocks):
    x = src_ref[0].reshape(n_blocks, MOBA_BLOCK, KV_WIDTH)
    o_ref[0, 0:n_blocks, :] = jnp.sum(x, axis=1) * (1.0 / MOBA_BLOCK)
    o_ref[0, n_blocks:, :] = jnp.zeros((o_ref.shape[1] - n_blocks, KV_WIDTH), F32)


def _kmean(full, nb_pad):
    b, l, _ = full.shape
    return pl.pallas_call(
        functools.partial(_kmean_kernel, n_blocks=l // MOBA_BLOCK),
        grid=(b,),
        in_specs=[pl.BlockSpec((1, l, KV_WIDTH), lambda i: (i, 0, 4))],
        out_specs=pl.BlockSpec((1, nb_pad, KV_WIDTH), lambda i: (i, 0, 0)),
        out_shape=jax.ShapeDtypeStruct((b, nb_pad, KV_WIDTH), F32),
        compiler_params=_params(("parallel",)),
        name="kmean",
    )(full)


def _stack_heads(ref, kvh):
    return jnp.concatenate(
        [ref[0, :, (GROUP * kvh + g) * HEAD_DIM:(GROUP * kvh + g + 1) * HEAD_DIM] for g in range(GROUP)], axis=0)


def _masked_softmax(s, valid):
    s = jnp.where(valid, s, NEG_INF)
    e = jnp.where(valid, jnp.exp(s - jnp.max(s, axis=-1, keepdims=True)), 0.0)
    l = jnp.sum(e, axis=-1, keepdims=True)
    return e / jnp.where(l > 0.0, l, 1.0)


def _topk_mask(score, k, axis):
    n = score.shape[axis]
    idx = lax.broadcasted_iota(jnp.int32, score.shape, axis).astype(F32)
    sel = jnp.zeros(score.shape, F32)
    s = score
    for _ in range(k):
        m = jnp.max(s, axis=axis, keepdims=True)
        first = jnp.min(jnp.where(s == m, idx, float(n)), axis=axis, keepdims=True)
        pick = idx == first
        sel = jnp.where(pick & (m > 0.5 * NEG_INF), 1.0, sel)
        s = jnp.where(pick, -3e38, s)
    return sel


def _topk_rows(score, k):
    n = score.shape[0]
    pad = jnp.full((LANES - n, score.shape[1]), NEG_INF, F32)
    return _topk_mask(jnp.concatenate([score, pad], axis=0).T, k, 0).T[0:n]


def _block_bias(chosen):
    return ((chosen - 1.0) * -NEG_INF).astype(BF16)


def _flash_t(q_aug, k_ref, v_ref, diag, t_col, expand_t):
    n_heads = len(q_aug)
    ones = jnp.ones((16, KEY_TILE), BF16)

    def scores(j, live):
        k0 = pl.multiple_of(j * KEY_TILE, KEY_TILE)
        kt = k_ref[0, pl.ds(k0, KEY_TILE), :].astype(BF16)
        lhs = jnp.concatenate([kt, expand_t(j, live)], axis=1)
        return [_dot(lhs, qa) for qa in q_aug]

    def values_t(j):
        k0 = pl.multiple_of(j * KEY_TILE, KEY_TILE)
        vt = v_ref[0, pl.ds(k0, KEY_TILE), :].T.astype(BF16)
        return [jnp.concatenate([vt[h * HEAD_DIM:(h + 1) * HEAD_DIM], ones], axis=0) for h in range(n_heads)]

    visible = diag * KEY_TILE + lax.broadcasted_iota(jnp.int32, (KEY_TILE, 1), 0) <= t_col
    init = []
    for s, vt in zip(scores(diag, True), values_t(diag)):
        s = jnp.where(visible, s, NEG_INF)
        m0 = jnp.max(s, axis=0, keepdims=True)
        init += [m0, _dot(vt, jnp.exp2(s - m0).astype(BF16))]

    def update(stats, s, vt):
        m_i, acc = stats
        m_new = jnp.maximum(m_i, jnp.max(s, axis=0, keepdims=True))
        return [m_new, jnp.exp2(m_i - m_new) * acc + _dot(vt, jnp.exp2(s - m_new).astype(BF16))]

    def body(i, carry):
        ja, jb = 2 * i, 2 * i + 1
        live = jb < diag
        jb = jnp.minimum(jb, diag)
        sa, va = scores(ja, True), values_t(ja)
        sb, vb = scores(jb, live), values_t(jb)
        out = []
        for h in range(n_heads):
            out += update(update(carry[2 * h:2 * h + 2], sa[h], va[h]), sb[h], vb[h])
        return tuple(out)

    res = lax.fori_loop(0, (diag + 1) // 2, body, tuple(init))
    return [res[2 * h + 1][0:HEAD_DIM] / res[2 * h + 1][HEAD_DIM:HEAD_DIM + 1] for h in range(n_heads)]


def _head_cols(x_t, kvh):
    x = jnp.concatenate([x_t[(GROUP * kvh + g) * HEAD_DIM:(GROUP * kvh + g + 1) * HEAD_DIM] for g in range(GROUP)],
                        axis=1)
    return jnp.concatenate([x if k == kvh else jnp.zeros_like(x) for k in range(KV_HEADS)], axis=0)


def _selection_map(n_cmp, nc_pad, ns_pad):
    c0 = np.arange(nc_pad)[:, None] * CMP_STRIDE
    j0 = np.arange(ns_pad)[None, :] * SEL_BLOCK
    ov = np.clip(np.minimum(c0 + CMP_LEN, j0 + SEL_BLOCK) - np.maximum(c0, j0), 0, None) / CMP_STRIDE
    return np.where(np.arange(nc_pad)[:, None] < n_cmp, ov, 0.0).astype(np.float32)


def _window_branch(qr, win_ref, w0, span, lo, w_valid):
    wk = win_ref[0, pl.ds(w0, span), lo:lo + HEAD_DIM].astype(BF16)
    wv = win_ref[0, pl.ds(w0, span), KV_WIDTH + lo:KV_WIDTH + lo + HEAD_DIM].astype(BF16)
    s = jnp.where(w_valid, _dot_nt(qr, wk), NEG_INF)
    e = jnp.exp(s - jnp.max(s, axis=-1, keepdims=True))
    return _dot(e.astype(BF16), wv) / jnp.sum(e, axis=-1, keepdims=True)


def _write_heads(o_ref, kvh, tq, head_out):
    for g in range(GROUP):
        h = GROUP * kvh + g
        o_ref[0, :, h * HEAD_DIM:(h + 1) * HEAD_DIM] = head_out(g, h, slice(g * tq, (g + 1) * tq))


def _nsa_kernel(qa_ref, qar_ref, gate_ref, kc_ref, vct_ref, sk_ref, sv_ref, win_ref, mapt_ref, o_ref, *, tq, span):
    i = pl.program_id(1)
    t0 = i * tq
    r = GROUP * tq
    nc_pad = kc_ref.shape[1]
    ns_pad = mapt_ref.shape[0]
    t_col = t0 + (lax.broadcasted_iota(jnp.int32, (1, r), 1) & (tq - 1))
    c_end = lax.broadcasted_iota(jnp.int32, (nc_pad, 1), 0) * CMP_STRIDE + CMP_LEN
    blk_t = lax.broadcasted_iota(jnp.int32, (ns_pad, 1), 0)
    blk_lane = lax.broadcasted_iota(jnp.int32, (1, ns_pad), 1)
    key_blk = lax.broadcasted_iota(jnp.int32, (KEY_TILE, 1), 0) >> SEL_SHIFT
    jt_t = (t0 + lax.broadcasted_iota(jnp.int32, (1, tq), 1)) >> SEL_SHIFT
    diag = t0 // KEY_TILE
    w0 = pl.multiple_of(jnp.maximum(t0 - WINDOW, 0), 8)
    wpos = w0 + lax.broadcasted_iota(jnp.int32, (span, 1), 0)
    w_valid = (wpos <= t_col) & (wpos >= t_col - WINDOW)
    c_valid = c_end <= t_col + 1

    qa_t = qa_ref[0].T
    qar_t = qar_ref[0].T
    gates_t = gate_ref[0].T
    kc = kc_ref[0].astype(BF16)
    wk = win_ref[0, pl.ds(w0, span), 0:KV_WIDTH].astype(BF16)
    wv_t = win_ref[0, pl.ds(w0, span), KV_WIDTH:2 * KV_WIDTH].T.astype(BF16)
    w_ones = jnp.ones((16, span), BF16)

    def expand_t(j, live):
        return (jnp.where(live, key_blk + j * (KEY_TILE // SEL_BLOCK), ns_pad - 1) == blk_lane).astype(BF16)

    o_cmp_t, o_win_t, qr_t, score_t = [], [], [], []
    for kvh in range(KV_HEADS):
        rows = slice(kvh * HEAD_DIM, (kvh + 1) * HEAD_DIM)
        q_t = (_head_cols(qa_t, kvh) * SCALE).astype(BF16)
        s = jnp.where(c_valid, _dot(kc, q_t), NEG_INF)
        e = jnp.where(c_valid, jnp.exp(s - jnp.max(s, axis=0, keepdims=True)), 0.0)
        l = jnp.sum(e, axis=0, keepdims=True)
        p_cmp = e / jnp.where(l > 0.0, l, 1.0)
        o_cmp_t.append(_dot(vct_ref[0, rows, :].astype(BF16), p_cmp.astype(BF16)))

        p_sum = p_cmp[:, 0:tq]
        for g in range(1, GROUP):
            p_sum = p_sum + p_cmp[:, g * tq:(g + 1) * tq]
        imp_t = jnp.dot(mapt_ref[...], p_sum, preferred_element_type=F32, precision=lax.Precision.HIGHEST)
        score_t.append(jnp.where((blk_t == jt_t) | (blk_t == 0), FORCE_SCORE,
                                 jnp.where(blk_t < jt_t, imp_t, NEG_INF)))

        qr_t.append((_head_cols(qar_t, kvh) * (SCALE * LOG2E)).astype(BF16))
        s = jnp.where(w_valid, _dot(wk, qr_t[kvh]), NEG_INF)
        e = jnp.exp2(s - jnp.max(s, axis=0, keepdims=True))
        o_win = _dot(jnp.concatenate([wv_t[rows], w_ones], axis=0), e.astype(BF16))
        o_win_t.append(o_win[0:HEAD_DIM] / o_win[HEAD_DIM:HEAD_DIM + 1])

    bias_t = _block_bias(_topk_mask(jnp.concatenate(score_t, axis=1), SEL_TOPK, 0))
    q_aug = [jnp.concatenate([qr_t[kvh], jnp.concatenate([bias_t[:, kvh * tq:(kvh + 1) * tq]] * GROUP, axis=1)],
                             axis=0) for kvh in range(KV_HEADS)]

    o_sel_t = _flash_t(q_aug, sk_ref, sv_ref, diag, t_col, expand_t)

    heads = []
    for kvh in range(KV_HEADS):
        for g in range(GROUP):
            h = GROUP * kvh + g
            cols = slice(g * tq, (g + 1) * tq)
            heads.append(gates_t[3 * h:3 * h + 1] * o_cmp_t[kvh][:, cols]
                         + gates_t[3 * h + 1:3 * h + 2] * o_sel_t[kvh][:, cols]
                         + gates_t[3 * h + 2:3 * h + 3] * o_win_t[kvh][:, cols])
    o_ref[0] = jnp.concatenate(heads, axis=0).T


def _nsa(qa, qar, gate, kc, vc_t, full, win, tq):
    b, t, _ = qa.shape
    l = full.shape[1]
    nc_pad = kc.shape[1]
    ns_pad = _round_up(l // SEL_BLOCK + 1, LANES)
    span = _round_up(WINDOW + tq, LANES)
    map_t = jnp.asarray(_selection_map(l // CMP_STRIDE - 1, nc_pad, ns_pad).T)
    qspec = pl.BlockSpec((1, tq, NSA_WIDTH), lambda bi, i: (bi, i, 0))
    seq = lambda a: pl.BlockSpec((1,) + a.shape[1:], lambda bi, i: (bi, 0, 0))
    return pl.pallas_call(
        functools.partial(_nsa_kernel, tq=tq, span=span),
        grid=(b, t // tq),
        in_specs=[
            qspec, qspec,
            pl.BlockSpec((1, tq, LANES), lambda bi, i: (bi, i, 0)),
            seq(kc), seq(vc_t),
            pl.BlockSpec((1, l, KV_WIDTH), lambda bi, i: (bi, 0, 2)),
            pl.BlockSpec((1, l, KV_WIDTH), lambda bi, i: (bi, 0, 3)),
            seq(win),
            _resident(map_t),
        ],
        out_specs=qspec,
        out_shape=jax.ShapeDtypeStruct((b, t, NSA_WIDTH), F32),
        compiler_params=_params(("parallel", "arbitrary")),
        name="nsa",
    )(qa, qar, gate, kc, vc_t, full, full, win, map_t)


def _moba_kernel(qb_ref, km_ref, mk_ref, mv_ref, o_ref, *, tq):
    i = pl.program_id(1)
    t0 = i * tq
    r = GROUP * tq
    nb_pad = km_ref.shape[1]
    t_col = t0 + (lax.broadcasted_iota(jnp.int32, (1, r), 1) & (tq - 1))
    bt_t = t_col >> MOBA_SHIFT
    nblk_t = lax.broadcasted_iota(jnp.int32, (nb_pad, 1), 0)
    nblk_lane = lax.broadcasted_iota(jnp.int32, (1, nb_pad), 1)
    diag = t0 // KEY_TILE
    qb_t = qb_ref[0].T
    km = km_ref[0]

    def expand_t(j, live):
        return jnp.broadcast_to(nblk_lane == jnp.where(live, j, nb_pad - 1), (KEY_TILE, nb_pad)).astype(BF16)

    q_t, gate_t = [], []
    for kvh in range(KV_HEADS):
        q_t.append(_head_cols(qb_t, kvh))
        g_t = jnp.dot(km, q_t[kvh], preferred_element_type=F32, precision=lax.Precision.HIGHEST)
        gate_t.append(jnp.where(nblk_t < bt_t, g_t, NEG_INF))
    chosen_t = _topk_mask(jnp.concatenate(gate_t, axis=1), MOBA_TOPK, 0)
    q_aug = []
    for kvh in range(KV_HEADS):
        own_t = jnp.where(nblk_t == bt_t, 1.0, chosen_t[:, kvh * r:(kvh + 1) * r])
        q_aug.append(jnp.concatenate([(q_t[kvh] * (SCALE * LOG2E)).astype(BF16), _block_bias(own_t)], axis=0))

    o_t = _flash_t(q_aug, mk_ref, mv_ref, diag, t_col, expand_t)
    heads = [o_t[kvh][:, g * tq:(g + 1) * tq] for kvh in range(KV_HEADS) for g in range(GROUP)]
    o_ref[0] = jnp.concatenate(heads, axis=0).T


def _moba(qb, kmean, full, tq):
    b, t, _ = qb.shape
    l = full.shape[1]
    assert l // MOBA_BLOCK < kmean.shape[1]
    qspec = pl.BlockSpec((1, tq, MOBA_WIDTH), lambda bi, i: (bi, i, 0))
    return pl.pallas_call(
        functools.partial(_moba_kernel, tq=tq),
        grid=(b, t // tq),
        in_specs=[
            qspec,
            pl.BlockSpec((1,) + kmean.shape[1:], lambda bi, i: (bi, 0, 0)),
            pl.BlockSpec((1, l, KV_WIDTH), lambda bi, i: (bi, 0, 4)),
            pl.BlockSpec((1, l, KV_WIDTH), lambda bi, i: (bi, 0, 5)),
        ],
        out_specs=qspec,
        out_shape=jax.ShapeDtypeStruct((b, t, MOBA_WIDTH), F32),
        compiler_params=_params(("parallel", "arbitrary")),
        name="moba",
    )(qb, kmean, full, full)


def _softmax_pv_t(s, vt):
    e = jnp.exp(s - jnp.max(s, axis=-1, keepdims=True))
    return _dot_nt(e.astype(BF16), vt) / jnp.sum(e, axis=-1, keepdims=True)


def _segment_bias(bias_rows, e_ref, block, n_keys):
    rows, seg = e_ref.shape
    parts = []
    for s in range(-(-n_keys // seg)):
        n = min(seg, n_keys - s * seg)
        b0 = s * (seg // block)
        parts.append(_dot(bias_rows[:, b0:b0 + rows], e_ref[:, 0:n]))
    return jnp.concatenate(parts, axis=1)


def _block_means_t(kt, n_keys, n_out):
    lane = lax.broadcasted_iota(jnp.int32, (1, n_out), 1)
    out = jnp.zeros((kt.shape[0], n_out), F32)
    for n in range(n_keys // MOBA_BLOCK):
        blk = kt[:, n * MOBA_BLOCK:(n + 1) * MOBA_BLOCK].astype(F32)
        out = jnp.where(lane == n, jnp.sum(blk, axis=1, keepdims=True) * (1.0 / MOBA_BLOCK), out)
    return out


def _sample_kernel(pt_ref, *refs, past_len, ts, span):
    pages = refs[:PAGES_PER_STEP]
    (new_ref, qa_ref, qar_ref, gate_ref, qb_ref, win_ref, map_ref, esel_ref, emoba_ref,
     posk_ref, w1k_ref, w2k_ref, posv_ref, w1v_ref, w2v_ref, oa_ref, ob_ref, tok_scr, kv_scr) = refs[PAGES_PER_STEP:]
    del pt_ref
    step = pl.program_id(1)
    n_steps = pl.num_programs(1)
    page = pages[0].shape[2]
    cmp_rows = 2 * KV_WIDTH

    m = past_len // CMP_STRIDE
    pitch = tok_scr.shape[1] // CMP_STRIDE
    chunks_per_page = page // CMP_STRIDE
    for j in range(PAGES_PER_STEP):
        chunk0 = (step * PAGES_PER_STEP + j) * chunks_per_page
        for slot in range(2):
            x_t = pages[j][0, slot * KV_WIDTH:(slot + 1) * KV_WIDTH, :].T
            for k in range(chunks_per_page):
                tok_scr[slot, pl.ds(chunk0 + k, CMP_STRIDE, stride=pitch), :] = (
                    x_t[k * CMP_STRIDE:(k + 1) * CMP_STRIDE])
    for k in range(past_len // (PAGES_PER_STEP * page)):
        @pl.when(step == k)
        def _(k=k):
            for j in range(PAGES_PER_STEP):
                c0 = (k * PAGES_PER_STEP + j) * page
                kv_scr[:, c0:c0 + page] = pages[j][0, cmp_rows:, :].astype(BF16)

    @pl.when(step == n_steps - 1)
    def _():
        lc = kv_scr.shape[1]
        r = GROUP * ts
        new_pad = jnp.concatenate([new_ref[0], jnp.zeros((page - ts, PAGED_WIDTH), F32)], axis=0)
        kv_scr[:, past_len:past_len + page] = new_pad.T[cmp_rows:, :].astype(BF16)

        kc_heads = _compress_rows(lambda l: tok_scr[0, l * pitch:l * pitch + m, :], posk_ref, w1k_ref, w2k_ref, m)
        vc_heads = _compress_rows(lambda l: tok_scr[1, l * pitch:l * pitch + m, :], posv_ref, w1v_ref, w2v_ref, m)

        ns_pad = map_ref.shape[1]
        nb_pad = LANES
        row = lax.broadcasted_iota(jnp.int32, (r, 1), 0)
        t_row = past_len + (row & (ts - 1))
        t_tok = past_len + lax.broadcasted_iota(jnp.int32, (ts, 1), 0)
        new_visible = past_len + lax.broadcasted_iota(jnp.int32, (1, page), 1) <= t_row

        def causal(s):
            return jnp.concatenate([s[:, 0:past_len], jnp.where(new_visible, s[:, past_len:], NEG_INF)], axis=1)

        c_end = lax.broadcasted_iota(jnp.int32, (1, m), 1) * CMP_STRIDE + CMP_LEN
        blk = lax.broadcasted_iota(jnp.int32, (1, ns_pad), 1)
        nblk = lax.broadcasted_iota(jnp.int32, (1, nb_pad), 1)
        jt = t_tok >> SEL_SHIFT
        bt = t_row >> MOBA_SHIFT
        wpos = past_len - WINDOW + lax.broadcasted_iota(jnp.int32, (1, span), 1)
        w_valid = (wpos >= 0) & (wpos <= t_row) & (wpos >= t_row - WINDOW)
        gates = gate_ref[0]

        def kv_rows(slot, kvh):
            lo = slot * KV_WIDTH + kvh * HEAD_DIM
            return kv_scr[lo:lo + HEAD_DIM, :]

        o_cmp, scores, gate_s, qf = [], [], [], []
        for kvh in range(KV_HEADS):
            q = (_stack_heads(qa_ref, kvh) * SCALE).astype(BF16)
            p_cmp = _masked_softmax(_dot_nt(q, kc_heads[kvh].astype(BF16)), c_end <= t_row + 1)
            o_cmp.append(_dot(p_cmp.astype(BF16), vc_heads[kvh].astype(BF16)))
            p_sum = p_cmp[0:ts]
            for g in range(1, GROUP):
                p_sum = p_sum + p_cmp[g * ts:(g + 1) * ts]
            imp = jnp.dot(p_sum, map_ref[...], preferred_element_type=F32, precision=lax.Precision.HIGHEST)
            scores.append(jnp.where((blk == jt) | (blk == 0), FORCE_SCORE, jnp.where(blk < jt, imp, NEG_INF)))
            qf.append(_stack_heads(qb_ref, kvh))
            g_s = jnp.dot(qf[kvh], _block_means_t(kv_rows(2, kvh), past_len, nb_pad), preferred_element_type=F32,
                          precision=lax.Precision.HIGHEST)
            gate_s.append(jnp.where(nblk < bt, g_s, NEG_INF))
        sel = _topk_rows(jnp.concatenate(scores, axis=0), SEL_TOPK)
        chosen = _topk_rows(jnp.concatenate(gate_s, axis=0), MOBA_TOPK)

        for kvh in range(KV_HEADS):
            lo = kvh * HEAD_DIM
            qr = (_stack_heads(qar_ref, kvh) * SCALE).astype(BF16)
            bias_rows = _block_bias(jnp.concatenate([sel[kvh * ts:(kvh + 1) * ts]] * GROUP, axis=0))
            s = _dot(qr, kv_rows(0, kvh)) + _segment_bias(bias_rows, esel_ref, SEL_BLOCK, lc)
            o_sel = _softmax_pv_t(causal(s), kv_rows(1, kvh))
            o_win = _window_branch(qr, win_ref, 0, span, lo, w_valid)
            _write_heads(oa_ref, kvh, ts, lambda g, h, rows: (
                gates[:, 3 * h:3 * h + 1] * o_cmp[kvh][rows] + gates[:, 3 * h + 1:3 * h + 2] * o_sel[rows]
                + gates[:, 3 * h + 2:3 * h + 3] * o_win[rows]))

            own = jnp.where(nblk == bt, 1.0, chosen[kvh * r:(kvh + 1) * r])
            s = (_dot((qf[kvh] * SCALE).astype(BF16), kv_rows(2, kvh))
                 + _segment_bias(_block_bias(own), emoba_ref, MOBA_BLOCK, lc))
            o = _softmax_pv_t(causal(s), kv_rows(3, kvh))
            _write_heads(ob_ref, kvh, ts, lambda g, h, rows: o[rows])


def _block_expansion(n_rows, block, n_keys):
    return jnp.asarray(np.arange(n_rows)[:, None] == (np.arange(n_keys)[None, :] // block), dtype=BF16)


def _sample_attn(pool_t, page_table, page_base, new_rows, qa, qar, gate, qb, win, cmp_k, cmp_v):
    bs, n_pages = page_table.shape
    page = pool_t.shape[2]
    ts = new_rows.shape[1]
    past_len = n_pages * page
    n_steps = n_pages // PAGES_PER_STEP
    assert page == LANES and n_pages % PAGES_PER_STEP == 0 and ts < CMP_STRIDE and ts & (ts - 1) == 0
    assert past_len % MOBA_BLOCK == 0 and (past_len // CMP_STRIDE) % LANES == 0
    lc = past_len + page
    ns_pad = _round_up(lc // SEL_BLOCK, LANES)
    assert lc // MOBA_BLOCK + 1 <= LANES and past_len % BIAS_SEG == 0
    span = win.shape[1]
    sel_map = jnp.asarray(_selection_map(past_len // CMP_STRIDE - 1, past_len // CMP_STRIDE, ns_pad))
    e_sel = _block_expansion(BIAS_SEG // SEL_BLOCK, SEL_BLOCK, BIAS_SEG)
    e_moba = _block_expansion(2 * (BIAS_SEG // MOBA_BLOCK), MOBA_BLOCK, BIAS_SEG)

    def page_spec(j):
        return pl.BlockSpec((1, PAGED_WIDTH, page),
                            lambda b, s, pt: (page_base + pt[b, s * PAGES_PER_STEP + j], 0, 0))

    per_seq = lambda a: pl.BlockSpec((1,) + a.shape[1:], lambda b, s, pt: (b, 0, 0))
    consts = (sel_map, e_sel, e_moba) + tuple(cmp_k) + tuple(cmp_v)
    out_spec = pl.BlockSpec((1, ts, NSA_WIDTH), lambda b, s, pt: (b, 0, 0))
    grid_spec = pltpu.PrefetchScalarGridSpec(
        num_scalar_prefetch=1,
        grid=(bs, n_steps),
        in_specs=[page_spec(j) for j in range(PAGES_PER_STEP)]
        + [per_seq(a) for a in (new_rows, qa, qar, gate, qb, win)] + [_resident(a) for a in consts],
        out_specs=[out_spec, out_spec],
        scratch_shapes=[pltpu.VMEM((2, CMP_STRIDE * (past_len // CMP_STRIDE + 8), KV_WIDTH), F32),
                        pltpu.VMEM((PAGED_WIDTH - 2 * KV_WIDTH, lc), BF16)],
    )
    return pl.pallas_call(
        functools.partial(_sample_kernel, past_len=past_len, ts=ts, span=span),
        grid_spec=grid_spec,
        out_shape=[jax.ShapeDtypeStruct((bs, ts, NSA_WIDTH), F32)] * 2,
        compiler_params=_params(("parallel", "arbitrary")),
        name="sample_attn",
    )(page_table, *([pool_t] * PAGES_PER_STEP), new_rows, qa, qar, gate, qb, win, *consts)


def _out_kernel(x_ref, oa_ref, ob_ref, za_ref, zb_ref, wout_ref, gple_ref, wg_ref, ple_ref, wp_ref, gfin_ref,
                o_ref, *, final):
    mixed = jnp.concatenate([oa_ref[...] * za_ref[...], ob_ref[...] * zb_ref[...]], axis=-1).astype(BF16)
    x1 = x_ref[...] + _dot(mixed, wout_ref[...])
    gate = _sigmoid(_dot(_rms(x1, gple_ref[...]).astype(BF16), wg_ref[...]))
    x2 = x1 + gate * _dot(ple_ref[...].astype(BF16), wp_ref[...])
    o_ref[...] = _rms(x2, gfin_ref[...]) if final else x2


def _out(x2d, oa, ob, za, zb, w_out, g_ple, w_gate, ple, w_proj, g_final, final):
    n, d_model = x2d.shape
    tm = min(256, n)
    row = lambda a: pl.BlockSpec((tm, a.shape[1]), lambda i: (i, 0))
    args = (x2d, oa, ob, za, zb, w_out, g_ple, w_gate, ple, w_proj, g_final)
    specs = [row(x2d), row(oa), row(ob), row(za), row(zb), _resident(w_out), _resident(g_ple), _resident(w_gate),
             row(ple), _resident(w_proj), _resident(g_final)]
    return pl.pallas_call(
        functools.partial(_out_kernel, final=final),
        grid=(n // tm,),
        in_specs=specs,
        out_specs=row(x2d),
        out_shape=jax.ShapeDtypeStruct((n, d_model), F32),
        compiler_params=_params(("parallel",)),
        name="out",
    )(*args)


def _mixer_layer(x, ple, pos, past, win_prev, weights, g_final, final):
    (g_mix, w_split, w_out, cmp_k, cmp_v, g_ple, w_gate, w_proj) = weights
    b, t, d_model = x.shape
    n = b * t
    qa, qar, paged, win_new, gate, za, qb, zb, *transposed = _proj(x.reshape(n, d_model), pos, g_mix, w_split)
    paged3 = paged.reshape(b, t, PAGED_WIDTH)
    win_new3 = win_new.reshape(b, t, 2 * KV_WIDTH)
    tq = min(KEY_TILE, t)
    assert KEY_TILE % tq == 0 and tq & (tq - 1) == 0
    span = _round_up(WINDOW + tq, LANES)
    qa3, qar3, gate3, qb3 = (a.reshape(b, t, -1) for a in (qa, qar, gate, qb))
    if past is None:
        assert t % KEY_TILE == 0 and span == WINDOW + tq and t >= span
        nc_pad = _round_up(t // CMP_STRIDE, LANES)
        kc = _compress(paged3, 0, *cmp_k, nc_pad, False)
        vc_t = _compress(paged3, 1, *cmp_v, nc_pad, True)
        o_a = _nsa(qa3, qar3, gate3, kc, vc_t, paged3, win_new3, tq)
        o_b = _moba(qb3, _kmean(paged3, LANES), paged3, tq)
    else:
        win = jnp.concatenate([win_prev, win_new3, jnp.zeros((b, span - WINDOW - t, 2 * KV_WIDTH), F32)], axis=1)
        o_a, o_b = _sample_attn(*past, paged3, qa3, qar3, gate3, qb3, win, cmp_k, cmp_v)
    x_next = _out(x.reshape(n, d_model), o_a.reshape(n, -1), o_b.reshape(n, -1), za, zb, w_out, g_ple, w_gate,
                  ple.reshape(n, -1), w_proj, g_final, final)
    if transposed:
        new_paged, new_win = (a.reshape(b, -1, KV_HEADS, HEAD_DIM, t).transpose(0, 4, 1, 2, 3) for a in transposed)
    else:
        new_paged, new_win = (a.reshape(b, t, -1, KV_HEADS, HEAD_DIM) for a in (paged3, win_new3))
    return x_next.reshape(b, t, d_model), new_paged, new_win


def _split_w_in(w_in):
    sizes = (NSA_WIDTH, 6 * KV_WIDTH, 3 * NSA_HEADS, NSA_WIDTH, MOBA_WIDTH, 2 * KV_WIDTH, MOBA_WIDTH)
    offs = np.concatenate([[0], np.cumsum(sizes)])
    wqa, wkva, wgate, wza, wqb, wkvb, wzb = [w_in[:, offs[k]:offs[k + 1]].astype(BF16) for k in range(len(sizes))]
    wgate = jnp.pad(wgate, ((0, 0), (0, LANES - wgate.shape[1])))
    return wqa, wkva, wgate, wza, wqb, wkvb, wzb


def kernel(x_prompt, x_sample, cache_paged_kv, cache_win_kv, page_table, p_prompt, p_sample, g_mix, w_in, w_out,
           cmp_pos_k, cmp_w1_k, cmp_w2_k, cmp_pos_v, cmp_w1_v, cmp_w2_v, g_ple, w_ple_gate, w_ple_proj, g_final):
    depth = w_in.shape[0]
    bp, tp, _ = x_prompt.shape
    bs, ts, _ = x_sample.shape
    n_pool, page = cache_paged_kv.shape[1], cache_paged_kv.shape[2]
    past_len = page_table.shape[1] * page
    wbuf = cache_win_kv.shape[2]
    pos_p = jnp.arange(tp, dtype=jnp.int32)
    pos_s = past_len + jnp.arange(ts, dtype=jnp.int32)
    pool_t = cache_paged_kv.transpose(0, 1, 3, 4, 5, 2).reshape(depth * n_pool, PAGED_WIDTH, page)
    g_fin = g_final.reshape(1, -1)
    xp, xs = x_prompt, x_sample
    new_pp, new_pw, new_sp, new_sw = [], [], [], []
    for i in range(depth):
        final = i == depth - 1
        weights = (g_mix[i].reshape(1, -1), _split_w_in(w_in[i]), w_out[i].astype(BF16),
                   _compress_weights(cmp_pos_k[i], cmp_w1_k[i], cmp_w2_k[i]),
                   _compress_weights(cmp_pos_v[i], cmp_w1_v[i], cmp_w2_v[i]),
                   g_ple[i].reshape(1, -1), w_ple_gate[i].astype(BF16), w_ple_proj[i].astype(BF16))
        win0 = jnp.zeros((bp, WINDOW, 2 * KV_WIDTH), F32)
        xp, pk, pw = _mixer_layer(xp, p_prompt[i], pos_p, None, win0, weights, g_fin, final)
        win_prev = jnp.concatenate([jnp.zeros((bs, WINDOW - wbuf, 2 * KV_WIDTH), F32),
                                    cache_win_kv[i].reshape(bs, wbuf, 2 * KV_WIDTH)], axis=1)
        xs, sk, sw = _mixer_layer(xs, p_sample[i], pos_s, (pool_t, page_table, i * n_pool), win_prev, weights,
                                  g_fin, final)
        new_pp.append(pk)
        new_pw.append(pw[:, tp - min(WINDOW, tp):])
        new_sp.append(sk)
        new_sw.append(sw)
    return (xp, xs, jnp.stack(new_pp), jnp.stack(new_pw), jnp.stack(new_sp), jnp.stack(new_sw))
```

```python
import functools

import numpy as np
import jax
import jax.numpy as jnp
from jax import lax
from jax.experimental import pallas as pl
from jax.experimental.pallas import tpu as pltpu

HEAD_DIM = 64
NSA_HEADS = 8
MOBA_HEADS = 8
KV_HEADS = 2
GROUP = NSA_HEADS // KV_HEADS
ROT_DIM = HEAD_DIM // 4
ROPE_THETA = 500000.0
CMP_LEN = 32
CMP_STRIDE = 16
SEL_BLOCK = 64
SEL_TOPK = 8
WINDOW = 512
MOBA_BLOCK = 256
MOBA_TOPK = 3
N_PAGED_SLOTS = 6
RMS_EPS = 1e-6
NEG_INF = -1e30
FORCE_SCORE = 1e4
SCALE = HEAD_DIM ** -0.5
LOG2E = 1.4426950408889634
SEL_SHIFT = SEL_BLOCK.bit_length() - 1
MOBA_SHIFT = MOBA_BLOCK.bit_length() - 1

LANES = 128
KV_WIDTH = KV_HEADS * HEAD_DIM
NSA_WIDTH = NSA_HEADS * HEAD_DIM
MOBA_WIDTH = MOBA_HEADS * HEAD_DIM
PAGED_WIDTH = N_PAGED_SLOTS * KV_WIDTH
KEY_TILE = 256
PAGES_PER_STEP = 32
BIAS_SEG = 2048
VMEM_LIMIT = 56 * 1024 * 1024

F32 = jnp.float32
BF16 = jnp.bfloat16
NT_DIMS = (((1,), (1,)), ((), ()))


def _round_up(n, m):
    return -(-n // m) * m


def _params(sem):
    return pltpu.CompilerParams(dimension_semantics=sem, vmem_limit_bytes=VMEM_LIMIT)


def _rms(x, g):
    return x * lax.rsqrt(jnp.mean(x * x, axis=-1, keepdims=True) + RMS_EPS) * g


def _sigmoid(x):
    return 1.0 / (1.0 + jnp.exp(-x))


def _dot(a, b):
    return jnp.dot(a, b, preferred_element_type=F32)


def _dot_nt(a, b):
    return lax.dot_general(a, b, NT_DIMS, preferred_element_type=F32)


def _dot_nt_exact(a, b):
    return lax.dot_general(a, b, NT_DIMS, preferred_element_type=F32, precision=lax.Precision.HIGHEST)


def _resident(a):
    return pl.BlockSpec(a.shape, lambda *_: (0,) * a.ndim, pipeline_mode=pl.Buffered(1))


def _rope128(v, c, sp, sm):
    half = ROT_DIM // 2
    return v * c + pltpu.roll(v, half, axis=1) * sp + pltpu.roll(v, LANES - half, axis=1) * sm


def _rope(v, c, sp, sm):
    n = v.shape[-1] // LANES
    return jnp.concatenate([_rope128(v[:, i * LANES:(i + 1) * LANES], c, sp, sm) for i in range(n)], axis=-1)


def _proj_kernel(x_ref, g_ref, c_ref, sp_ref, sm_ref, wqa, wkva, wgate, wza, wqb, wkvb, wzb,
                 qa_o, qar_o, paged_o, win_o, gate_o, za_o, qb_o, zb_o, *transposed_o):
    h = _rms(x_ref[...], g_ref[...]).astype(BF16)
    c, sp, sm = c_ref[...], sp_ref[...], sm_ref[...]
    qa = _dot(h, wqa[...])
    qa_o[...] = qa
    qar_o[...] = _rope(qa, c, sp, sm)
    kva = _dot(h, wkva[...])
    paged_o[:, 0:2 * KV_WIDTH] = kva[:, 0:2 * KV_WIDTH]
    paged_o[:, 2 * KV_WIDTH:3 * KV_WIDTH] = _rope(kva[:, 2 * KV_WIDTH:3 * KV_WIDTH], c, sp, sm)
    paged_o[:, 3 * KV_WIDTH:4 * KV_WIDTH] = kva[:, 3 * KV_WIDTH:4 * KV_WIDTH]
    win_o[:, 0:KV_WIDTH] = _rope(kva[:, 4 * KV_WIDTH:5 * KV_WIDTH], c, sp, sm)
    win_o[:, KV_WIDTH:2 * KV_WIDTH] = kva[:, 5 * KV_WIDTH:6 * KV_WIDTH]
    kvb = _dot(h, wkvb[...])
    paged_o[:, 4 * KV_WIDTH:5 * KV_WIDTH] = _rope(kvb[:, 0:KV_WIDTH], c, sp, sm)
    paged_o[:, 5 * KV_WIDTH:6 * KV_WIDTH] = kvb[:, KV_WIDTH:2 * KV_WIDTH]
    gate_o[...] = _sigmoid(_dot(h, wgate[...]))
    za = _dot(h, wza[...])
    za_o[...] = za * _sigmoid(za)
    qb_o[...] = _rope(_dot(h, wqb[...]), c, sp, sm)
    zb = _dot(h, wzb[...])
    zb_o[...] = zb * _sigmoid(zb)
    if transposed_o:
        paged_t_o, win_t_o = transposed_o
        paged_t_o[0] = paged_o[...].T
        win_t_o[0] = win_o[...].T


def _rope_tables(pos, n_rows):
    half = ROT_DIM // 2
    inv_freq = ROPE_THETA ** (-jnp.arange(half, dtype=F32) / half)
    ang = pos.astype(F32)[:, None] * inv_freq[None, :]
    cos, sin = jnp.cos(ang), jnp.sin(ang)
    d = np.arange(LANES) % HEAD_DIM
    idx = d % half
    c = jnp.where(d < ROT_DIM, cos[:, idx], 1.0)
    sp = jnp.where((d >= half) & (d < ROT_DIM), sin[:, idx], 0.0)
    sm = jnp.where(d < half, -sin[:, idx], 0.0)
    rep = n_rows // pos.shape[0]
    return tuple(jnp.tile(t, (rep, 1)) for t in (c, sp, sm))


def _proj(x2d, pos, g, w):
    n, d_model = x2d.shape
    t = pos.shape[0]
    tm = min(256, n)
    tab_rows = max(t, tm)
    tabs = _rope_tables(pos, tab_rows)
    n_tab = tab_rows // tm
    row = lambda width: pl.BlockSpec((tm, width), lambda i: (i, 0))
    tab = pl.BlockSpec((tm, LANES), lambda i: (i % n_tab, 0))
    widths = (NSA_WIDTH, NSA_WIDTH, PAGED_WIDTH, 2 * KV_WIDTH, LANES, NSA_WIDTH, MOBA_WIDTH, MOBA_WIDTH)
    out_specs = [row(wd) for wd in widths]
    out_shape = [jax.ShapeDtypeStruct((n, wd), F32) for wd in widths]
    if t % tm == 0:
        tiles = t // tm
        for wd in (PAGED_WIDTH, 2 * KV_WIDTH):
            out_specs.append(pl.BlockSpec((1, wd, tm), lambda i: (i // tiles, 0, i % tiles)))
            out_shape.append(jax.ShapeDtypeStruct((n // t, wd, t), F32))
    return pl.pallas_call(
        _proj_kernel,
        grid=(n // tm,),
        in_specs=[row(d_model), _resident(g), tab, tab, tab] + [_resident(a) for a in w],
        out_specs=out_specs,
        out_shape=out_shape,
        compiler_params=_params(("parallel",)),
        name="proj",
    )(x2d, g, *tabs, *w)


def _compress_rows(chunk_row, pos_ref, w1_ref, w2_ref, m):
    assert KV_HEADS == 2 and KV_WIDTH == LANES
    hidden = w2_ref.shape[0]
    low = lax.broadcasted_iota(jnp.int32, (1, LANES), 1) < HEAD_DIM
    heads = [[], []]
    for l in range(0, CMP_STRIDE, 2):
        a, b = chunk_row(l), chunk_row(l + 1)
        heads[0].append(jnp.where(low, a, pltpu.roll(b, HEAD_DIM, axis=1)).astype(BF16))
        heads[1].append(jnp.where(low, pltpu.roll(a, HEAD_DIM, axis=1), b).astype(BF16))
    x = jnp.concatenate([jnp.concatenate(h, axis=1) for h in heads] + [pos_ref[...]], axis=0)
    ab = _dot(x, w1_ref[...])
    bias = ab[2 * m:2 * m + 1, 0:hidden] + ab[2 * m + 1:2 * m + 2, hidden:2 * hidden]
    hid = []
    for h in range(KV_HEADS):
        rows = slice(h * m, (h + 1) * m)
        nxt = pltpu.roll(ab[rows, hidden:2 * hidden], m - 1, axis=0)
        pre = ab[rows, 0:hidden] + nxt + bias
        hid.append((pre * _sigmoid(pre)).astype(BF16))
    out = _dot(jnp.concatenate(hid, axis=0), w2_ref[...])
    return [out[h * m:(h + 1) * m] for h in range(KV_HEADS)]


def _compress_kernel(src_ref, pos_ref, w1_ref, w2_ref, o_ref, *, n_chunks, nc_pad, transposed):
    m = n_chunks
    out = jnp.concatenate(
        _compress_rows(lambda l: src_ref[0, pl.ds(l, m, stride=CMP_STRIDE), :], pos_ref, w1_ref, w2_ref, m), axis=1)
    if nc_pad > m:
        out = jnp.concatenate([out, jnp.zeros((nc_pad - m, LANES), F32)], axis=0)
    o_ref[0] = out.T if transposed else out


def _compress(full, slot, pos_rows, w1, w2, nc_pad, transposed):
    b, l, _ = full.shape
    n_chunks = l // CMP_STRIDE
    out_dims = (KV_WIDTH, nc_pad) if transposed else (nc_pad, KV_WIDTH)
    return pl.pallas_call(
        functools.partial(_compress_kernel, n_chunks=n_chunks, nc_pad=nc_pad, transposed=transposed),
        grid=(b,),
        in_specs=[pl.BlockSpec((1, l, KV_WIDTH), lambda i: (i, 0, slot)),
                  _resident(pos_rows), _resident(w1), _resident(w2)],
        out_specs=pl.BlockSpec((1,) + out_dims, lambda i: (i, 0, 0)),
        out_shape=jax.ShapeDtypeStruct((b,) + out_dims, F32),
        compiler_params=_params(("parallel",)),
        name="compress",
    )(full, pos_rows, w1, w2)


def _compress_weights(pos_emb, w1, w2):
    hidden = w1.shape[-1]
    rows = CMP_STRIDE * HEAD_DIM
    w1cat = jnp.concatenate([w1[:CMP_STRIDE].reshape(rows, hidden), w1[CMP_STRIDE:].reshape(rows, hidden)], axis=1)
    pos_rows = jnp.zeros((16, rows), F32)
    pos_rows = pos_rows.at[0].set(pos_emb[:CMP_STRIDE].reshape(-1)).at[1].set(pos_emb[CMP_STRIDE:].reshape(-1))
    return pos_rows.astype(BF16), w1cat.astype(BF16), w2.astype(BF16)


def _kmean_kernel(src_ref, o_ref, *, n_blocks):
    x = src_ref[0].reshape(n_blocks, MOBA_BLOCK, KV_WIDTH)
    o_ref[0, 0:n_blocks, :] = jnp.sum(x, axis=1) * (1.0 / MOBA_BLOCK)
    o_ref[0, n_blocks:, :] = jnp.zeros((o_ref.shape[1] - n_blocks, KV_WIDTH), F32)


def _kmean(full, nb_pad):
    b, l, _ = full.shape
    return pl.pallas_call(
        functools.partial(_kmean_kernel, n_blocks=l // MOBA_BLOCK),
        grid=(b,),
        in_specs=[pl.BlockSpec((1, l, KV_WIDTH), lambda i: (i, 0, 4))],
        out_specs=pl.BlockSpec((1, nb_pad, KV_WIDTH), lambda i: (i, 0, 0)),
        out_shape=jax.ShapeDtypeStruct((b, nb_pad, KV_WIDTH), F32),
        compiler_params=_params(("parallel",)),
        name="kmean",
    )(full)


def _stack_heads(ref, kvh):
    return jnp.concatenate(
        [ref[0, :, (GROUP * kvh + g) * HEAD_DIM:(GROUP * kvh + g + 1) * HEAD_DIM] for g in range(GROUP)], axis=0)


def _masked_softmax(s, valid):
    s = jnp.where(valid, s, NEG_INF)
    e = jnp.where(valid, jnp.exp(s - jnp.max(s, axis=-1, keepdims=True)), 0.0)
    l = jnp.sum(e, axis=-1, keepdims=True)
    return e / jnp.where(l > 0.0, l, 1.0)


def _topk_mask(score, k, axis):
    n = score.shape[axis]
    idx = lax.broadcasted_iota(jnp.int32, score.shape, axis).astype(F32)
    sel = jnp.zeros(score.shape, F32)
    s = score
    for _ in range(k):
        m = jnp.max(s, axis=axis, keepdims=True)
        first = jnp.min(jnp.where(s == m, idx, float(n)), axis=axis, keepdims=True)
        pick = idx == first
        sel = jnp.where(pick & (m > 0.5 * NEG_INF), 1.0, sel)
        s = jnp.where(pick, -3e38, s)
    return sel


def _topk_rows(score, k):
    n = score.shape[0]
    pad = jnp.full((LANES - n, score.shape[1]), NEG_INF, F32)
    return _topk_mask(jnp.concatenate([score, pad], axis=0).T, k, 0).T[0:n]


def _block_bias(chosen):
    return ((chosen - 1.0) * -NEG_INF).astype(BF16)


def _flash_t(q_aug, k_ref, v_ref, diag, t_col, expand_t):
    n_heads = len(q_aug)
    ones = jnp.ones((16, KEY_TILE), BF16)

    def scores(j, live):
        k0 = pl.multiple_of(j * KEY_TILE, KEY_TILE)
        kt = k_ref[0, pl.ds(k0, KEY_TILE), :].astype(BF16)
        lhs = jnp.concatenate([kt, expand_t(j, live)], axis=1)
        return [_dot(lhs, qa) for qa in q_aug]

    def values_t(j):
        k0 = pl.multiple_of(j * KEY_TILE, KEY_TILE)
        vt = v_ref[0, pl.ds(k0, KEY_TILE), :].T.astype(BF16)
        return [jnp.concatenate([vt[h * HEAD_DIM:(h + 1) * HEAD_DIM], ones], axis=0) for h in range(n_heads)]

    visible = diag * KEY_TILE + lax.broadcasted_iota(jnp.int32, (KEY_TILE, 1), 0) <= t_col
    init = []
    for s, vt in zip(scores(diag, True), values_t(diag)):
        s = jnp.where(visible, s, NEG_INF)
        m0 = jnp.max(s, axis=0, keepdims=True)
        init += [m0, _dot(vt, jnp.exp2(s - m0).astype(BF16))]

    def update(stats, s, vt):
        m_i, acc = stats
        m_new = jnp.maximum(m_i, jnp.max(s, axis=0, keepdims=True))
        return [m_new, jnp.exp2(m_i - m_new) * acc + _dot(vt, jnp.exp2(s - m_new).astype(BF16))]

    def body(i, carry):
        ja, jb = 2 * i, 2 * i + 1
        live = jb < diag
        jb = jnp.minimum(jb, diag)
        sa, va = scores(ja, True), values_t(ja)
        sb, vb = scores(jb, live), values_t(jb)
        out = []
        for h in range(n_heads):
            out += update(update(carry[2 * h:2 * h + 2], sa[h], va[h]), sb[h], vb[h])
        return tuple(out)

    res = lax.fori_loop(0, (diag + 1) // 2, body, tuple(init))
    return [res[2 * h + 1][0:HEAD_DIM] / res[2 * h + 1][HEAD_DIM:HEAD_DIM + 1] for h in range(n_heads)]


def _head_cols(x_t, kvh):
    x = jnp.concatenate([x_t[(GROUP * kvh + g) * HEAD_DIM:(GROUP * kvh + g + 1) * HEAD_DIM] for g in range(GROUP)],
                        axis=1)
    return jnp.concatenate([x if k == kvh else jnp.zeros_like(x) for k in range(KV_HEADS)], axis=0)


def _selection_map(n_cmp, nc_pad, ns_pad):
    c0 = np.arange(nc_pad)[:, None] * CMP_STRIDE
    j0 = np.arange(ns_pad)[None, :] * SEL_BLOCK
    ov = np.clip(np.minimum(c0 + CMP_LEN, j0 + SEL_BLOCK) - np.maximum(c0, j0), 0, None) / CMP_STRIDE
    return np.where(np.arange(nc_pad)[:, None] < n_cmp, ov, 0.0).astype(np.float32)


def _window_branch(qr, win_ref, w0, span, lo, w_valid):
    wk = win_ref[0, pl.ds(w0, span), lo:lo + HEAD_DIM].astype(BF16)
    wv = win_ref[0, pl.ds(w0, span), KV_WIDTH + lo:KV_WIDTH + lo + HEAD_DIM].astype(BF16)
    s = jnp.where(w_valid, _dot_nt(qr, wk), NEG_INF)
    e = jnp.exp(s - jnp.max(s, axis=-1, keepdims=True))
    return _dot(e.astype(BF16), wv) / jnp.sum(e, axis=-1, keepdims=True)


def _write_heads(o_ref, kvh, tq, head_out):
    for g in range(GROUP):
        h = GROUP * kvh + g
        o_ref[0, :, h * HEAD_DIM:(h + 1) * HEAD_DIM] = head_out(g, h, slice(g * tq, (g + 1) * tq))


def _nsa_kernel(qa_ref, qar_ref, gate_ref, kc_ref, vct_ref, sk_ref, sv_ref, win_ref, mapt_ref, o_ref, *, tq, span):
    i = pl.program_id(1)
    t0 = i * tq
    r = GROUP * tq
    nc_pad = kc_ref.shape[1]
    ns_pad = mapt_ref.shape[0]
    t_col = t0 + (lax.broadcasted_iota(jnp.int32, (1, r), 1) & (tq - 1))
    c_end = lax.broadcasted_iota(jnp.int32, (nc_pad, 1), 0) * CMP_STRIDE + CMP_LEN
    blk_t = lax.broadcasted_iota(jnp.int32, (ns_pad, 1), 0)
    blk_lane = lax.broadcasted_iota(jnp.int32, (1, ns_pad), 1)
    key_blk = lax.broadcasted_iota(jnp.int32, (KEY_TILE, 1), 0) >> SEL_SHIFT
    jt_t = (t0 + lax.broadcasted_iota(jnp.int32, (1, tq), 1)) >> SEL_SHIFT
    diag = t0 // KEY_TILE
    w0 = pl.multiple_of(jnp.maximum(t0 - WINDOW, 0), 8)
    wpos = w0 + lax.broadcasted_iota(jnp.int32, (span, 1), 0)
    w_valid = (wpos <= t_col) & (wpos >= t_col - WINDOW)
    c_valid = c_end <= t_col + 1

    qa_t = qa_ref[0].T
    qar_t = qar_ref[0].T
    gates_t = gate_ref[0].T
    kc = kc_ref[0].astype(BF16)
    wk = win_ref[0, pl.ds(w0, span), 0:KV_WIDTH].astype(BF16)
    wv_t = win_ref[0, pl.ds(w0, span), KV_WIDTH:2 * KV_WIDTH].T.astype(BF16)
    w_ones = jnp.ones((16, span), BF16)

    def expand_t(j, live):
        return (jnp.where(live, key_blk + j * (KEY_TILE // SEL_BLOCK), ns_pad - 1) == blk_lane).astype(BF16)

    o_cmp_t, o_win_t, qr_t, score_t = [], [], [], []
    for kvh in range(KV_HEADS):
        rows = slice(kvh * HEAD_DIM, (kvh + 1) * HEAD_DIM)
        q_t = (_head_cols(qa_t, kvh) * SCALE).astype(BF16)
        s = jnp.where(c_valid, _dot(kc, q_t), NEG_INF)
        e = jnp.where(c_valid, jnp.exp(s - jnp.max(s, axis=0, keepdims=True)), 0.0)
        l = jnp.sum(e, axis=0, keepdims=True)
        p_cmp = e / jnp.where(l > 0.0, l, 1.0)
        o_cmp_t.append(_dot(vct_ref[0, rows, :].astype(BF16), p_cmp.astype(BF16)))

        p_sum = p_cmp[:, 0:tq]
        for g in range(1, GROUP):
            p_sum = p_sum + p_cmp[:, g * tq:(g + 1) * tq]
        imp_t = jnp.dot(mapt_ref[...], p_sum, preferred_element_type=F32, precision=lax.Precision.HIGHEST)
        score_t.append(jnp.where((blk_t == jt_t) | (blk_t == 0), FORCE_SCORE,
                                 jnp.where(blk_t < jt_t, imp_t, NEG_INF)))

        qr_t.append((_head_cols(qar_t, kvh) * (SCALE * LOG2E)).astype(BF16))
        s = jnp.where(w_valid, _dot(wk, qr_t[kvh]), NEG_INF)
        e = jnp.exp2(s - jnp.max(s, axis=0, keepdims=True))
        o_win = _dot(jnp.concatenate([wv_t[rows], w_ones], axis=0), e.astype(BF16))
        o_win_t.append(o_win[0:HEAD_DIM] / o_win[HEAD_DIM:HEAD_DIM + 1])

    bias_t = _block_bias(_topk_mask(jnp.concatenate(score_t, axis=1), SEL_TOPK, 0))
    q_aug = [jnp.concatenate([qr_t[kvh], jnp.concatenate([bias_t[:, kvh * tq:(kvh + 1) * tq]] * GROUP, axis=1)],
                             axis=0) for kvh in range(KV_HEADS)]

    o_sel_t = _flash_t(q_aug, sk_ref, sv_ref, diag, t_col, expand_t)

    heads = []
    for kvh in range(KV_HEADS):
        for g in range(GROUP):
            h = GROUP * kvh + g
            cols = slice(g * tq, (g + 1) * tq)
            heads.append(gates_t[3 * h:3 * h + 1] * o_cmp_t[kvh][:, cols]
                         + gates_t[3 * h + 1:3 * h + 2] * o_sel_t[kvh][:, cols]
                         + gates_t[3 * h + 2:3 * h + 3] * o_win_t[kvh][:, cols])
    o_ref[0] = jnp.concatenate(heads, axis=0).T


def _nsa(qa, qar, gate, kc, vc_t, full, win, tq):
    b, t, _ = qa.shape
    l = full.shape[1]
    nc_pad = kc.shape[1]
    ns_pad = _round_up(l // SEL_BLOCK + 1, LANES)
    span = _round_up(WINDOW + tq, LANES)
    map_t = jnp.asarray(_selection_map(l // CMP_STRIDE - 1, nc_pad, ns_pad).T)
    qspec = pl.BlockSpec((1, tq, NSA_WIDTH), lambda bi, i: (bi, i, 0))
    seq = lambda a: pl.BlockSpec((1,) + a.shape[1:], lambda bi, i: (bi, 0, 0))
    return pl.pallas_call(
        functools.partial(_nsa_kernel, tq=tq, span=span),
        grid=(b, t // tq),
        in_specs=[
            qspec, qspec,
            pl.BlockSpec((1, tq, LANES), lambda bi, i: (bi, i, 0)),
            seq(kc), seq(vc_t),
            pl.BlockSpec((1, l, KV_WIDTH), lambda bi, i: (bi, 0, 2)),
            pl.BlockSpec((1, l, KV_WIDTH), lambda bi, i: (bi, 0, 3)),
            seq(win),
            _resident(map_t),
        ],
        out_specs=qspec,
        out_shape=jax.ShapeDtypeStruct((b, t, NSA_WIDTH), F32),
        compiler_params=_params(("parallel", "arbitrary")),
        name="nsa",
    )(qa, qar, gate, kc, vc_t, full, full, win, map_t)


def _moba_kernel(qb_ref, km_ref, mk_ref, mv_ref, o_ref, *, tq):
    i = pl.program_id(1)
    t0 = i * tq
    r = GROUP * tq
    nb_pad = km_ref.shape[1]
    t_col = t0 + (lax.broadcasted_iota(jnp.int32, (1, r), 1) & (tq - 1))
    bt_t = t_col >> MOBA_SHIFT
    nblk_t = lax.broadcasted_iota(jnp.int32, (nb_pad, 1), 0)
    nblk_lane = lax.broadcasted_iota(jnp.int32, (1, nb_pad), 1)
    diag = t0 // KEY_TILE
    qb_t = qb_ref[0].T
    km = km_ref[0]

    def expand_t(j, live):
        return jnp.broadcast_to(nblk_lane == jnp.where(live, j, nb_pad - 1), (KEY_TILE, nb_pad)).astype(BF16)

    q_t, gate_t = [], []
    for kvh in range(KV_HEADS):
        q_t.append(_head_cols(qb_t, kvh))
        g_t = jnp.dot(km, q_t[kvh], preferred_element_type=F32, precision=lax.Precision.HIGHEST)
        gate_t.append(jnp.where(nblk_t < bt_t, g_t, NEG_INF))
    chosen_t = _topk_mask(jnp.concatenate(gate_t, axis=1), MOBA_TOPK, 0)
    q_aug = []
    for kvh in range(KV_HEADS):
        own_t = jnp.where(nblk_t == bt_t, 1.0, chosen_t[:, kvh * r:(kvh + 1) * r])
        q_aug.append(jnp.concatenate([(q_t[kvh] * (SCALE * LOG2E)).astype(BF16), _block_bias(own_t)], axis=0))

    o_t = _flash_t(q_aug, mk_ref, mv_ref, diag, t_col, expand_t)
    heads = [o_t[kvh][:, g * tq:(g + 1) * tq] for kvh in range(KV_HEADS) for g in range(GROUP)]
    o_ref[0] = jnp.concatenate(heads, axis=0).T


def _moba(qb, kmean, full, tq):
    b, t, _ = qb.shape
    l = full.shape[1]
    assert l // MOBA_BLOCK < kmean.shape[1]
    qspec = pl.BlockSpec((1, tq, MOBA_WIDTH), lambda bi, i: (bi, i, 0))
    return pl.pallas_call(
        functools.partial(_moba_kernel, tq=tq),
        grid=(b, t // tq),
        in_specs=[
            qspec,
            pl.BlockSpec((1,) + kmean.shape[1:], lambda bi, i: (bi, 0, 0)),
            pl.BlockSpec((1, l, KV_WIDTH), lambda bi, i: (bi, 0, 4)),
            pl.BlockSpec((1, l, KV_WIDTH), lambda bi, i: (bi, 0, 5)),
        ],
        out_specs=qspec,
        out_shape=jax.ShapeDtypeStruct((b, t, MOBA_WIDTH), F32),
        compiler_params=_params(("parallel", "arbitrary")),
        name="moba",
    )(qb, kmean, full, full)


def _softmax_pv_t(s, vt):
    e = jnp.exp(s - jnp.max(s, axis=-1, keepdims=True))
    return _dot_nt(e.astype(BF16), vt) / jnp.sum(e, axis=-1, keepdims=True)


def _segment_bias(bias_rows, e_ref, block, n_keys):
    rows, seg = e_ref.shape
    parts = []
    for s in range(-(-n_keys // seg)):
        n = min(seg, n_keys - s * seg)
        b0 = s * (seg // block)
        parts.append(_dot(bias_rows[:, b0:b0 + rows], e_ref[:, 0:n]))
    return jnp.concatenate(parts, axis=1)


def _block_means_t(kt, n_keys, n_out):
    lane = lax.broadcasted_iota(jnp.int32, (1, n_out), 1)
    out = jnp.zeros((kt.shape[0], n_out), F32)
    for n in range(n_keys // MOBA_BLOCK):
        blk = kt[:, n * MOBA_BLOCK:(n + 1) * MOBA_BLOCK].astype(F32)
        out = jnp.where(lane == n, jnp.sum(blk, axis=1, keepdims=True) * (1.0 / MOBA_BLOCK), out)
    return out


def _sample_kernel(pt_ref, *refs, past_len, ts, span):
    pages = refs[:PAGES_PER_STEP]
    (new_ref, qa_ref, qar_ref, gate_ref, qb_ref, win_ref, map_ref, esel_ref, emoba_ref,
     posk_ref, w1k_ref, w2k_ref, posv_ref, w1v_ref, w2v_ref, oa_ref, ob_ref, tok_scr, kv_scr) = refs[PAGES_PER_STEP:]
    del pt_ref
    step = pl.program_id(1)
    n_steps = pl.num_programs(1)
    page = pages[0].shape[2]
    cmp_rows = 2 * KV_WIDTH

    m = past_len // CMP_STRIDE
    pitch = tok_scr.shape[1] // CMP_STRIDE
    chunks_per_page = page // CMP_STRIDE
    for j in range(PAGES_PER_STEP):
        chunk0 = (step * PAGES_PER_STEP + j) * chunks_per_page
        for slot in range(2):
            x_t = pages[j][0, slot * KV_WIDTH:(slot + 1) * KV_WIDTH, :].T
            for k in range(chunks_per_page):
                tok_scr[slot, pl.ds(chunk0 + k, CMP_STRIDE, stride=pitch), :] = (
                    x_t[k * CMP_STRIDE:(k + 1) * CMP_STRIDE])
    for k in range(past_len // (PAGES_PER_STEP * page)):
        @pl.when(step == k)
        def _(k=k):
            for j in range(PAGES_PER_STEP):
                c0 = (k * PAGES_PER_STEP + j) * page
                kv_scr[:, c0:c0 + page] = pages[j][0, cmp_rows:, :].astype(BF16)

    @pl.when(step == n_steps - 1)
    def _():
        lc = kv_scr.shape[1]
        r = GROUP * ts
        new_pad = jnp.concatenate([new_ref[0], jnp.zeros((page - ts, PAGED_WIDTH), F32)], axis=0)
        kv_scr[:, past_len:past_len + page] = new_pad.T[cmp_rows:, :].astype(BF16)

        kc_heads = _compress_rows(lambda l: tok_scr[0, l * pitch:l * pitch + m, :], posk_ref, w1k_ref, w2k_ref, m)
        vc_heads = _compress_rows(lambda l: tok_scr[1, l * pitch:l * pitch + m, :], posv_ref, w1v_ref, w2v_ref, m)

        ns_pad = map_ref.shape[1]
        nb_pad = LANES
        row = lax.broadcasted_iota(jnp.int32, (r, 1), 0)
        t_row = past_len + (row & (ts - 1))
        t_tok = past_len + lax.broadcasted_iota(jnp.int32, (ts, 1), 0)
        new_visible = past_len + lax.broadcasted_iota(jnp.int32, (1, page), 1) <= t_row

        def causal(s):
            return jnp.concatenate([s[:, 0:past_len], jnp.where(new_visible, s[:, past_len:], NEG_INF)], axis=1)

        c_end = lax.broadcasted_iota(jnp.int32, (1, m), 1) * CMP_STRIDE + CMP_LEN
        blk = lax.broadcasted_iota(jnp.int32, (1, ns_pad), 1)
        nblk = lax.broadcasted_iota(jnp.int32, (1, nb_pad), 1)
        jt = t_tok >> SEL_SHIFT
        bt = t_row >> MOBA_SHIFT
        wpos = past_len - WINDOW + lax.broadcasted_iota(jnp.int32, (1, span), 1)
        w_valid = (wpos >= 0) & (wpos <= t_row) & (wpos >= t_row - WINDOW)
        gates = gate_ref[0]

        def kv_rows(slot, kvh):
            lo = slot * KV_WIDTH + kvh * HEAD_DIM
            return kv_scr[lo:lo + HEAD_DIM, :]

        o_cmp, scores, gate_s, qf = [], [], [], []
        for kvh in range(KV_HEADS):
            q = (_stack_heads(qa_ref, kvh) * SCALE).astype(BF16)
            p_cmp = _masked_softmax(_dot_nt(q, kc_heads[kvh].astype(BF16)), c_end <= t_row + 1)
            o_cmp.append(_dot(p_cmp.astype(BF16), vc_heads[kvh].astype(BF16)))
            p_sum = p_cmp[0:ts]
            for g in range(1, GROUP):
                p_sum = p_sum + p_cmp[g * ts:(g + 1) * ts]
            imp = jnp.dot(p_sum, map_ref[...], preferred_element_type=F32, precision=lax.Precision.HIGHEST)
            scores.append(jnp.where((blk == jt) | (blk == 0), FORCE_SCORE, jnp.where(blk < jt, imp, NEG_INF)))
            qf.append(_stack_heads(qb_ref, kvh))
            g_s = jnp.dot(qf[kvh], _block_means_t(kv_rows(2, kvh), past_len, nb_pad), preferred_element_type=F32,
                          precision=lax.Precision.HIGHEST)
            gate_s.append(jnp.where(nblk < bt, g_s, NEG_INF))
        sel = _topk_rows(jnp.concatenate(scores, axis=0), SEL_TOPK)
        chosen = _topk_rows(jnp.concatenate(gate_s, axis=0), MOBA_TOPK)

        for kvh in range(KV_HEADS):
            lo = kvh * HEAD_DIM
            qr = (_stack_heads(qar_ref, kvh) * SCALE).astype(BF16)
            bias_rows = _block_bias(jnp.concatenate([sel[kvh * ts:(kvh + 1) * ts]] * GROUP, axis=0))
            s = _dot(qr, kv_rows(0, kvh)) + _segment_bias(bias_rows, esel_ref, SEL_BLOCK, lc)
            o_sel = _softmax_pv_t(causal(s), kv_rows(1, kvh))
            o_win = _window_branch(qr, win_ref, 0, span, lo, w_valid)
            _write_heads(oa_ref, kvh, ts, lambda g, h, rows: (
                gates[:, 3 * h:3 * h + 1] * o_cmp[kvh][rows] + gates[:, 3 * h + 1:3 * h + 2] * o_sel[rows]
                + gates[:, 3 * h + 2:3 * h + 3] * o_win[rows]))

            own = jnp.where(nblk == bt, 1.0, chosen[kvh * r:(kvh + 1) * r])
            s = (_dot((qf[kvh] * SCALE).astype(BF16), kv_rows(2, kvh))
                 + _segment_bias(_block_bias(own), emoba_ref, MOBA_BLOCK, lc))
            o = _softmax_pv_t(causal(s), kv_rows(3, kvh))
            _write_heads(ob_ref, kvh, ts, lambda g, h, rows: o[rows])


def _block_expansion(n_rows, block, n_keys):
    return jnp.asarray(np.arange(n_rows)[:, None] == (np.arange(n_keys)[None, :] // block), dtype=BF16)


def _sample_attn(pool_t, page_table, page_base, new_rows, qa, qar, gate, qb, win, cmp_k, cmp_v):
    bs, n_pages = page_table.shape
    page = pool_t.shape[2]
    ts = new_rows.shape[1]
    past_len = n_pages * page
    n_steps = n_pages // PAGES_PER_STEP
    assert page == LANES and n_pages % PAGES_PER_STEP == 0 and ts < CMP_STRIDE and ts & (ts - 1) == 0
    assert past_len % MOBA_BLOCK == 0 and (past_len // CMP_STRIDE) % LANES == 0
    lc = past_len + page
    ns_pad = _round_up(lc // SEL_BLOCK, LANES)
    assert lc // MOBA_BLOCK + 1 <= LANES and past_len % BIAS_SEG == 0
    span = win.shape[1]
    sel_map = jnp.asarray(_selection_map(past_len // CMP_STRIDE - 1, past_len // CMP_STRIDE, ns_pad))
    e_sel = _block_expansion(BIAS_SEG // SEL_BLOCK, SEL_BLOCK, BIAS_SEG)
    e_moba = _block_expansion(2 * (BIAS_SEG // MOBA_BLOCK), MOBA_BLOCK, BIAS_SEG)

    def page_spec(j):
        return pl.BlockSpec((1, PAGED_WIDTH, page),
                            lambda b, s, pt: (page_base + pt[b, s * PAGES_PER_STEP + j], 0, 0))

    per_seq = lambda a: pl.BlockSpec((1,) + a.shape[1:], lambda b, s, pt: (b, 0, 0))
    consts = (sel_map, e_sel, e_moba) + tuple(cmp_k) + tuple(cmp_v)
    out_spec = pl.BlockSpec((1, ts, NSA_WIDTH), lambda b, s, pt: (b, 0, 0))
    grid_spec = pltpu.PrefetchScalarGridSpec(
        num_scalar_prefetch=1,
        grid=(bs, n_steps),
        in_specs=[page_spec(j) for j in range(PAGES_PER_STEP)]
        + [per_seq(a) for a in (new_rows, qa, qar, gate, qb, win)] + [_resident(a) for a in consts],
        out_specs=[out_spec, out_spec],
        scratch_shapes=[pltpu.VMEM((2, CMP_STRIDE * (past_len // CMP_STRIDE + 8), KV_WIDTH), F32),
                        pltpu.VMEM((PAGED_WIDTH - 2 * KV_WIDTH, lc), BF16)],
    )
    return pl.pallas_call(
        functools.partial(_sample_kernel, past_len=past_len, ts=ts, span=span),
        grid_spec=grid_spec,
        out_shape=[jax.ShapeDtypeStruct((bs, ts, NSA_WIDTH), F32)] * 2,
        compiler_params=_params(("parallel", "arbitrary")),
        name="sample_attn",
    )(page_table, *([pool_t] * PAGES_PER_STEP), new_rows, qa, qar, gate, qb, win, *consts)


def _out_kernel(x_ref, oa_ref, ob_ref, za_ref, zb_ref, wout_ref, gple_ref, wg_ref, ple_ref, wp_ref, gfin_ref,
                o_ref, *, final):
    mixed = jnp.concatenate([oa_ref[...] * za_ref[...], ob_ref[...] * zb_ref[...]], axis=-1).astype(BF16)
    x1 = x_ref[...] + _dot(mixed, wout_ref[...])
    gate = _sigmoid(_dot(_rms(x1, gple_ref[...]).astype(BF16), wg_ref[...]))
    x2 = x1 + gate * _dot(ple_ref[...].astype(BF16), wp_ref[...])
    o_ref[...] = _rms(x2, gfin_ref[...]) if final else x2


def _out(x2d, oa, ob, za, zb, w_out, g_ple, w_gate, ple, w_proj, g_final, final):
    n, d_model = x2d.shape
    tm = min(256, n)
    row = lambda a: pl.BlockSpec((tm, a.shape[1]), lambda i: (i, 0))
    args = (x2d, oa, ob, za, zb, w_out, g_ple, w_gate, ple, w_proj, g_final)
    specs = [row(x2d), row(oa), row(ob), row(za), row(zb), _resident(w_out), _resident(g_ple), _resident(w_gate),
             row(ple), _resident(w_proj), _resident(g_final)]
    return pl.pallas_call(
        functools.partial(_out_kernel, final=final),
        grid=(n // tm,),
        in_specs=specs,
        out_specs=row(x2d),
        out_shape=jax.ShapeDtypeStruct((n, d_model), F32),
        compiler_params=_params(("parallel",)),
        name="out",
    )(*args)


def _mixer_layer(x, ple, pos, past, win_prev, weights, g_final, final):
    (g_mix, w_split, w_out, cmp_k, cmp_v, g_ple, w_gate, w_proj) = weights
    b, t, d_model = x.shape
    n = b * t
    qa, qar, paged, win_new, gate, za, qb, zb, *transposed = _proj(x.reshape(n, d_model), pos, g_mix, w_split)
    paged3 = paged.reshape(b, t, PAGED_WIDTH)
    win_new3 = win_new.reshape(b, t, 2 * KV_WIDTH)
    tq = min(KEY_TILE, t)
    assert KEY_TILE % tq == 0 and tq & (tq - 1) == 0
    span = _round_up(WINDOW + tq, LANES)
    qa3, qar3, gate3, qb3 = (a.reshape(b, t, -1) for a in (qa, qar, gate, qb))
    if past is None:
        assert t % KEY_TILE == 0 and span == WINDOW + tq and t >= span
        nc_pad = _round_up(t // CMP_STRIDE, LANES)
        kc = _compress(paged3, 0, *cmp_k, nc_pad, False)
        vc_t = _compress(paged3, 1, *cmp_v, nc_pad, True)
        o_a = _nsa(qa3, qar3, gate3, kc, vc_t, paged3, win_new3, tq)
        o_b = _moba(qb3, _kmean(paged3, LANES), paged3, tq)
    else:
        win = jnp.concatenate([win_prev, win_new3, jnp.zeros((b, span - WINDOW - t, 2 * KV_WIDTH), F32)], axis=1)
        o_a, o_b = _sample_attn(*past, paged3, qa3, qar3, gate3, qb3, win, cmp_k, cmp_v)
    x_next = _out(x.reshape(n, d_model), o_a.reshape(n, -1), o_b.reshape(n, -1), za, zb, w_out, g_ple, w_gate,
                  ple.reshape(n, -1), w_proj, g_final, final)
    if transposed:
        new_paged, new_win = (a.reshape(b, -1, KV_HEADS, HEAD_DIM, t).transpose(0, 4, 1, 2, 3) for a in transposed)
    else:
        new_paged, new_win = (a.reshape(b, t, -1, KV_HEADS, HEAD_DIM) for a in (paged3, win_new3))
    return x_next.reshape(b, t, d_model), new_paged, new_win


def _split_w_in(w_in):
    sizes = (NSA_WIDTH, 6 * KV_WIDTH, 3 * NSA_HEADS, NSA_WIDTH, MOBA_WIDTH, 2 * KV_WIDTH, MOBA_WIDTH)
    offs = np.concatenate([[0], np.cumsum(sizes)])
    wqa, wkva, wgate, wza, wqb, wkvb, wzb = [w_in[:, offs[k]:offs[k + 1]].astype(BF16) for k in range(len(sizes))]
    wgate = jnp.pad(wgate, ((0, 0), (0, LANES - wgate.shape[1])))
    return wqa, wkva, wgate, wza, wqb, wkvb, wzb


def kernel(x_prompt, x_sample, cache_paged_kv, cache_win_kv, page_table, p_prompt, p_sample, g_mix, w_in, w_out,
           cmp_pos_k, cmp_w1_k, cmp_w2_k, cmp_pos_v, cmp_w1_v, cmp_w2_v, g_ple, w_ple_gate, w_ple_proj, g_final):
    depth = w_in.shape[0]
    bp, tp, _ = x_prompt.shape
    bs, ts, _ = x_sample.shape
    n_pool, page = cache_paged_kv.shape[1], cache_paged_kv.shape[2]
    past_len = page_table.shape[1] * page
    wbuf = cache_win_kv.shape[2]
    pos_p = jnp.arange(tp, dtype=jnp.int32)
    pos_s = past_len + jnp.arange(ts, dtype=jnp.int32)
    pool_t = cache_paged_kv.transpose(0, 1, 3, 4, 5, 2).reshape(depth * n_pool, PAGED_WIDTH, page)
    g_fin = g_final.reshape(1, -1)
    xp, xs = x_prompt, x_sample
    new_pp, new_pw, new_sp, new_sw = [], [], [], []
    for i in range(depth):
        final = i == depth - 1
        weights = (g_mix[i].reshape(1, -1), _split_w_in(w_in[i]), w_out[i].astype(BF16),
                   _compress_weights(cmp_pos_k[i], cmp_w1_k[i], cmp_w2_k[i]),
                   _compress_weights(cmp_pos_v[i], cmp_w1_v[i], cmp_w2_v[i]),
                   g_ple[i].reshape(1, -1), w_ple_gate[i].astype(BF16), w_ple_proj[i].astype(BF16))
        win0 = jnp.zeros((bp, WINDOW, 2 * KV_WIDTH), F32)
        xp, pk, pw = _mixer_layer(xp, p_prompt[i], pos_p, None, win0, weights, g_fin, final)
        win_prev = jnp.concatenate([jnp.zeros((bs, WINDOW - wbuf, 2 * KV_WIDTH), F32),
                                    cache_win_kv[i].reshape(bs, wbuf, 2 * KV_WIDTH)], axis=1)
        xs, sk, sw = _mixer_layer(xs, p_sample[i], pos_s, (pool_t, page_table, i * n_pool), win_prev, weights,
                                  g_fin, final)
        new_pp.append(pk)
        new_pw.append(pw[:, tp - min(WINDOW, tp):])
        new_sp.append(sk)
        new_sw.append(sw)
    return (xp, xs, jnp.stack(new_pp), jnp.stack(new_pw), jnp.stack(new_sp), jnp.stack(new_sw))
```

```python
import functools

import numpy as np
import jax
import jax.numpy as jnp
from jax import lax
from jax.experimental import pallas as pl
from jax.experimental.pallas import tpu as pltpu

HEAD_DIM = 64
NSA_HEADS = 8
MOBA_HEADS = 8
KV_HEADS = 2
GROUP = NSA_HEADS // KV_HEADS
ROT_DIM = HEAD_DIM // 4
ROPE_THETA = 500000.0
CMP_LEN = 32
CMP_STRIDE = 16
SEL_BLOCK = 64
SEL_TOPK = 8
WINDOW = 512
MOBA_BLOCK = 256
MOBA_TOPK = 3
N_PAGED_SLOTS = 6
RMS_EPS = 1e-6
NEG_INF = -1e30
FORCE_SCORE = 1e4
SCALE = HEAD_DIM ** -0.5
LOG2E = 1.4426950408889634
SEL_SHIFT = SEL_BLOCK.bit_length() - 1
MOBA_SHIFT = MOBA_BLOCK.bit_length() - 1

LANES = 128
KV_WIDTH = KV_HEADS * HEAD_DIM
NSA_WIDTH = NSA_HEADS * HEAD_DIM
MOBA_WIDTH = MOBA_HEADS * HEAD_DIM
PAGED_WIDTH = N_PAGED_SLOTS * KV_WIDTH
KEY_TILE = 256
PAGES_PER_STEP = 32
BIAS_SEG = 2048
VMEM_LIMIT = 56 * 1024 * 1024

F32 = jnp.float32
BF16 = jnp.bfloat16
NT_DIMS = (((1,), (1,)), ((), ()))


def _round_up(n, m):
    return -(-n // m) * m


def _params(sem):
    return pltpu.CompilerParams(dimension_semantics=sem, vmem_limit_bytes=VMEM_LIMIT)


def _rms(x, g):
    return x * lax.rsqrt(jnp.mean(x * x, axis=-1, keepdims=True) + RMS_EPS) * g


def _sigmoid(x):
    return 1.0 / (1.0 + jnp.exp(-x))


def _dot(a, b):
    return jnp.dot(a, b, preferred_element_type=F32)


def _dot_nt(a, b):
    return lax.dot_general(a, b, NT_DIMS, preferred_element_type=F32)


def _dot_nt_exact(a, b):
    return lax.dot_general(a, b, NT_DIMS, preferred_element_type=F32, precision=lax.Precision.HIGHEST)


def _resident(a):
    return pl.BlockSpec(a.shape, lambda *_: (0,) * a.ndim, pipeline_mode=pl.Buffered(1))


def _rope128(v, c, sp, sm):
    half = ROT_DIM // 2
    return v * c + pltpu.roll(v, half, axis=1) * sp + pltpu.roll(v, LANES - half, axis=1) * sm


def _rope(v, c, sp, sm):
    n = v.shape[-1] // LANES
    return jnp.concatenate([_rope128(v[:, i * LANES:(i + 1) * LANES], c, sp, sm) for i in range(n)], axis=-1)


def _proj_kernel(x_ref, g_ref, c_ref, sp_ref, sm_ref, wqa, wkva, wgate, wza, wqb, wkvb, wzb,
                 qa_o, qar_o, paged_o, win_o, gate_o, za_o, qb_o, zb_o, *transposed_o):
    h = _rms(x_ref[...], g_ref[...]).astype(BF16)
    c, sp, sm = c_ref[...], sp_ref[...], sm_ref[...]
    qa = _dot(h, wqa[...])
    qa_o[...] = qa
    qar_o[...] = _rope(qa, c, sp, sm)
    kva = _dot(h, wkva[...])
    paged_o[:, 0:2 * KV_WIDTH] = kva[:, 0:2 * KV_WIDTH]
    paged_o[:, 2 * KV_WIDTH:3 * KV_WIDTH] = _rope(kva[:, 2 * KV_WIDTH:3 * KV_WIDTH], c, sp, sm)
    paged_o[:, 3 * KV_WIDTH:4 * KV_WIDTH] = kva[:, 3 * KV_WIDTH:4 * KV_WIDTH]
    win_o[:, 0:KV_WIDTH] = _rope(kva[:, 4 * KV_WIDTH:5 * KV_WIDTH], c, sp, sm)
    win_o[:, KV_WIDTH:2 * KV_WIDTH] = kva[:, 5 * KV_WIDTH:6 * KV_WIDTH]
    kvb = _dot(h, wkvb[...])
    paged_o[:, 4 * KV_WIDTH:5 * KV_WIDTH] = _rope(kvb[:, 0:KV_WIDTH], c, sp, sm)
    paged_o[:, 5 * KV_WIDTH:6 * KV_WIDTH] = kvb[:, KV_WIDTH:2 * KV_WIDTH]
    gate_o[...] = _sigmoid(_dot(h, wgate[...]))
    za = _dot(h, wza[...])
    za_o[...] = za * _sigmoid(za)
    qb_o[...] = _rope(_dot(h, wqb[...]), c, sp, sm)
    zb = _dot(h, wzb[...])
    zb_o[...] = zb * _sigmoid(zb)
    if transposed_o:
        paged_t_o, win_t_o = transposed_o
        paged_t_o[0] = paged_o[...].T
        win_t_o[0] = win_o[...].T


def _rope_tables(pos, n_rows):
    half = ROT_DIM // 2
    inv_freq = ROPE_THETA ** (-jnp.arange(half, dtype=F32) / half)
    ang = pos.astype(F32)[:, None] * inv_freq[None, :]
    cos, sin = jnp.cos(ang), jnp.sin(ang)
    d = np.arange(LANES) % HEAD_DIM
    idx = d % half
    c = jnp.where(d < ROT_DIM, cos[:, idx], 1.0)
    sp = jnp.where((d >= half) & (d < ROT_DIM), sin[:, idx], 0.0)
    sm = jnp.where(d < half, -sin[:, idx], 0.0)
    rep = n_rows // pos.shape[0]
    return tuple(jnp.tile(t, (rep, 1)) for t in (c, sp, sm))


def _proj(x2d, pos, g, w):
    n, d_model = x2d.shape
    t = pos.shape[0]
    tm = min(256, n)
    tab_rows = max(t, tm)
    tabs = _rope_tables(pos, tab_rows)
    n_tab = tab_rows // tm
    row = lambda width: pl.BlockSpec((tm, width), lambda i: (i, 0))
    tab = pl.BlockSpec((tm, LANES), lambda i: (i % n_tab, 0))
    widths = (NSA_WIDTH, NSA_WIDTH, PAGED_WIDTH, 2 * KV_WIDTH, LANES, NSA_WIDTH, MOBA_WIDTH, MOBA_WIDTH)
    out_specs = [row(wd) for wd in widths]
    out_shape = [jax.ShapeDtypeStruct((n, wd), F32) for wd in widths]
    if t % tm == 0:
        tiles = t // tm
        for wd in (PAGED_WIDTH, 2 * KV_WIDTH):
            out_specs.append(pl.BlockSpec((1, wd, tm), lambda i: (i // tiles, 0, i % tiles)))
            out_shape.append(jax.ShapeDtypeStruct((n // t, wd, t), F32))
    return pl.pallas_call(
        _proj_kernel,
        grid=(n // tm,),
        in_specs=[row(d_model), _resident(g), tab, tab, tab] + [_resident(a) for a in w],
        out_specs=out_specs,
        out_shape=out_shape,
        compiler_params=_params(("parallel",)),
        name="proj",
    )(x2d, g, *tabs, *w)


def _compress_rows(chunk_row, pos_ref, w1_ref, w2_ref, m):
    assert KV_HEADS == 2 and KV_WIDTH == LANES
    hidden = w2_ref.shape[0]
    low = lax.broadcasted_iota(jnp.int32, (1, LANES), 1) < HEAD_DIM
    heads = [[], []]
    for l in range(0, CMP_STRIDE, 2):
        a, b = chunk_row(l), chunk_row(l + 1)
        heads[0].append(jnp.where(low, a, pltpu.roll(b, HEAD_DIM, axis=1)).astype(BF16))
        heads[1].append(jnp.where(low, pltpu.roll(a, HEAD_DIM, axis=1), b).astype(BF16))
    x = jnp.concatenate([jnp.concatenate(h, axis=1) for h in heads] + [pos_ref[...]], axis=0)
    ab = _dot(x, w1_ref[...])
    bias = ab[2 * m:2 * m + 1, 0:hidden] + ab[2 * m + 1:2 * m + 2, hidden:2 * hidden]
    hid = []
    for h in range(KV_HEADS):
        rows = slice(h * m, (h + 1) * m)
        nxt = pltpu.roll(ab[rows, hidden:2 * hidden], m - 1, axis=0)
        pre = ab[rows, 0:hidden] + nxt + bias
        hid.append((pre * _sigmoid(pre)).astype(BF16))
    out = _dot(jnp.concatenate(hid, axis=0), w2_ref[...])
    return [out[h * m:(h + 1) * m] for h in range(KV_HEADS)]


def _compress_kernel(src_ref, pos_ref, w1_ref, w2_ref, o_ref, *, n_chunks, nc_pad, transposed):
    m = n_chunks
    out = jnp.concatenate(
        _compress_rows(lambda l: src_ref[0, pl.ds(l, m, stride=CMP_STRIDE), :], pos_ref, w1_ref, w2_ref, m), axis=1)
    if nc_pad > m:
        out = jnp.concatenate([out, jnp.zeros((nc_pad - m, LANES), F32)], axis=0)
    o_ref[0] = out.T if transposed else out


def _compress(full, slot, pos_rows, w1, w2, nc_pad, transposed):
    b, l, _ = full.shape
    n_chunks = l // CMP_STRIDE
    out_dims = (KV_WIDTH, nc_pad) if transposed else (nc_pad, KV_WIDTH)
    return pl.pallas_call(
        functools.partial(_compress_kernel, n_chunks=n_chunks, nc_pad=nc_pad, transposed=transposed),
        grid=(b,),
        in_specs=[pl.BlockSpec((1, l, KV_WIDTH), lambda i: (i, 0, slot)),
                  _resident(pos_rows), _resident(w1), _resident(w2)],
        out_specs=pl.BlockSpec((1,) + out_dims, lambda i: (i, 0, 0)),
        out_shape=jax.ShapeDtypeStruct((b,) + out_dims, F32),
        compiler_params=_params(("parallel",)),
        name="compress",
    )(full, pos_rows, w1, w2)


def _compress_weights(pos_emb, w1, w2):
    hidden = w1.shape[-1]
    rows = CMP_STRIDE * HEAD_DIM
    w1cat = jnp.concatenate([w1[:CMP_STRIDE].reshape(rows, hidden), w1[CMP_STRIDE:].reshape(rows, hidden)], axis=1)
    pos_rows = jnp.zeros((16, rows), F32)
    pos_rows = pos_rows.at[0].set(pos_emb[:CMP_STRIDE].reshape(-1)).at[1].set(pos_emb[CMP_STRIDE:].reshape(-1))
    return pos_rows.astype(BF16), w1cat.astype(BF16), w2.astype(BF16)


def _kmean_kernel(src_ref, o_ref, *, n_blocks):
    x = src_ref[0].reshape(n_blocks, MOBA_BLOCK, KV_WIDTH)
    o_ref[0, 0:n_blocks, :] = jnp.sum(x, axis=1) * (1.0 / MOBA_BLOCK)
    o_ref[0, n_blocks:, :] = jnp.zeros((o_ref.shape[1] - n_blocks, KV_WIDTH), F32)


def _kmean(full, nb_pad):
    b, l, _ = full.shape
    return pl.pallas_call(
        functools.partial(_kmean_kernel, n_blocks=l // MOBA_BLOCK),
        grid=(b,),
        in_specs=[pl.BlockSpec((1, l, KV_WIDTH), lambda i: (i, 0, 4))],
        out_specs=pl.BlockSpec((1, nb_pad, KV_WIDTH), lambda i: (i, 0, 0)),
        out_shape=jax.ShapeDtypeStruct((b, nb_pad, KV_WIDTH), F32),
        compiler_params=_params(("parallel",)),
        name="kmean",
    )(full)


def _stack_heads(ref, kvh):
    return jnp.concatenate(
        [ref[0, :, (GROUP * kvh + g) * HEAD_DIM:(GROUP * kvh + g + 1) * HEAD_DIM] for g in range(GROUP)], axis=0)


def _masked_softmax(s, valid):
    s = jnp.where(valid, s, NEG_INF)
    e = jnp.where(valid, jnp.exp(s - jnp.max(s, axis=-1, keepdims=True)), 0.0)
    l = jnp.sum(e, axis=-1, keepdims=True)
    return e / jnp.where(l > 0.0, l, 1.0)


def _topk_mask(score, k, axis):
    n = score.shape[axis]
    idx = lax.broadcasted_iota(jnp.int32, score.shape, axis).astype(F32)
    sel = jnp.zeros(score.shape, F32)
    s = score
    for _ in range(k):
        m = jnp.max(s, axis=axis, keepdims=True)
        first = jnp.min(jnp.where(s == m, idx, float(n)), axis=axis, keepdims=True)
        pick = idx == first
        sel = jnp.where(pick & (m > 0.5 * NEG_INF), 1.0, sel)
        s = jnp.where(pick, -3e38, s)
    return sel


def _topk_rows(score, k):
    n = score.shape[0]
    pad = jnp.full((LANES - n, score.shape[1]), NEG_INF, F32)
    return _topk_mask(jnp.concatenate([score, pad], axis=0).T, k, 0).T[0:n]


def _block_bias(chosen):
    return ((chosen - 1.0) * -NEG_INF).astype(BF16)


def _flash_t(q_aug, k_ref, v_ref, diag, t_col, expand_t):
    n_heads = len(q_aug)
    ones = jnp.ones((16, KEY_TILE), BF16)

    def scores(j, live):
        k0 = pl.multiple_of(j * KEY_TILE, KEY_TILE)
        kt = k_ref[0, pl.ds(k0, KEY_TILE), :].astype(BF16)
        lhs = jnp.concatenate([kt, expand_t(j, live)], axis=1)
        return [_dot(lhs, qa) for qa in q_aug]

    def values_t(j):
        k0 = pl.multiple_of(j * KEY_TILE, KEY_TILE)
        vt = v_ref[0, :, pl.ds(k0, KEY_TILE)].astype(BF16)
        return [jnp.concatenate([vt[h * HEAD_DIM:(h + 1) * HEAD_DIM], ones], axis=0) for h in range(n_heads)]

    visible = diag * KEY_TILE + lax.broadcasted_iota(jnp.int32, (KEY_TILE, 1), 0) <= t_col
    init = []
    for s, vt in zip(scores(diag, True), values_t(diag)):
        s = jnp.where(visible, s, NEG_INF)
        m0 = jnp.max(s, axis=0, keepdims=True)
        init += [m0, _dot(vt, jnp.exp2(s - m0).astype(BF16))]

    def update(stats, s, vt):
        m_i, acc = stats
        m_new = jnp.maximum(m_i, jnp.max(s, axis=0, keepdims=True))
        return [m_new, jnp.exp2(m_i - m_new) * acc + _dot(vt, jnp.exp2(s - m_new).astype(BF16))]

    def body(i, carry):
        ja, jb = 2 * i, 2 * i + 1
        live = jb < diag
        jb = jnp.minimum(jb, diag)
        sa, va = scores(ja, True), values_t(ja)
        sb, vb = scores(jb, live), values_t(jb)
        out = []
        for h in range(n_heads):
            out += update(update(carry[2 * h:2 * h + 2], sa[h], va[h]), sb[h], vb[h])
        return tuple(out)

    res = lax.fori_loop(0, (diag + 1) // 2, body, tuple(init))
    return [res[2 * h + 1][0:HEAD_DIM] / res[2 * h + 1][HEAD_DIM:HEAD_DIM + 1] for h in range(n_heads)]


def _head_cols(x_t, kvh):
    x = jnp.concatenate([x_t[(GROUP * kvh + g) * HEAD_DIM:(GROUP * kvh + g + 1) * HEAD_DIM] for g in range(GROUP)],
                        axis=1)
    return jnp.concatenate([x if k == kvh else jnp.zeros_like(x) for k in range(KV_HEADS)], axis=0)


def _selection_map(n_cmp, nc_pad, ns_pad):
    c0 = np.arange(nc_pad)[:, None] * CMP_STRIDE
    j0 = np.arange(ns_pad)[None, :] * SEL_BLOCK
    ov = np.clip(np.minimum(c0 + CMP_LEN, j0 + SEL_BLOCK) - np.maximum(c0, j0), 0, None) / CMP_STRIDE
    return np.where(np.arange(nc_pad)[:, None] < n_cmp, ov, 0.0).astype(np.float32)


def _window_branch(qr, win_ref, w0, span, lo, w_valid):
    wk = win_ref[0, pl.ds(w0, span), lo:lo + HEAD_DIM].astype(BF16)
    wv = win_ref[0, pl.ds(w0, span), KV_WIDTH + lo:KV_WIDTH + lo + HEAD_DIM].astype(BF16)
    s = jnp.where(w_valid, _dot_nt(qr, wk), NEG_INF)
    e = jnp.exp(s - jnp.max(s, axis=-1, keepdims=True))
    return _dot(e.astype(BF16), wv) / jnp.sum(e, axis=-1, keepdims=True)


def _write_heads(o_ref, kvh, tq, head_out):
    for g in range(GROUP):
        h = GROUP * kvh + g
        o_ref[0, :, h * HEAD_DIM:(h + 1) * HEAD_DIM] = head_out(g, h, slice(g * tq, (g + 1) * tq))


def _nsa_kernel(qa_ref, qar_ref, gate_ref, kc_ref, vct_ref, sk_ref, sv_ref, win_ref, mapt_ref, o_ref, *, tq, span):
    i = pl.program_id(1)
    t0 = i * tq
    r = GROUP * tq
    nc_pad = kc_ref.shape[1]
    ns_pad = mapt_ref.shape[0]
    t_col = t0 + (lax.broadcasted_iota(jnp.int32, (1, r), 1) & (tq - 1))
    c_end = lax.broadcasted_iota(jnp.int32, (nc_pad, 1), 0) * CMP_STRIDE + CMP_LEN
    blk_t = lax.broadcasted_iota(jnp.int32, (ns_pad, 1), 0)
    blk_lane = lax.broadcasted_iota(jnp.int32, (1, ns_pad), 1)
    key_blk = lax.broadcasted_iota(jnp.int32, (KEY_TILE, 1), 0) >> SEL_SHIFT
    jt_t = (t0 + lax.broadcasted_iota(jnp.int32, (1, tq), 1)) >> SEL_SHIFT
    diag = t0 // KEY_TILE
    w0 = pl.multiple_of(jnp.maximum(t0 - WINDOW, 0), 8)
    wpos = w0 + lax.broadcasted_iota(jnp.int32, (span, 1), 0)
    w_valid = (wpos <= t_col) & (wpos >= t_col - WINDOW)
    c_valid = c_end <= t_col + 1

    qa_t = qa_ref[0].T
    qar_t = qar_ref[0].T
    gates_t = gate_ref[0].T
    kc = kc_ref[0].astype(BF16)
    wk = win_ref[0, pl.ds(w0, span), 0:KV_WIDTH].astype(BF16)
    wv_t = win_ref[0, pl.ds(w0, span), KV_WIDTH:2 * KV_WIDTH].T.astype(BF16)
    w_ones = jnp.ones((16, span), BF16)

    def expand_t(j, live):
        return (jnp.where(live, key_blk + j * (KEY_TILE // SEL_BLOCK), ns_pad - 1) == blk_lane).astype(BF16)

    o_cmp_t, o_win_t, qr_t, score_t = [], [], [], []
    for kvh in range(KV_HEADS):
        rows = slice(kvh * HEAD_DIM, (kvh + 1) * HEAD_DIM)
        q_t = (_head_cols(qa_t, kvh) * SCALE).astype(BF16)
        s = jnp.where(c_valid, _dot(kc, q_t), NEG_INF)
        e = jnp.where(c_valid, jnp.exp(s - jnp.max(s, axis=0, keepdims=True)), 0.0)
        l = jnp.sum(e, axis=0, keepdims=True)
        p_cmp = e / jnp.where(l > 0.0, l, 1.0)
        o_cmp_t.append(_dot(vct_ref[0, rows, :].astype(BF16), p_cmp.astype(BF16)))

        p_sum = p_cmp[:, 0:tq]
        for g in range(1, GROUP):
            p_sum = p_sum + p_cmp[:, g * tq:(g + 1) * tq]
        imp_t = jnp.dot(mapt_ref[...], p_sum, preferred_element_type=F32, precision=lax.Precision.HIGHEST)
        score_t.append(jnp.where((blk_t == jt_t) | (blk_t == 0), FORCE_SCORE,
                                 jnp.where(blk_t < jt_t, imp_t, NEG_INF)))

        qr_t.append((_head_cols(qar_t, kvh) * (SCALE * LOG2E)).astype(BF16))
        s = jnp.where(w_valid, _dot(wk, qr_t[kvh]), NEG_INF)
        e = jnp.exp2(s - jnp.max(s, axis=0, keepdims=True))
        o_win = _dot(jnp.concatenate([wv_t[rows], w_ones], axis=0), e.astype(BF16))
        o_win_t.append(o_win[0:HEAD_DIM] / o_win[HEAD_DIM:HEAD_DIM + 1])

    bias_t = _block_bias(_topk_mask(jnp.concatenate(score_t, axis=1), SEL_TOPK, 0))
    q_aug = [jnp.concatenate([qr_t[kvh], jnp.concatenate([bias_t[:, kvh * tq:(kvh + 1) * tq]] * GROUP, axis=1)],
                             axis=0) for kvh in range(KV_HEADS)]

    o_sel_t = _flash_t(q_aug, sk_ref, sv_ref, diag, t_col, expand_t)

    heads = []
    for kvh in range(KV_HEADS):
        for g in range(GROUP):
            h = GROUP * kvh + g
            cols = slice(g * tq, (g + 1) * tq)
            heads.append(gates_t[3 * h:3 * h + 1] * o_cmp_t[kvh][:, cols]
                         + gates_t[3 * h + 1:3 * h + 2] * o_sel_t[kvh][:, cols]
                         + gates_t[3 * h + 2:3 * h + 3] * o_win_t[kvh][:, cols])
    o_ref[0] = jnp.concatenate(heads, axis=0).T


def _nsa(qa, qar, gate, kc, vc_t, full, full_t, win, tq):
    b, t, _ = qa.shape
    l = full.shape[1]
    nc_pad = kc.shape[1]
    ns_pad = _round_up(l // SEL_BLOCK + 1, LANES)
    span = _round_up(WINDOW + tq, LANES)
    map_t = jnp.asarray(_selection_map(l // CMP_STRIDE - 1, nc_pad, ns_pad).T)
    qspec = pl.BlockSpec((1, tq, NSA_WIDTH), lambda bi, i: (bi, i, 0))
    seq = lambda a: pl.BlockSpec((1,) + a.shape[1:], lambda bi, i: (bi, 0, 0))
    return pl.pallas_call(
        functools.partial(_nsa_kernel, tq=tq, span=span),
        grid=(b, t // tq),
        in_specs=[
            qspec, qspec,
            pl.BlockSpec((1, tq, LANES), lambda bi, i: (bi, i, 0)),
            seq(kc), seq(vc_t),
            pl.BlockSpec((1, l, KV_WIDTH), lambda bi, i: (bi, 0, 2)),
            pl.BlockSpec((1, KV_WIDTH, l), lambda bi, i: (bi, 3, 0)),
            seq(win),
            _resident(map_t),
        ],
        out_specs=qspec,
        out_shape=jax.ShapeDtypeStruct((b, t, NSA_WIDTH), F32),
        compiler_params=_params(("parallel", "arbitrary")),
        name="nsa",
    )(qa, qar, gate, kc, vc_t, full, full_t, win, map_t)


def _moba_kernel(qb_ref, km_ref, mk_ref, mv_ref, o_ref, *, tq):
    i = pl.program_id(1)
    t0 = i * tq
    r = GROUP * tq
    nb_pad = km_ref.shape[1]
    t_col = t0 + (lax.broadcasted_iota(jnp.int32, (1, r), 1) & (tq - 1))
    bt_t = t_col >> MOBA_SHIFT
    nblk_t = lax.broadcasted_iota(jnp.int32, (nb_pad, 1), 0)
    nblk_lane = lax.broadcasted_iota(jnp.int32, (1, nb_pad), 1)
    diag = t0 // KEY_TILE
    qb_t = qb_ref[0].T
    km = km_ref[0]

    def expand_t(j, live):
        return jnp.broadcast_to(nblk_lane == jnp.where(live, j, nb_pad - 1), (KEY_TILE, nb_pad)).astype(BF16)

    q_t, gate_t = [], []
    for kvh in range(KV_HEADS):
        q_t.append(_head_cols(qb_t, kvh))
        g_t = jnp.dot(km, q_t[kvh], preferred_element_type=F32, precision=lax.Precision.HIGHEST)
        gate_t.append(jnp.where(nblk_t < bt_t, g_t, NEG_INF))
    chosen_t = _topk_mask(jnp.concatenate(gate_t, axis=1), MOBA_TOPK, 0)
    q_aug = []
    for kvh in range(KV_HEADS):
        own_t = jnp.where(nblk_t == bt_t, 1.0, chosen_t[:, kvh * r:(kvh + 1) * r])
        q_aug.append(jnp.concatenate([(q_t[kvh] * (SCALE * LOG2E)).astype(BF16), _block_bias(own_t)], axis=0))

    o_t = _flash_t(q_aug, mk_ref, mv_ref, diag, t_col, expand_t)
    heads = [o_t[kvh][:, g * tq:(g + 1) * tq] for kvh in range(KV_HEADS) for g in range(GROUP)]
    o_ref[0] = jnp.concatenate(heads, axis=0).T


def _moba(qb, kmean, full, full_t, tq):
    b, t, _ = qb.shape
    l = full.shape[1]
    assert l // MOBA_BLOCK < kmean.shape[1]
    qspec = pl.BlockSpec((1, tq, MOBA_WIDTH), lambda bi, i: (bi, i, 0))
    return pl.pallas_call(
        functools.partial(_moba_kernel, tq=tq),
        grid=(b, t // tq),
        in_specs=[
            qspec,
            pl.BlockSpec((1,) + kmean.shape[1:], lambda bi, i: (bi, 0, 0)),
            pl.BlockSpec((1, l, KV_WIDTH), lambda bi, i: (bi, 0, 4)),
            pl.BlockSpec((1, KV_WIDTH, l), lambda bi, i: (bi, 5, 0)),
        ],
        out_specs=qspec,
        out_shape=jax.ShapeDtypeStruct((b, t, MOBA_WIDTH), F32),
        compiler_params=_params(("parallel", "arbitrary")),
        name="moba",
    )(qb, kmean, full, full_t)


def _softmax_pv_t(s, vt):
    e = jnp.exp(s - jnp.max(s, axis=-1, keepdims=True))
    return _dot_nt(e.astype(BF16), vt) / jnp.sum(e, axis=-1, keepdims=True)


def _segment_bias(bias_rows, e_ref, block, n_keys):
    rows, seg = e_ref.shape
    parts = []
    for s in range(-(-n_keys // seg)):
        n = min(seg, n_keys - s * seg)
        b0 = s * (seg // block)
        parts.append(_dot(bias_rows[:, b0:b0 + rows], e_ref[:, 0:n]))
    return jnp.concatenate(parts, axis=1)


def _block_means_t(kt, n_keys, n_out):
    lane = lax.broadcasted_iota(jnp.int32, (1, n_out), 1)
    out = jnp.zeros((kt.shape[0], n_out), F32)
    for n in range(n_keys // MOBA_BLOCK):
        blk = kt[:, n * MOBA_BLOCK:(n + 1) * MOBA_BLOCK].astype(F32)
        out = jnp.where(lane == n, jnp.sum(blk, axis=1, keepdims=True) * (1.0 / MOBA_BLOCK), out)
    return out


def _sample_kernel(pt_ref, *refs, past_len, ts, span):
    pages = refs[:PAGES_PER_STEP]
    (new_ref, qa_ref, qar_ref, gate_ref, qb_ref, win_ref, map_ref, esel_ref, emoba_ref,
     posk_ref, w1k_ref, w2k_ref, posv_ref, w1v_ref, w2v_ref, oa_ref, ob_ref, tok_scr, kv_scr) = refs[PAGES_PER_STEP:]
    del pt_ref
    step = pl.program_id(1)
    n_steps = pl.num_programs(1)
    page = pages[0].shape[2]
    cmp_rows = 2 * KV_WIDTH

    m = past_len // CMP_STRIDE
    pitch = tok_scr.shape[1] // CMP_STRIDE
    chunks_per_page = page // CMP_STRIDE
    for j in range(PAGES_PER_STEP):
        chunk0 = (step * PAGES_PER_STEP + j) * chunks_per_page
        for slot in range(2):
            x_t = pages[j][0, slot * KV_WIDTH:(slot + 1) * KV_WIDTH, :].T
            for k in range(chunks_per_page):
                tok_scr[slot, pl.ds(chunk0 + k, CMP_STRIDE, stride=pitch), :] = (
                    x_t[k * CMP_STRIDE:(k + 1) * CMP_STRIDE])
    for k in range(past_len // (PAGES_PER_STEP * page)):
        @pl.when(step == k)
        def _(k=k):
            for j in range(PAGES_PER_STEP):
                c0 = (k * PAGES_PER_STEP + j) * page
                kv_scr[:, c0:c0 + page] = pages[j][0, cmp_rows:, :].astype(BF16)

    @pl.when(step == n_steps - 1)
    def _():
        lc = kv_scr.shape[1]
        r = GROUP * ts
        new_pad = jnp.concatenate([new_ref[0], jnp.zeros((page - ts, PAGED_WIDTH), F32)], axis=0)
        kv_scr[:, past_len:past_len + page] = new_pad.T[cmp_rows:, :].astype(BF16)

        kc_heads = _compress_rows(lambda l: tok_scr[0, l * pitch:l * pitch + m, :], posk_ref, w1k_ref, w2k_ref, m)
        vc_heads = _compress_rows(lambda l: tok_scr[1, l * pitch:l * pitch + m, :], posv_ref, w1v_ref, w2v_ref, m)

        ns_pad = map_ref.shape[1]
        nb_pad = LANES
        row = lax.broadcasted_iota(jnp.int32, (r, 1), 0)
        t_row = past_len + (row & (ts - 1))
        t_tok = past_len + lax.broadcasted_iota(jnp.int32, (ts, 1), 0)
        new_visible = past_len + lax.broadcasted_iota(jnp.int32, (1, page), 1) <= t_row

        def causal(s):
            return jnp.concatenate([s[:, 0:past_len], jnp.where(new_visible, s[:, past_len:], NEG_INF)], axis=1)

        c_end = lax.broadcasted_iota(jnp.int32, (1, m), 1) * CMP_STRIDE + CMP_LEN
        blk = lax.broadcasted_iota(jnp.int32, (1, ns_pad), 1)
        nblk = lax.broadcasted_iota(jnp.int32, (1, nb_pad), 1)
        jt = t_tok >> SEL_SHIFT
        bt = t_row >> MOBA_SHIFT
        wpos = past_len - WINDOW + lax.broadcasted_iota(jnp.int32, (1, span), 1)
        w_valid = (wpos >= 0) & (wpos <= t_row) & (wpos >= t_row - WINDOW)
        gates = gate_ref[0]

        def kv_rows(slot, kvh):
            lo = slot * KV_WIDTH + kvh * HEAD_DIM
            return kv_scr[lo:lo + HEAD_DIM, :]

        o_cmp, scores, gate_s, qf = [], [], [], []
        for kvh in range(KV_HEADS):
            q = (_stack_heads(qa_ref, kvh) * SCALE).astype(BF16)
            p_cmp = _masked_softmax(_dot_nt(q, kc_heads[kvh].astype(BF16)), c_end <= t_row + 1)
            o_cmp.append(_dot(p_cmp.astype(BF16), vc_heads[kvh].astype(BF16)))
            p_sum = p_cmp[0:ts]
            for g in range(1, GROUP):
                p_sum = p_sum + p_cmp[g * ts:(g + 1) * ts]
            imp = jnp.dot(p_sum, map_ref[...], preferred_element_type=F32, precision=lax.Precision.HIGHEST)
            scores.append(jnp.where((blk == jt) | (blk == 0), FORCE_SCORE, jnp.where(blk < jt, imp, NEG_INF)))
            qf.append(_stack_heads(qb_ref, kvh))
            g_s = jnp.dot(qf[kvh], _block_means_t(kv_rows(2, kvh), past_len, nb_pad), preferred_element_type=F32,
                          precision=lax.Precision.HIGHEST)
            gate_s.append(jnp.where(nblk < bt, g_s, NEG_INF))
        sel = _topk_rows(jnp.concatenate(scores, axis=0), SEL_TOPK)
        chosen = _topk_rows(jnp.concatenate(gate_s, axis=0), MOBA_TOPK)

        for kvh in range(KV_HEADS):
            lo = kvh * HEAD_DIM
            qr = (_stack_heads(qar_ref, kvh) * SCALE).astype(BF16)
            bias_rows = _block_bias(jnp.concatenate([sel[kvh * ts:(kvh + 1) * ts]] * GROUP, axis=0))
            s = _dot(qr, kv_rows(0, kvh)) + _segment_bias(bias_rows, esel_ref, SEL_BLOCK, lc)
            o_sel = _softmax_pv_t(causal(s), kv_rows(1, kvh))
            o_win = _window_branch(qr, win_ref, 0, span, lo, w_valid)
            _write_heads(oa_ref, kvh, ts, lambda g, h, rows: (
                gates[:, 3 * h:3 * h + 1] * o_cmp[kvh][rows] + gates[:, 3 * h + 1:3 * h + 2] * o_sel[rows]
                + gates[:, 3 * h + 2:3 * h + 3] * o_win[rows]))

            own = jnp.where(nblk == bt, 1.0, chosen[kvh * r:(kvh + 1) * r])
            s = (_dot((qf[kvh] * SCALE).astype(BF16), kv_rows(2, kvh))
                 + _segment_bias(_block_bias(own), emoba_ref, MOBA_BLOCK, lc))
            o = _softmax_pv_t(causal(s), kv_rows(3, kvh))
            _write_heads(ob_ref, kvh, ts, lambda g, h, rows: o[rows])


def _block_expansion(n_rows, block, n_keys):
    return jnp.asarray(np.arange(n_rows)[:, None] == (np.arange(n_keys)[None, :] // block), dtype=BF16)


def _sample_attn(pool_t, page_table, page_base, new_rows, qa, qar, gate, qb, win, cmp_k, cmp_v):
    bs, n_pages = page_table.shape
    page = pool_t.shape[2]
    ts = new_rows.shape[1]
    past_len = n_pages * page
    n_steps = n_pages // PAGES_PER_STEP
    assert page == LANES and n_pages % PAGES_PER_STEP == 0 and ts < CMP_STRIDE and ts & (ts - 1) == 0
    assert past_len % MOBA_BLOCK == 0 and (past_len // CMP_STRIDE) % LANES == 0
    lc = past_len + page
    ns_pad = _round_up(lc // SEL_BLOCK, LANES)
    assert lc // MOBA_BLOCK + 1 <= LANES and past_len % BIAS_SEG == 0
    span = win.shape[1]
    sel_map = jnp.asarray(_selection_map(past_len // CMP_STRIDE - 1, past_len // CMP_STRIDE, ns_pad))
    e_sel = _block_expansion(BIAS_SEG // SEL_BLOCK, SEL_BLOCK, BIAS_SEG)
    e_moba = _block_expansion(2 * (BIAS_SEG // MOBA_BLOCK), MOBA_BLOCK, BIAS_SEG)

    def page_spec(j):
        return pl.BlockSpec((1, PAGED_WIDTH, page),
                            lambda b, s, pt: (page_base + pt[b, s * PAGES_PER_STEP + j], 0, 0))

    per_seq = lambda a: pl.BlockSpec((1,) + a.shape[1:], lambda b, s, pt: (b, 0, 0))
    consts = (sel_map, e_sel, e_moba) + tuple(cmp_k) + tuple(cmp_v)
    out_spec = pl.BlockSpec((1, ts, NSA_WIDTH), lambda b, s, pt: (b, 0, 0))
    grid_spec = pltpu.PrefetchScalarGridSpec(
        num_scalar_prefetch=1,
        grid=(bs, n_steps),
        in_specs=[page_spec(j) for j in range(PAGES_PER_STEP)]
        + [per_seq(a) for a in (new_rows, qa, qar, gate, qb, win)] + [_resident(a) for a in consts],
        out_specs=[out_spec, out_spec],
        scratch_shapes=[pltpu.VMEM((2, CMP_STRIDE * (past_len // CMP_STRIDE + 8), KV_WIDTH), F32),
                        pltpu.VMEM((PAGED_WIDTH - 2 * KV_WIDTH, lc), BF16)],
    )
    return pl.pallas_call(
        functools.partial(_sample_kernel, past_len=past_len, ts=ts, span=span),
        grid_spec=grid_spec,
        out_shape=[jax.ShapeDtypeStruct((bs, ts, NSA_WIDTH), F32)] * 2,
        compiler_params=_params(("parallel", "arbitrary")),
        name="sample_attn",
    )(page_table, *([pool_t] * PAGES_PER_STEP), new_rows, qa, qar, gate, qb, win, *consts)


def _out_kernel(x_ref, oa_ref, ob_ref, za_ref, zb_ref, wout_ref, gple_ref, wg_ref, ple_ref, wp_ref, gfin_ref,
                o_ref, *, final):
    mixed = jnp.concatenate([oa_ref[...] * za_ref[...], ob_ref[...] * zb_ref[...]], axis=-1).astype(BF16)
    x1 = x_ref[...] + _dot(mixed, wout_ref[...])
    gate = _sigmoid(_dot(_rms(x1, gple_ref[...]).astype(BF16), wg_ref[...]))
    x2 = x1 + gate * _dot(ple_ref[...].astype(BF16), wp_ref[...])
    o_ref[...] = _rms(x2, gfin_ref[...]) if final else x2


def _out(x2d, oa, ob, za, zb, w_out, g_ple, w_gate, ple, w_proj, g_final, final):
    n, d_model = x2d.shape
    tm = min(256, n)
    row = lambda a: pl.BlockSpec((tm, a.shape[1]), lambda i: (i, 0))
    args = (x2d, oa, ob, za, zb, w_out, g_ple, w_gate, ple, w_proj, g_final)
    specs = [row(x2d), row(oa), row(ob), row(za), row(zb), _resident(w_out), _resident(g_ple), _resident(w_gate),
             row(ple), _resident(w_proj), _resident(g_final)]
    return pl.pallas_call(
        functools.partial(_out_kernel, final=final),
        grid=(n // tm,),
        in_specs=specs,
        out_specs=row(x2d),
        out_shape=jax.ShapeDtypeStruct((n, d_model), F32),
        compiler_params=_params(("parallel",)),
        name="out",
    )(*args)


def _mixer_layer(x, ple, pos, past, win_prev, weights, g_final, final):
    (g_mix, w_split, w_out, cmp_k, cmp_v, g_ple, w_gate, w_proj) = weights
    b, t, d_model = x.shape
    n = b * t
    qa, qar, paged, win_new, gate, za, qb, zb, *transposed = _proj(x.reshape(n, d_model), pos, g_mix, w_split)
    paged3 = paged.reshape(b, t, PAGED_WIDTH)
    win_new3 = win_new.reshape(b, t, 2 * KV_WIDTH)
    tq = min(KEY_TILE, t)
    assert KEY_TILE % tq == 0 and tq & (tq - 1) == 0
    span = _round_up(WINDOW + tq, LANES)
    qa3, qar3, gate3, qb3 = (a.reshape(b, t, -1) for a in (qa, qar, gate, qb))
    if past is None:
        assert t % KEY_TILE == 0 and span == WINDOW + tq and t >= span
        nc_pad = _round_up(t // CMP_STRIDE, LANES)
        kc = _compress(paged3, 0, *cmp_k, nc_pad, False)
        vc_t = _compress(paged3, 1, *cmp_v, nc_pad, True)
        o_a = _nsa(qa3, qar3, gate3, kc, vc_t, paged3, transposed[0], win_new3, tq)
        o_b = _moba(qb3, _kmean(paged3, LANES), paged3, transposed[0], tq)
    else:
        win = jnp.concatenate([win_prev, win_new3, jnp.zeros((b, span - WINDOW - t, 2 * KV_WIDTH), F32)], axis=1)
        o_a, o_b = _sample_attn(*past, paged3, qa3, qar3, gate3, qb3, win, cmp_k, cmp_v)
    x_next = _out(x.reshape(n, d_model), o_a.reshape(n, -1), o_b.reshape(n, -1), za, zb, w_out, g_ple, w_gate,
                  ple.reshape(n, -1), w_proj, g_final, final)
    if transposed:
        new_paged, new_win = (a.reshape(b, -1, KV_HEADS, HEAD_DIM, t).transpose(0, 4, 1, 2, 3) for a in transposed)
    else:
        new_paged, new_win = (a.reshape(b, t, -1, KV_HEADS, HEAD_DIM) for a in (paged3, win_new3))
    return x_next.reshape(b, t, d_model), new_paged, new_win


def _split_w_in(w_in):
    sizes = (NSA_WIDTH, 6 * KV_WIDTH, 3 * NSA_HEADS, NSA_WIDTH, MOBA_WIDTH, 2 * KV_WIDTH, MOBA_WIDTH)
    offs = np.concatenate([[0], np.cumsum(sizes)])
    wqa, wkva, wgate, wza, wqb, wkvb, wzb = [w_in[:, offs[k]:offs[k + 1]].astype(BF16) for k in range(len(sizes))]
    wgate = jnp.pad(wgate, ((0, 0), (0, LANES - wgate.shape[1])))
    return wqa, wkva, wgate, wza, wqb, wkvb, wzb


def kernel(x_prompt, x_sample, cache_paged_kv, cache_win_kv, page_table, p_prompt, p_sample, g_mix, w_in, w_out,
           cmp_pos_k, cmp_w1_k, cmp_w2_k, cmp_pos_v, cmp_w1_v, cmp_w2_v, g_ple, w_ple_gate, w_ple_proj, g_final):
    depth = w_in.shape[0]
    bp, tp, _ = x_prompt.shape
    bs, ts, _ = x_sample.shape
    n_pool, page = cache_paged_kv.shape[1], cache_paged_kv.shape[2]
    past_len = page_table.shape[1] * page
    wbuf = cache_win_kv.shape[2]
    pos_p = jnp.arange(tp, dtype=jnp.int32)
    pos_s = past_len + jnp.arange(ts, dtype=jnp.int32)
    pool_t = cache_paged_kv.transpose(0, 1, 3, 4, 5, 2).reshape(depth * n_pool, PAGED_WIDTH, page)
    g_fin = g_final.reshape(1, -1)
    xp, xs = x_prompt, x_sample
    new_pp, new_pw, new_sp, new_sw = [], [], [], []
    for i in range(depth):
        final = i == depth - 1
        weights = (g_mix[i].reshape(1, -1), _split_w_in(w_in[i]), w_out[i].astype(BF16),
                   _compress_weights(cmp_pos_k[i], cmp_w1_k[i], cmp_w2_k[i]),
                   _compress_weights(cmp_pos_v[i], cmp_w1_v[i], cmp_w2_v[i]),
                   g_ple[i].reshape(1, -1), w_ple_gate[i].astype(BF16), w_ple_proj[i].astype(BF16))
        win0 = jnp.zeros((bp, WINDOW, 2 * KV_WIDTH), F32)
        xp, pk, pw = _mixer_layer(xp, p_prompt[i], pos_p, None, win0, weights, g_fin, final)
        win_prev = jnp.concatenate([jnp.zeros((bs, WINDOW - wbuf, 2 * KV_WIDTH), F32),
                                    cache_win_kv[i].reshape(bs, wbuf, 2 * KV_WIDTH)], axis=1)
        xs, sk, sw = _mixer_layer(xs, p_sample[i], pos_s, (pool_t, page_table, i * n_pool), win_prev, weights,
                                  g_fin, final)
        new_pp.append(pk)
        new_pw.append(pw[:, tp - min(WINDOW, tp):])
        new_sp.append(sk)
        new_sw.append(sw)
    return (xp, xs, jnp.stack(new_pp), jnp.stack(new_pw), jnp.stack(new_sp), jnp.stack(new_sw))
```
